```python
import math
import jax
import jax.numpy as jnp
from jax import lax
import numpy as np

D_MODEL = 2048
BATCH = 8
SEQ = 4096
DEPTH = 2

GDN_HEADS = 6
GDN_DK = 128
GDN_DV = 128
GDN_CONV = 4
GDN_CHUNK = 64
DIFF_HEADS = 4
DIFF_QK_DIM = 64
DIFF_V_DIM = 128
Q_BLOCK = 128
DIL_HEADS = 6
DIL_DIM = 128
DILATED_PAIRS = ((128, 1), (512, 4), (2048, 16))
GDN_QK = GDN_HEADS * GDN_DK
GDN_V = GDN_HEADS * GDN_DV
CONV_DIM = 2 * GDN_QK + GDN_V
DIFF_QK = DIFF_HEADS * 2 * DIFF_QK_DIM
DIFF_V = DIFF_HEADS * DIFF_V_DIM
DIL_W = DIL_HEADS * DIL_DIM
MIX_WIDTH = GDN_V + DIFF_V + DIL_W
IN_SIZES = (CONV_DIM, GDN_V, GDN_HEADS, GDN_HEADS, DIFF_QK, DIFF_QK, DIFF_V, DIL_W, DIL_W, DIL_W)
IN_COLS = sum(IN_SIZES)
IN_SPLITS = tuple(int(s) for s in np.cumsum(IN_SIZES)[:-1])
N_EXPERTS = 64
N_GROUPS = 8
EXPERTS_PER_GROUP = N_EXPERTS // N_GROUPS
TOP_K = 2
D_FF_EXPERT = 512
MOE_BLOCK = 128
N_MOD = 6
EPS = 1e-6

kernel_name = 'hymba_style_gdn_diff_dilated_grouped_moe'


def rms_norm(x, w):
    xf = x.astype(jnp.float32)
    y = xf * lax.rsqrt(jnp.mean(xf * xf, axis=-1, keepdims=True) + EPS)
    return (y * w.astype(jnp.float32)).astype(x.dtype)


def l2_normalize(x):
    xf = x.astype(jnp.float32)
    return (xf * lax.rsqrt(jnp.sum(xf * xf, axis=-1, keepdims=True) + EPS)).astype(x.dtype)


def causal_depthwise_conv(x, w):
    k_size, ch = w.shape
    return lax.conv_general_dilated(
        x, w[:, None, :].astype(x.dtype), window_strides=(1,), padding=[(k_size - 1, 0)],
        dimension_numbers=('NWC', 'WIO', 'NWC'), feature_group_count=ch)


def gated_delta_rule(q, k, v, g, beta):
    out_dtype = v.dtype
    b_, s_, h_, dk = q.shape
    dv = v.shape[-1]
    cs = GDN_CHUNK
    n_ch = s_ // cs
    f32 = jnp.float32

    def chunks(t):
        t = jnp.moveaxis(t.astype(f32), 2, 1)
        return t.reshape((b_, h_, n_ch, cs) + t.shape[3:])

    q = chunks(q) * (dk ** -0.5)
    k = chunks(k)
    v = chunks(v)
    g = chunks(g)
    beta = chunks(beta)
    gc = jnp.cumsum(g, axis=-1)
    causal = jnp.tril(jnp.ones((cs, cs), bool))
    strict = jnp.tril(jnp.ones((cs, cs), bool), -1)
    decay = jnp.where(causal, jnp.exp(jnp.where(causal, gc[..., :, None] - gc[..., None, :], 0.0)), 0.0)
    kb = k * beta[..., None]
    lower = jnp.where(strict, jnp.einsum('bhncd,bhnjd->bhncj', kb, k) * decay, 0.0)
    rhs = jnp.concatenate([v * beta[..., None], kb * jnp.exp(gc)[..., None]], axis=-1)
    sol = lax.linalg.triangular_solve(lower + jnp.eye(cs, dtype=f32), rhs, left_side=True,
                                      lower=True, unit_diagonal=True)
    u, w = sol[..., :dv], sol[..., dv:]
    attn = jnp.where(causal, jnp.einsum('bhncd,bhnjd->bhncj', q, k) * decay, 0.0)
    q_dec = q * jnp.exp(gc)[..., None]
    k_dec = k * jnp.exp(gc[..., -1:] - gc)[..., None]
    chunk_dec = jnp.exp(gc[..., -1])

    def step(state, inp):
        u_n, w_n, a_n, qd_n, kd_n, cd_n = inp
        v_new = u_n - jnp.einsum('bhck,bhkv->bhcv', w_n, state)
        o_n = jnp.einsum('bhck,bhkv->bhcv', qd_n, state) + jnp.einsum('bhcj,bhjv->bhcv', a_n, v_new)
        state = state * cd_n[..., None, None] + jnp.einsum('bhck,bhcv->bhkv', kd_n, v_new)
        return state, o_n

    xs = tuple(jnp.moveaxis(t, 2, 0) for t in (u, w, attn, q_dec, k_dec, chunk_dec))
    _, o = lax.scan(step, jnp.zeros((b_, h_, dk, dv), f32), xs)
    o = jnp.moveaxis(o, 0, 2).reshape(b_, h_, s_, dv)
    return jnp.moveaxis(o, 1, 2).astype(out_dtype)


def diff_attention(q, k, v, lam):
    b_, s_, h_, _, dh = q.shape
    dv = v.shape[-1]
    nqb = s_ // Q_BLOCK
    scale = dh ** -0.5
    kt = k.transpose(0, 2, 3, 1, 4)
    vt = v.transpose(0, 2, 1, 3)
    qb = q.transpose(0, 2, 3, 1, 4).reshape(b_, h_, 2, nqb, Q_BLOCK, dh).transpose(3, 0, 1, 2, 4, 5)
    kpos = jnp.arange(s_)

    def one_block(args):
        qblk, i = args
        qpos = i * Q_BLOCK + jnp.arange(Q_BLOCK)
        s = jnp.einsum('bhmqd,bhmkd->bhmqk', qblk, kt).astype(jnp.float32) * scale
        s = jnp.where(kpos[None, :] <= qpos[:, None], s, -jnp.inf)
        p = jax.nn.softmax(s, axis=-1)
        a = p[:, :, 0] - lam * p[:, :, 1]
        return jnp.einsum('bhqk,bhkd->bhqd', a.astype(v.dtype), vt)

    o = lax.map(one_block, (qb, jnp.arange(nqb)))
    return o.transpose(1, 0, 3, 2, 4).reshape(b_, s_, h_, dv)


def dilated_window_attention(q, k, v, window, dilation):
    b_, s_, h_, dh = q.shape
    n_sub = s_ // dilation
    steps = window // dilation
    blk = steps
    nb = -(-n_sub // blk)
    lp = nb * blk

    def to_sub(t):
        t = t.reshape(b_, n_sub, dilation, h_, dh).transpose(0, 2, 3, 1, 4)
        t = jnp.pad(t, ((0, 0), (0, 0), (0, 0), (0, lp - n_sub), (0, 0)))
        return t.reshape(b_, dilation, h_, nb, blk, dh)

    qs, ks, vs = to_sub(q), to_sub(k), to_sub(v)
    prev = lambda t: jnp.pad(t, ((0, 0), (0, 0), (0, 0), (1, 0), (0, 0), (0, 0)))[:, :, :, :nb]
    kk = jnp.concatenate([prev(ks), ks], axis=4)
    vv = jnp.concatenate([prev(vs), vs], axis=4)
    s = jnp.einsum('brhnqe,brhnke->brhnqk', qs, kk).astype(jnp.float32) * (dh ** -0.5)
    qi = jnp.arange(blk)[:, None] + blk
    kj = jnp.arange(2 * blk)[None, :]
    dist = qi - kj
    band = (dist >= 0) & (dist <= steps)
    mask = band[None] & ((jnp.arange(nb)[:, None, None] > 0) | (kj >= blk)[None])
    s = jnp.where(mask, s, -jnp.inf)
    m = jnp.max(s, axis=-1, keepdims=True)
    p = jnp.exp(s - m)
    den = jnp.sum(p, axis=-1, keepdims=True)
    o = jnp.einsum('brhnqk,brhnke->brhnqe', (p / den).astype(v.dtype), vv)
    lse = (m + jnp.log(den))[..., 0]
    o = o.reshape(b_, dilation, h_, lp, dh)[:, :, :, :n_sub].transpose(0, 3, 1, 2, 4).reshape(b_, s_, h_, dh)
    lse = lse.reshape(b_, dilation, h_, lp)[..., :n_sub].transpose(0, 3, 1, 2).reshape(b_, s_, h_)
    return o, lse


def hybrid_mixer(h, layer_idx, w_in, conv_w, a_log, dt_bias, gdn_norm, diff_q_norm, diff_k_norm,
                 lam_q1, lam_k1, lam_q2, lam_k2, diff_subln, dil_q_norm, dil_k_norm, w_out):
    b_, s_, _ = h.shape
    f32 = jnp.float32
    proj = jnp.dot(h, w_in)
    qkv_a, z_a, b_a, a_a, q_b, k_b, v_b, q_c, k_c, v_c = jnp.split(proj, IN_SPLITS, axis=-1)

    qkv_a = jax.nn.silu(causal_depthwise_conv(qkv_a, conv_w))
    q_a, k_a, v_a = jnp.split(qkv_a, [GDN_QK, 2 * GDN_QK], axis=-1)
    q_a = l2_normalize(q_a.reshape(b_, s_, GDN_HEADS, GDN_DK))
    k_a = l2_normalize(k_a.reshape(b_, s_, GDN_HEADS, GDN_DK))
    v_a = v_a.reshape(b_, s_, GDN_HEADS, GDN_DV)
    beta = jax.nn.sigmoid(b_a.astype(f32))
    g = -jnp.exp(a_log.astype(f32)) * jax.nn.softplus(a_a.astype(f32) + dt_bias.astype(f32))
    o_a = gated_delta_rule(q_a, k_a, v_a, g, beta)
    o_a = rms_norm(o_a, gdn_norm) * jax.nn.silu(z_a.reshape(b_, s_, GDN_HEADS, GDN_DV))
    o_a = o_a.reshape(b_, s_, GDN_V)

    lam_init = 0.8 - 0.6 * math.exp(-0.3 * layer_idx)
    lam = (jnp.exp(jnp.sum(lam_q1.astype(f32) * lam_k1.astype(f32)))
           - jnp.exp(jnp.sum(lam_q2.astype(f32) * lam_k2.astype(f32))) + lam_init)
    qd = rms_norm(q_b.reshape(b_, s_, DIFF_HEADS, 2, DIFF_QK_DIM), diff_q_norm)
    kd = rms_norm(k_b.reshape(b_, s_, DIFF_HEADS, 2, DIFF_QK_DIM), diff_k_norm)
    vd = v_b.reshape(b_, s_, DIFF_HEADS, DIFF_V_DIM)
    o_b = diff_attention(qd, kd, vd, lam)
    o_b = (rms_norm(o_b, diff_subln) * (1.0 - lam_init)).reshape(b_, s_, DIFF_V)

    qc = rms_norm(q_c.reshape(b_, s_, DIL_HEADS, DIL_DIM), dil_q_norm)
    kc = rms_norm(k_c.reshape(b_, s_, DIL_HEADS, DIL_DIM), dil_k_norm)
    vc = v_c.reshape(b_, s_, DIL_HEADS, DIL_DIM)
    outs, lses = [], []
    for window, dilation in DILATED_PAIRS:
        o_i, lse_i = dilated_window_attention(qc, kc, vc, window, dilation)
        outs.append(o_i)
        lses.append(lse_i)
    wts = jax.nn.softmax(jnp.stack(lses, axis=0), axis=0)
    o_c = jnp.sum(wts[..., None].astype(vc.dtype) * jnp.stack(outs, axis=0), axis=0).reshape(b_, s_, DIL_W)

    return jnp.dot(jnp.concatenate([o_a, o_b, o_c], axis=-1), w_out)


def grouped_moe(h, w_router, router_bias, w_gate, w_up, w_down):
    b_, s_, d = h.shape
    t_ = b_ * s_
    ht = h.reshape(t_, d)
    f32 = jnp.float32
    scores = jax.nn.sigmoid(jnp.dot(ht, w_router).astype(f32))
    sel = (scores + router_bias.astype(f32)).reshape(t_, N_GROUPS, EXPERTS_PER_GROUP)
    group_score = jnp.sum(lax.top_k(sel, 2)[0], axis=-1)
    gidx = jnp.argmax(group_score, axis=-1)
    in_group = jnp.take_along_axis(sel, gidx[:, None, None], axis=1)[:, 0]
    _, local = lax.top_k(in_group, TOP_K)
    eidx = gidx[:, None] * EXPERTS_PER_GROUP + local
    wts = jnp.take_along_axis(scores, eidx, axis=1)
    wts = wts / jnp.sum(wts, axis=-1, keepdims=True)

    n_assign = t_ * TOP_K
    flat_e = eidx.reshape(n_assign).astype(jnp.int32)
    order = jnp.argsort(flat_e)
    se = flat_e[order]
    stok = order // TOP_K
    sw = wts.reshape(n_assign)[order]
    counts = jnp.bincount(flat_e, length=N_EXPERTS)
    padded = (counts + MOE_BLOCK - 1) // MOE_BLOCK * MOE_BLOCK
    pad_end = jnp.cumsum(padded)
    pad_start = pad_end - padded
    start = jnp.cumsum(counts) - counts
    dest = pad_start[se] + jnp.arange(n_assign) - start[se]
    n_blocks = -(-n_assign // MOE_BLOCK) + N_EXPERTS
    rows = n_blocks * MOE_BLOCK
    row_tok = jnp.zeros((rows,), jnp.int32).at[dest].set(stok)
    row_w = jnp.zeros((rows,), h.dtype).at[dest].set(sw.astype(h.dtype))
    block_e = jnp.minimum(jnp.searchsorted(pad_end, jnp.arange(n_blocks) * MOE_BLOCK, side='right'),
                          N_EXPERTS - 1)
    xb = ht[row_tok].reshape(n_blocks, MOE_BLOCK, d)

    def expert_block(args):
        xe, e = args
        a = jnp.dot(xe, w_gate[e])
        u = jnp.dot(xe, w_up[e])
        return jnp.dot(jax.nn.silu(a) * u, w_down[e])

    yb = lax.map(expert_block, (xb, block_e))
    y = jnp.zeros((t_, d), h.dtype).at[row_tok].add(yb.reshape(rows, d) * row_w[:, None])
    return y.reshape(b_, s_, d)


def setup_inputs(seed: int = 0) -> dict:
    key = jax.random.key(seed)
    ks = jax.random.split(key, 32)
    f32 = jnp.float32
    nrm = lambda k, shape, sc: jax.random.normal(k, shape, f32) * sc
    gain = lambda k, shape: 1.0 + 0.02 * jax.random.normal(k, shape, f32)
    dt = jnp.exp(jax.random.uniform(ks[9], (DEPTH, GDN_HEADS), f32, math.log(1e-3), math.log(1e-1)))
    return {
        'x': nrm(ks[0], (BATCH, SEQ, D_MODEL), 1.0),
        'c': nrm(ks[1], (BATCH, D_MODEL), 1.0),
        'norm_mix': gain(ks[2], (DEPTH, D_MODEL)),
        'norm_ffn': gain(ks[3], (DEPTH, D_MODEL)),
        'w_ada': nrm(ks[4], (DEPTH, D_MODEL, N_MOD * D_MODEL), 0.5 * D_MODEL ** -0.5),
        'b_ada': nrm(ks[5], (DEPTH, N_MOD * D_MODEL), 0.02),
        'w_in': nrm(ks[6], (DEPTH, D_MODEL, IN_COLS), D_MODEL ** -0.5),
        'conv_w': nrm(ks[7], (DEPTH, GDN_CONV, CONV_DIM), GDN_CONV ** -0.5),
        'a_log': jnp.log(jax.random.uniform(ks[8], (DEPTH, GDN_HEADS), f32, 1.0, 16.0)),
        'dt_bias': dt + jnp.log(-jnp.expm1(-dt)),
        'gdn_norm': gain(ks[10], (DEPTH, GDN_DV)),
        'diff_q_norm': gain(ks[11], (DEPTH, DIFF_QK_DIM)),
        'diff_k_norm': gain(ks[12], (DEPTH, DIFF_QK_DIM)),
        'lam_q1': nrm(ks[13], (DEPTH, DIFF_QK_DIM), 0.1),
        'lam_k1': nrm(ks[14], (DEPTH, DIFF_QK_DIM), 0.1),
        'lam_q2': nrm(ks[15], (DEPTH, DIFF_QK_DIM), 0.1),
        'lam_k2': nrm(ks[16], (DEPTH, DIFF_QK_DIM), 0.1),
        'diff_subln': gain(ks[17], (DEPTH, DIFF_V_DIM)),
        'dil_q_norm': gain(ks[18], (DEPTH, DIL_DIM)),
        'dil_k_norm': gain(ks[19], (DEPTH, DIL_DIM)),
        'w_out': nrm(ks[20], (DEPTH, MIX_WIDTH, D_MODEL), MIX_WIDTH ** -0.5),
        'w_router': nrm(ks[21], (D_MODEL, N_EXPERTS), D_MODEL ** -0.5),
        'router_bias': nrm(ks[22], (N_EXPERTS,), 0.01),
        'w_gate': nrm(ks[23], (DEPTH, N_EXPERTS, D_MODEL, D_FF_EXPERT), D_MODEL ** -0.5),
        'w_up': nrm(ks[24], (DEPTH, N_EXPERTS, D_MODEL, D_FF_EXPERT), D_MODEL ** -0.5),
        'w_down': nrm(ks[25], (DEPTH, N_EXPERTS, D_FF_EXPERT, D_MODEL), D_FF_EXPERT ** -0.5),
    }


def reference(x, c, norm_mix, norm_ffn, w_ada, b_ada, w_in, conv_w, a_log, dt_bias, gdn_norm,
              diff_q_norm, diff_k_norm, lam_q1, lam_k1, lam_q2, lam_k2, diff_subln, dil_q_norm,
              dil_k_norm, w_out, w_router, router_bias, w_gate, w_up, w_down):
    c_act = jax.nn.silu(c)
    for l in range(DEPTH):
        mod = jnp.dot(c_act, w_ada[l]) + b_ada[l]
        sh_a, sc_a, g_a, sh_f, sc_f, g_f = jnp.split(mod[:, None, :], N_MOD, axis=-1)
        h = rms_norm(x, norm_mix[l]) * (1.0 + sc_a) + sh_a
        y = hybrid_mixer(h, l, w_in[l], conv_w[l], a_log[l], dt_bias[l], gdn_norm[l], diff_q_norm[l],
                         diff_k_norm[l], lam_q1[l], lam_k1[l], lam_q2[l], lam_k2[l], diff_subln[l],
                         dil_q_norm[l], dil_k_norm[l], w_out[l])
        x = x + g_a * y
        h = rms_norm(x, norm_ffn[l]) * (1.0 + sc_f) + sh_f
        x = x + g_f * grouped_moe(h, w_router, router_bias, w_gate[l], w_up[l], w_down[l])
    return x
```

```python
import functools
import math

import jax
import jax.numpy as jnp
from jax import lax
from jax.experimental import pallas as pl
from jax.experimental.pallas import tpu as pltpu

F32 = jnp.float32
BF16 = jnp.bfloat16

LANES = 128
EPS = 1e-6
N_MOD = 6

GDN_HEADS = 6
GDN_CONV = 4
GDN_CHUNK = 256
DIFF_HEADS = 4
DIFF_QK_DIM = 64
DIL_HEADS = 6
DILATED_PAIRS = ((128, 1), (512, 4), (2048, 16))
DIL_BLOCK = 128

N_EXPERTS = 64
N_GROUPS = 8
EXPERTS_PER_GROUP = N_EXPERTS // N_GROUPS
TOP_K = 2
MOE_ROWS = 256

SLAB_QA, SLAB_KA, SLAB_VA, SLAB_ZA = 0, 6, 12, 18
SLAB_QB, SLAB_KB, SLAB_VB = 24, 28, 32
SLAB_QC, SLAB_KC, SLAB_VC = 36, 42, 48
N_SLABS = 54

VMEM_LIMIT = 56 * 1024 * 1024


def _cparams(sem, vmem=VMEM_LIMIT, **kw):
    return pltpu.CompilerParams(dimension_semantics=sem, vmem_limit_bytes=vmem, **kw)


def _silu(v):
    return v * jax.nn.sigmoid(v)


def _dot(a, b):
    return jnp.dot(a, b, preferred_element_type=F32)


def _dot_nt(a, b):
    return lax.dot_general(a, b, (((1,), (1,)), ((), ())), preferred_element_type=F32)


def _dot_tn(a, b):
    return lax.dot_general(a, b, (((0,), (0,)), ((), ())), preferred_element_type=F32)


def _ada_kernel(c_ref, w_ref, b_ref, o_ref):
    cact = _silu(c_ref[...]).astype(BF16)
    o_ref[0] = _dot(cact, w_ref[0].astype(BF16)) + b_ref[0]


def _ada(c, w_ada, b_ada):
    depth, d, n = w_ada.shape
    b = c.shape[0]
    tn = 1024
    return pl.pallas_call(
        _ada_kernel,
        grid=(depth, n // tn),
        in_specs=[
            pl.BlockSpec((b, d), lambda l, j: (0, 0)),
            pl.BlockSpec((1, d, tn), lambda l, j: (l, 0, j)),
            pl.BlockSpec((1, 1, tn), lambda l, j: (l, 0, j)),
        ],
        out_specs=pl.BlockSpec((1, b, tn), lambda l, j: (l, 0, j)),
        out_shape=jax.ShapeDtypeStruct((depth, b, n), F32),
        compiler_params=_cparams(("arbitrary", "arbitrary")),
        name="ada_mod",
    )(c, w_ada, b_ada.reshape(depth, 1, n))


def _modulated_norm(x, nw, sc, sh):
    ms = jnp.mean(x * x, axis=-1, keepdims=True)
    return (x * lax.rsqrt(ms + EPS) * nw) * (1.0 + sc) + sh


def _inproj_kernel(x_ref, nw_ref, sc_ref, sh_ref, w_ref, wba_ref, o_ref, ba_ref, h_scr, *, n_sub):
    @pl.when(pl.program_id(1) == 0)
    def _():
        h = _modulated_norm(x_ref[...], nw_ref[...], sc_ref[0], sh_ref[0]).astype(BF16)
        h_scr[...] = h
        ba_ref[...] = _dot(h, wba_ref[...])

    acc = _dot(h_scr[...], w_ref[...])
    for k in range(n_sub):
        o_ref[k] = acc[:, k * LANES:(k + 1) * LANES].astype(BF16)


def _inproj(x2, nw, sc, sh, w_main, w_ba, seq):
    t, d = x2.shape
    n = w_main.shape[1]
    tm = min(1024, seq)
    tn = 768
    n_sub = tn // LANES
    per_b = seq // tm
    return pl.pallas_call(
        functools.partial(_inproj_kernel, n_sub=n_sub),
        grid=(t // tm, n // tn),
        in_specs=[
            pl.BlockSpec((tm, d), lambda i, j: (i, 0)),
            pl.BlockSpec((1, d), lambda i, j: (0, 0)),
            pl.BlockSpec((1, 1, d), lambda i, j: (i // per_b, 0, 0)),
            pl.BlockSpec((1, 1, d), lambda i, j: (i // per_b, 0, 0)),
            pl.BlockSpec((d, tn), lambda i, j: (0, j)),
            pl.BlockSpec((d, LANES), lambda i, j: (0, 0)),
        ],
        out_specs=[
            pl.BlockSpec((n_sub, tm, LANES), lambda i, j: (j, i, 0)),
            pl.BlockSpec((tm, LANES), lambda i, j: (i, 0)),
        ],
        out_shape=[
            jax.ShapeDtypeStruct((n // LANES, t, LANES), BF16),
            jax.ShapeDtypeStruct((t, LANES), F32),
        ],
        scratch_shapes=[pltpu.VMEM((tm, d), BF16)],
        compiler_params=_cparams(("arbitrary", "arbitrary")),
        name="inproj",
    )(x2, nw, sc, sh, w_main, w_ba)


def _unit_lower_inverse(lm, xr):
    c = lm.shape[0]

    def mm(a, b):
        return _dot(a.astype(BF16), b.astype(BF16))

    d1 = jnp.where(xr < 16, lm, 0.0)
    d2 = mm(d1, d1)
    d4 = mm(d2, d2)
    d8 = mm(d4, d4)
    p = d2 - d1 - mm(d1, d2)
    p = p + d4 + mm(p, d4)
    p = p + d8 + mm(p, d8)
    x = jnp.where(xr == 0, 1.0, 0.0) + p
    blk = 32
    while blk <= c:
        e = jnp.where((xr < blk) & (xr >= blk // 2), lm, 0.0)
        x = x - mm(x, mm(e, x))
        blk *= 2
    return x


def _gdn_kernel(q_ref, k_ref, v_ref, z_ref, ba_ref, cwq_ref, cwk_ref, cwv_ref, al_ref, dtb_ref, nw_ref,
                o_ref, xq, xk, xv, u_s, w_s, a_s, qd_s, kd_s, cd_s, *, seq, chunk):
    c = chunk
    n_chunks = seq // c
    head = pl.program_id(1)
    dk = LANES

    lane1 = lax.broadcasted_iota(jnp.int32, (1, LANES), 1)
    a_exp = jnp.exp(jnp.sum(jnp.where(lane1 == head, al_ref[...], 0.0), axis=-1, keepdims=True))
    dtb = jnp.sum(jnp.where(lane1 == head, dtb_ref[...], 0.0), axis=-1, keepdims=True)

    zeros8 = jnp.zeros((8, LANES), F32)
    for src, dst in ((q_ref, xq), (k_ref, xk), (v_ref, xv)):
        dst[0:8, :] = zeros8

        def stage(i, carry, src=src, dst=dst):
            r0 = pl.multiple_of(i * c, c)
            dst[pl.ds(r0 + 8, c), :] = src[0, pl.ds(r0, c), :].astype(F32)
            return carry

        lax.fori_loop(0, n_chunks, stage, 0)

    row = lax.broadcasted_iota(jnp.int32, (c, c), 0)
    col = lax.broadcasted_iota(jnp.int32, (c, c), 1)
    causal = row >= col
    strict = row > col
    xr = row ^ col
    tril = jnp.where(causal, 1.0, 0.0).astype(BF16)
    lane_c = lax.broadcasted_iota(jnp.int32, (c, LANES), 1)

    def conv(xs, cw_ref, r0):
        acc = cw_ref[0, GDN_CONV - 1:GDN_CONV, :] * xs[pl.ds(r0 + 8, c), :]
        for back in range(1, GDN_CONV):
            tap = GDN_CONV - 1 - back
            acc = acc + cw_ref[0, tap:tap + 1, :] * xs[pl.ds(r0 + 8 - back, c), :]
        return _silu(acc)

    def l2n(t):
        return t * lax.rsqrt(jnp.sum(t * t, axis=-1, keepdims=True) + EPS)

    def intra(n, carry):
        r0 = pl.multiple_of(n * c, c)
        rows = pl.ds(r0, c)
        qn = l2n(conv(xq, cwq_ref, r0)) * (dk ** -0.5)
        kn = l2n(conv(xk, cwk_ref, r0))
        vc = conv(xv, cwv_ref, r0)
        bat = ba_ref[rows, :]
        bcol = jnp.sum(jnp.where(lane_c == head, bat, 0.0), axis=-1, keepdims=True)
        acol = jnp.sum(jnp.where(lane_c == head + GDN_HEADS, bat, 0.0), axis=-1, keepdims=True)
        beta = jax.nn.sigmoid(bcol)
        xsp = acol + dtb
        softplus = jnp.maximum(xsp, 0.0) + jnp.log1p(jnp.exp(-jnp.abs(xsp)))
        g = -a_exp * softplus
        g_rep = jnp.broadcast_to(g, (c, LANES))
        g_hi = g_rep.astype(BF16)
        g_lo = (g_rep - g_hi.astype(F32)).astype(BF16)
        gc = _dot(tril, g_hi) + _dot(tril, g_lo)
        gc_row = jnp.transpose(gc)[0:1, :]
        gc_wide = jnp.concatenate([gc] * (c // LANES), axis=1)
        diff = gc_wide - gc_row
        decay = jnp.where(causal, jnp.exp(jnp.where(causal, diff, 0.0)), 0.0)
        kb = kn * beta
        knb = kn.astype(BF16)
        lm = jnp.where(strict, _dot_nt(kb.astype(BF16), knb) * decay, 0.0)
        tinv = _unit_lower_inverse(lm, xr)
        egc = jnp.exp(gc)
        rhs = jnp.concatenate([vc * beta, kb * egc], axis=1).astype(BF16)
        sol = _dot(tinv.astype(BF16), rhs)
        u_s[rows, :] = sol[:, :LANES]
        w_s[rows, :] = sol[:, LANES:].astype(BF16)
        a_s[rows, :] = jnp.where(causal, _dot_nt(qn.astype(BF16), knb) * decay, 0.0).astype(BF16)
        qd_s[rows, :] = (qn * egc).astype(BF16)
        g_last = gc[c - 1:c, :]
        kd_s[rows, :] = (kn * jnp.exp(g_last - gc)).astype(BF16)
        cd_s[pl.ds(pl.multiple_of(n * 8, 8), 8), :] = jnp.broadcast_to(jnp.exp(g_last), (8, LANES))
        return carry

    lax.fori_loop(0, n_chunks, intra, 0)

    def inter(n, state):
        r0 = pl.multiple_of(n * c, c)
        rows = pl.ds(r0, c)
        sb = state.astype(BF16)
        v_new = u_s[rows, :] - _dot(w_s[rows, :], sb)
        vb = v_new.astype(BF16)
        o = _dot(qd_s[rows, :], sb) + _dot(a_s[rows, :], vb)
        cd = cd_s[pl.ds(pl.multiple_of(n * 8, 8), 1), :]
        new_state = state * cd + _dot_tn(kd_s[rows, :], vb)
        on = o * lax.rsqrt(jnp.mean(o * o, axis=-1, keepdims=True) + EPS) * nw_ref[...]
        z = z_ref[0, rows, :].astype(F32)
        o_ref[0, rows, :] = (on * _silu(z)).astype(BF16)
        return new_state

    lax.fori_loop(0, n_chunks, inter, jnp.zeros((dk, LANES), F32))


def _gdn(proj, ba, conv_w, a_log, dt_bias, gdn_norm, batch, seq):
    c = GDN_CHUNK
    per = seq
    cw = conv_w.reshape(GDN_CONV, 3 * GDN_HEADS, LANES).transpose(1, 0, 2)
    pad = lambda v: jnp.pad(v, (0, LANES - v.shape[0])).reshape(1, LANES)

    def slab(base):
        return pl.BlockSpec((1, per, LANES), lambda b, h, base=base: (base + h, b, 0))

    def cwspec(base):
        return pl.BlockSpec((1, GDN_CONV, LANES), lambda b, h, base=base: (base + h, 0, 0))

    vec = pl.BlockSpec((1, LANES), lambda b, h: (0, 0))
    return pl.pallas_call(
        functools.partial(_gdn_kernel, seq=seq, chunk=c),
        grid=(batch, GDN_HEADS),
        in_specs=[slab(SLAB_QA), slab(SLAB_KA), slab(SLAB_VA), slab(SLAB_ZA),
                  pl.BlockSpec((per, LANES), lambda b, h: (b, 0)),
                  cwspec(0), cwspec(GDN_HEADS), cwspec(2 * GDN_HEADS), vec, vec, vec],
        out_specs=pl.BlockSpec((1, per, LANES), lambda b, h: (h, b, 0)),
        out_shape=jax.ShapeDtypeStruct((GDN_HEADS, batch * seq, LANES), BF16),
        scratch_shapes=[
            pltpu.VMEM((seq + 8, LANES), F32), pltpu.VMEM((seq + 8, LANES), F32),
            pltpu.VMEM((seq + 8, LANES), F32),
            pltpu.VMEM((seq, LANES), F32),
            pltpu.VMEM((seq, LANES), BF16),
            pltpu.VMEM((seq, c), BF16),
            pltpu.VMEM((seq, LANES), BF16),
            pltpu.VMEM((seq, LANES), BF16),
            pltpu.VMEM((seq // c * 8, LANES), F32),
        ],
        compiler_params=_cparams(("arbitrary", "arbitrary")),
        name="gdn_mixer",
    )(proj, proj, proj, proj, ba, cw, cw, cw, pad(a_log), pad(dt_bias), gdn_norm.reshape(1, LANES))


def _diff_kernel(q_ref, k_ref, v_ref, qw_ref, kw_ref, sw_ref, lam_ref, o_ref,
                 q1_s, q2_s, kn_s, acc1, acc2, m1, l1, m2, l2, *, seq, tq, lam_init):
    dh = DIFF_QK_DIM
    n_blk = seq // tq
    lane = lax.broadcasted_iota(jnp.int32, (1, LANES), 1)
    lo = lane < dh

    lam = (jnp.exp(jnp.sum(lam_ref[0:1, :] * lam_ref[1:2, :], axis=-1, keepdims=True))
           - jnp.exp(jnp.sum(lam_ref[2:3, :] * lam_ref[3:4, :], axis=-1, keepdims=True)) + lam_init)

    def halfnorm(t, w):
        sq = t * t
        s_lo = jnp.sum(jnp.where(lo, sq, 0.0), axis=-1, keepdims=True)
        s_hi = jnp.sum(jnp.where(lo, 0.0, sq), axis=-1, keepdims=True)
        r = jnp.where(lo, lax.rsqrt(s_lo / dh + EPS), lax.rsqrt(s_hi / dh + EPS))
        return t * r * w

    def prep(i, carry):
        rows = pl.ds(pl.multiple_of(i * tq, tq), tq)
        kn_s[rows, :] = halfnorm(k_ref[0, rows, :].astype(F32), kw_ref[...]).astype(BF16)
        qn = halfnorm(q_ref[0, rows, :].astype(F32), qw_ref[...]) * (dh ** -0.5)
        q1_s[rows, :] = jnp.where(lo, qn, 0.0).astype(BF16)
        q2_s[rows, :] = jnp.where(lo, 0.0, qn).astype(BF16)
        return carry

    lax.fori_loop(0, n_blk, prep, 0)

    row = lax.broadcasted_iota(jnp.int32, (tq, tq), 0)
    col = lax.broadcasted_iota(jnp.int32, (tq, tq), 1)
    tri = row >= col

    def q_block(qi, carry):
        qrows = pl.ds(pl.multiple_of(qi * tq, tq), tq)
        maps = ((q1_s[qrows, :], m1, l1, acc1), (q2_s[qrows, :], m2, l2, acc2))
        for _, m_r, l_r, a_r in maps:
            m_r[...] = jnp.full((tq, 1), -jnp.inf, F32)
            l_r[...] = jnp.zeros((tq, 1), F32)
            a_r[...] = jnp.zeros((tq, LANES), F32)

        def kv_step(kj, masked):
            krows = pl.ds(pl.multiple_of(kj * tq, tq), tq)
            kb = kn_s[krows, :]
            vb = v_ref[0, krows, :]
            for qq, m_r, l_r, a_r in maps:
                s = _dot_nt(qq, kb)
                if masked:
                    s = jnp.where(tri, s, -jnp.inf)
                m_old = m_r[...]
                m_new = jnp.maximum(m_old, jnp.max(s, axis=-1, keepdims=True))
                alpha = jnp.exp(m_old - m_new)
                p = jnp.exp(s - m_new)
                l_r[...] = alpha * l_r[...] + jnp.sum(p, axis=-1, keepdims=True)
                a_r[...] = alpha * a_r[...] + _dot(p.astype(BF16), vb)
                m_r[...] = m_new

        def off_diag(kj, c2):
            kv_step(kj, False)
            return c2

        lax.fori_loop(0, qi, off_diag, 0)
        kv_step(qi, True)
        o = acc1[...] / l1[...] - lam * (acc2[...] / l2[...])
        o = o * lax.rsqrt(jnp.mean(o * o, axis=-1, keepdims=True) + EPS) * sw_ref[...] * (1.0 - lam_init)
        o_ref[0, qrows, :] = o.astype(BF16)
        return carry

    lax.fori_loop(0, n_blk, q_block, 0)


def _diff(proj, q_norm, k_norm, subln, lam_vecs, lam_init, batch, seq):
    tq = 256
    per = seq

    def slab(base):
        return pl.BlockSpec((1, per, LANES), lambda b, h, base=base: (base + h, b, 0))

    vec = pl.BlockSpec((1, LANES), lambda b, h: (0, 0))
    tile2 = lambda w: jnp.concatenate([w, w]).reshape(1, LANES)
    return pl.pallas_call(
        functools.partial(_diff_kernel, seq=seq, tq=tq, lam_init=lam_init),
        grid=(batch, DIFF_HEADS),
        in_specs=[slab(SLAB_QB), slab(SLAB_KB), slab(SLAB_VB), vec, vec, vec,
                  pl.BlockSpec((4, DIFF_QK_DIM), lambda b, h: (0, 0))],
        out_specs=pl.BlockSpec((1, per, LANES), lambda b, h: (h, b, 0)),
        out_shape=jax.ShapeDtypeStruct((DIFF_HEADS, batch * seq, LANES), BF16),
        scratch_shapes=[
            pltpu.VMEM((seq, LANES), BF16), pltpu.VMEM((seq, LANES), BF16), pltpu.VMEM((seq, LANES), BF16),
            pltpu.VMEM((tq, LANES), F32), pltpu.VMEM((tq, LANES), F32),
            pltpu.VMEM((tq, 1), F32), pltpu.VMEM((tq, 1), F32),
            pltpu.VMEM((tq, 1), F32), pltpu.VMEM((tq, 1), F32),
        ],
        compiler_params=_cparams(("arbitrary", "arbitrary")),
        name="diff_mixer",
    )(proj, proj, proj, tile2(q_norm), tile2(k_norm), subln.reshape(1, LANES), lam_vecs)


def _dil_kernel(q_ref, k_ref, v_ref, qw_ref, kw_ref, o_ref, qn_s, kn_s, vf_s, acc_s, m_s, l_s, *, seq):
    blk = DIL_BLOCK
    dh = LANES
    pc = 256
    n_pc = seq // pc

    def rms(t, w):
        return t * lax.rsqrt(jnp.mean(t * t, axis=-1, keepdims=True) + EPS) * w

    def prep(i, carry):
        rows = pl.ds(pl.multiple_of(i * pc, pc), pc)
        qn_s[rows, :] = rms(q_ref[0, rows, :].astype(F32), qw_ref[...]) * (dh ** -0.5)
        kn_s[rows, :] = rms(k_ref[0, rows, :].astype(F32), kw_ref[...])
        vf_s[rows, :] = v_ref[0, rows, :].astype(F32)
        return carry

    lax.fori_loop(0, n_pc, prep, 0)

    qi = lax.broadcasted_iota(jnp.int32, (blk, 2 * blk), 0)
    kj = lax.broadcasted_iota(jnp.int32, (blk, 2 * blk), 1)

    for set_idx, (window, dil) in enumerate(DILATED_PAIRS):
        steps = window // dil
        n_sub = seq // dil
        nb = n_sub // blk
        assert steps == blk and nb >= 2

        def block(idx, carry, dil=dil, nb=nb, first=(set_idx == 0)):
            r = idx // nb
            n = idx - r * nb
            kbase = jnp.maximum(n - 1, 0) * blk
            qsl = pl.ds(r + dil * blk * n, blk, stride=dil) if dil > 1 else pl.ds(pl.multiple_of(blk * n, blk), blk)
            ksl = pl.ds(r + dil * kbase, 2 * blk, stride=dil) if dil > 1 else pl.ds(pl.multiple_of(kbase, blk), 2 * blk)
            q = qn_s[qsl, :].astype(BF16)
            kk = kn_s[ksl, :].astype(BF16)
            vv = vf_s[ksl, :].astype(BF16)
            s = _dot_nt(q, kk)
            dist = (n * blk + qi) - (kbase + kj)
            s = jnp.where((dist >= 0) & (dist <= steps), s, -jnp.inf)
            m = jnp.max(s, axis=-1, keepdims=True)
            p = jnp.exp(s - m)
            l = jnp.sum(p, axis=-1, keepdims=True)
            o = _dot(p.astype(BF16), vv)
            if first:
                acc_s[qsl, :] = o
                m_s[qsl, :] = jnp.broadcast_to(m, (blk, LANES))
                l_s[qsl, :] = jnp.broadcast_to(l, (blk, LANES))
            else:
                m_old = m_s[qsl, :]
                m_new = jnp.maximum(m_old, m)
                a_old = jnp.exp(m_old - m_new)
                a_cur = jnp.exp(m - m_new)
                acc_s[qsl, :] = a_old * acc_s[qsl, :] + a_cur * o
                l_s[qsl, :] = a_old * l_s[qsl, :] + a_cur * l
                m_s[qsl, :] = m_new
            return carry

        lax.fori_loop(0, dil * nb, block, 0)

    def fin(i, carry):
        rows = pl.ds(pl.multiple_of(i * pc, pc), pc)
        o_ref[0, rows, :] = (acc_s[rows, :] / l_s[rows, :]).astype(BF16)
        return carry

    lax.fori_loop(0, n_pc, fin, 0)


def _dil(proj, q_norm, k_norm, batch, seq):
    per = seq

    def slab(base):
        return pl.BlockSpec((1, per, LANES), lambda b, h, base=base: (base + h, b, 0))

    vec = pl.BlockSpec((1, LANES), lambda b, h: (0, 0))
    return pl.pallas_call(
        functools.partial(_dil_kernel, seq=seq),
        grid=(batch, DIL_HEADS),
        in_specs=[slab(SLAB_QC), slab(SLAB_KC), slab(SLAB_VC), vec, vec],
        out_specs=pl.BlockSpec((1, per, LANES), lambda b, h: (h, b, 0)),
        out_shape=jax.ShapeDtypeStruct((DIL_HEADS, batch * seq, LANES), BF16),
        scratch_shapes=[pltpu.VMEM((seq, LANES), F32) for _ in range(6)],
        compiler_params=_cparams(("arbitrary", "arbitrary")),
        name="dil_mixer",
    )(proj, proj, proj, q_norm.reshape(1, LANES), k_norm.reshape(1, LANES))


def _outproj_kernel(oa_ref, ob_ref, oc_ref, w_ref, x_ref, g_ref, nw_ref, sc_ref, sh_ref, wr_ref,
                    xo_ref, hf_ref, s_ref, mix_s):
    off = 0
    for ref, heads in ((oa_ref, GDN_HEADS), (ob_ref, DIFF_HEADS), (oc_ref, DIL_HEADS)):
        for k in range(heads):
            mix_s[:, off:off + LANES] = ref[k]
            off += LANES
    xn = x_ref[...] + g_ref[0] * _dot(mix_s[...], w_ref[...])
    xo_ref[...] = xn
    h = _modulated_norm(xn, nw_ref[...], sc_ref[0], sh_ref[0])
    hf_ref[...] = h
    s_ref[...] = jax.nn.sigmoid(_dot(h.astype(BF16), wr_ref[...]))


def _outproj(o_a, o_b, o_c, w_out, x2, g_a, nw, sc, sh, w_router, seq):
    t, d = x2.shape
    tm = 256
    per_b = seq // tm
    ne = w_router.shape[1]
    row = pl.BlockSpec((tm, d), lambda i: (i, 0))
    bvec = pl.BlockSpec((1, 1, d), lambda i: (i // per_b, 0, 0))
    return pl.pallas_call(
        _outproj_kernel,
        grid=(t // tm,),
        in_specs=[
            pl.BlockSpec((GDN_HEADS, tm, LANES), lambda i: (0, i, 0)),
            pl.BlockSpec((DIFF_HEADS, tm, LANES), lambda i: (0, i, 0)),
            pl.BlockSpec((DIL_HEADS, tm, LANES), lambda i: (0, i, 0)),
            pl.BlockSpec((d, d), lambda i: (0, 0)),
            row, bvec,
            pl.BlockSpec((1, d), lambda i: (0, 0)),
            bvec, bvec,
            pl.BlockSpec((d, ne), lambda i: (0, 0)),
        ],
        out_specs=[row, row, pl.BlockSpec((tm, ne), lambda i: (i, 0))],
        out_shape=[jax.ShapeDtypeStruct((t, d), F32), jax.ShapeDtypeStruct((t, d), F32),
                   jax.ShapeDtypeStruct((t, ne), F32)],
        scratch_shapes=[pltpu.VMEM((tm, d), BF16)],
        compiler_params=_cparams(("arbitrary",)),
        name="outproj",
    )(o_a, o_b, o_c, w_out, x2, g_a, nw, sc, sh, w_router)


def _moe_kernel(be_ref, cnt_ref, nv_ref, tok_ref, tok_next_ref, dst_ref, rw_ref, hf_hbm, wg_ref, wu_ref, wd_ref,
                y_hbm, xbuf, ybuf, gsem, ssem, *, rows):
    i = pl.program_id(0)
    nv = nv_ref[0]
    slot = i % 2

    def gather_copy(tok_r, r, s):
        return pltpu.make_async_copy(hf_hbm.at[pl.ds(tok_r[0, 0, r], 1)], xbuf.at[s, pl.ds(r, 1)], gsem.at[s])

    def scatter_copy(r):
        return pltpu.make_async_copy(ybuf.at[pl.ds(r, 1)], y_hbm.at[pl.ds(dst_ref[0, 0, r], 1)], ssem.at[0])

    def start_gather(tok_r, count, s):
        def body(r, c2):
            gather_copy(tok_r, r, s).start()
            return c2
        lax.fori_loop(0, count, body, 0)

    def wait_scatter(count):
        def body(r, c2):
            scatter_copy(r).wait()
            return c2
        lax.fori_loop(0, count, body, 0)

    @pl.when(i == 0)
    def _():
        xbuf[...] = jnp.zeros(xbuf.shape, F32)
        start_gather(tok_ref, cnt_ref[0], 0)

    @pl.when(i + 1 < nv)
    def _():
        start_gather(tok_next_ref, cnt_ref[i + 1], 1 - slot)

    @pl.when(i < nv)
    def _():
        cnt = cnt_ref[i]

        def wait_body(r, c2):
            gather_copy(tok_ref, r, slot).wait()
            return c2
        lax.fori_loop(0, cnt, wait_body, 0)

        x = xbuf[slot].astype(BF16)
        a = _dot(x, wg_ref[0])
        u = _dot(x, wu_ref[0])
        y = _dot((_silu(a) * u).astype(BF16), wd_ref[0])

        @pl.when(i > 0)
        def _():
            wait_scatter(cnt_ref[i - 1])

        ybuf[...] = y * rw_ref[...]

        def scat_body(r, c2):
            scatter_copy(r).start()
            return c2
        lax.fori_loop(0, cnt, scat_body, 0)

        @pl.when(i == nv - 1)
        def _():
            wait_scatter(cnt)


def _moe(hf, block_e, block_cnt, n_valid, row_tok, row_dst, row_w, wg, wu, wd):
    t, d = hf.shape
    f = wg.shape[2]
    rows = MOE_ROWS
    nb = block_e.shape[0]
    tok3 = row_tok.reshape(nb, 1, rows)
    dst3 = row_dst.reshape(nb, 1, rows)
    smem_blk = lambda fn: pl.BlockSpec((1, 1, rows), fn, memory_space=pltpu.SMEM)
    grid_spec = pltpu.PrefetchScalarGridSpec(
        num_scalar_prefetch=3,
        grid=(nb,),
        in_specs=[
            smem_blk(lambda i, be, cnt, nv: (i, 0, 0)),
            smem_blk(lambda i, be, cnt, nv: (jnp.minimum(i + 1, nb - 1), 0, 0)),
            smem_blk(lambda i, be, cnt, nv: (i, 0, 0)),
            pl.BlockSpec((rows, 1), lambda i, be, cnt, nv: (i, 0)),
            pl.BlockSpec(memory_space=pl.ANY),
            pl.BlockSpec((1, d, f), lambda i, be, cnt, nv: (be[i], 0, 0)),
            pl.BlockSpec((1, d, f), lambda i, be, cnt, nv: (be[i], 0, 0)),
            pl.BlockSpec((1, f, d), lambda i, be, cnt, nv: (be[i], 0, 0)),
        ],
        out_specs=pl.BlockSpec(memory_space=pl.ANY),
        scratch_shapes=[
            pltpu.VMEM((2, rows, d), F32),
            pltpu.VMEM((rows, d), F32),
            pltpu.SemaphoreType.DMA((2,)),
            pltpu.SemaphoreType.DMA((1,)),
        ],
    )
    return pl.pallas_call(
        functools.partial(_moe_kernel, rows=rows),
        grid_spec=grid_spec,
        out_shape=jax.ShapeDtypeStruct((t * TOP_K, d), F32),
        compiler_params=_cparams(("arbitrary",), has_side_effects=True),
        name="moe_experts",
    )(block_e, block_cnt, n_valid, tok3, tok3, dst3, row_w.reshape(nb * rows, 1), hf, wg, wu, wd)


def _route(scores, router_bias):
    t = scores.shape[0]
    rows = MOE_ROWS
    sel = (scores + router_bias.astype(F32)).reshape(t, N_GROUPS, EXPERTS_PER_GROUP)
    group_score = jnp.sum(lax.top_k(sel, 2)[0], axis=-1)
    gidx = jnp.argmax(group_score, axis=-1)
    in_group = jnp.take_along_axis(sel, gidx[:, None, None], axis=1)[:, 0]
    _, local = lax.top_k(in_group, TOP_K)
    eidx = gidx[:, None] * EXPERTS_PER_GROUP + local
    wts = jnp.take_along_axis(scores, eidx, axis=1)
    wts = wts / jnp.sum(wts, axis=-1, keepdims=True)

    n_assign = t * TOP_K
    flat_e = eidx.reshape(n_assign).astype(jnp.int32)
    onehot = (flat_e[:, None] == jnp.arange(N_EXPERTS, dtype=jnp.int32)[None, :]).astype(jnp.int32)
    csum = jnp.cumsum(onehot, axis=0)
    counts = csum[-1]
    rank = jnp.sum((csum - onehot) * onehot, axis=1)
    blocks_per_e = (counts + rows - 1) // rows
    blk_end = jnp.cumsum(blocks_per_e)
    blk_start = blk_end - blocks_per_e
    dest = blk_start[flat_e] * rows + rank
    nb = n_assign // rows + N_EXPERTS
    n_rows = nb * rows
    assign = jnp.arange(n_assign, dtype=jnp.int32)
    row_dst = jnp.zeros((n_rows,), jnp.int32).at[dest].set(assign)
    row_tok = row_dst // TOP_K
    row_w = jnp.zeros((n_rows,), F32).at[dest].set(wts.reshape(n_assign))
    n_valid = blk_end[-1].astype(jnp.int32)
    bidx = jnp.arange(nb, dtype=jnp.int32)
    block_e = jnp.minimum(jnp.searchsorted(blk_end, bidx, side='right'), N_EXPERTS - 1).astype(jnp.int32)
    last_e = block_e[jnp.maximum(n_valid - 1, 0)]
    block_e = jnp.where(bidx < n_valid, block_e, last_e)
    block_cnt = jnp.clip(counts[block_e] - (bidx - blk_start[block_e]) * rows, 0, rows)
    block_cnt = jnp.where(bidx < n_valid, block_cnt, 0).astype(jnp.int32)
    return block_e, block_cnt, n_valid.reshape(1), row_tok, row_dst, row_w


def _combine_kernel(x_ref, y_ref, g_ref, o_ref, *, d):
    o_ref[...] = x_ref[...] + g_ref[0] * (y_ref[:, :d] + y_ref[:, d:])


def _combine(x2, y2, g_f, seq):
    t, d = x2.shape
    tm = 256
    per_b = seq // tm
    return pl.pallas_call(
        functools.partial(_combine_kernel, d=d),
        grid=(t // tm,),
        in_specs=[pl.BlockSpec((tm, d), lambda i: (i, 0)),
                  pl.BlockSpec((tm, TOP_K * d), lambda i: (i, 0)),
                  pl.BlockSpec((1, 1, d), lambda i: (i // per_b, 0, 0))],
        out_specs=pl.BlockSpec((tm, d), lambda i: (i, 0)),
        out_shape=jax.ShapeDtypeStruct((t, d), F32),
        compiler_params=_cparams(("arbitrary",)),
        name="moe_combine",
    )(x2, y2.reshape(t, TOP_K * d), g_f)


def kernel(x, c, norm_mix, norm_ffn, w_ada, b_ada, w_in, conv_w, a_log, dt_bias, gdn_norm, diff_q_norm, diff_k_norm, lam_q1, lam_k1, lam_q2, lam_k2, diff_subln, dil_q_norm, dil_k_norm, w_out, w_router, router_bias, w_gate, w_up, w_down):
    batch, seq, d = x.shape
    depth = w_ada.shape[0]
    t = batch * seq
    x2 = x.reshape(t, d)
    mod = _ada(c, w_ada, b_ada).reshape(depth, batch, N_MOD, 1, d)
    ba_lo = 3 * GDN_HEADS * LANES + GDN_HEADS * LANES
    ba_hi = ba_lo + 2 * GDN_HEADS
    w_router_b = w_router.astype(BF16)
    for l in range(depth):
        sh_a, sc_a, g_a, sh_f, sc_f, g_f = (mod[l, :, k] for k in range(N_MOD))
        w_main = jnp.concatenate([w_in[l, :, :ba_lo], w_in[l, :, ba_hi:]], axis=1).astype(BF16)
        w_ba = jnp.pad(w_in[l, :, ba_lo:ba_hi], ((0, 0), (0, LANES - 2 * GDN_HEADS))).astype(BF16)
        proj, ba = _inproj(x2, norm_mix[l].reshape(1, d), sc_a, sh_a, w_main, w_ba, seq)
        o_a = _gdn(proj, ba, conv_w[l], a_log[l], dt_bias[l], gdn_norm[l], batch, seq)
        lam_init = 0.8 - 0.6 * math.exp(-0.3 * l)
        lam_vecs = jnp.stack([lam_q1[l], lam_k1[l], lam_q2[l], lam_k2[l]]).astype(F32)
        o_b = _diff(proj, diff_q_norm[l], diff_k_norm[l], diff_subln[l], lam_vecs, lam_init, batch, seq)
        o_c = _dil(proj, dil_q_norm[l], dil_k_norm[l], batch, seq)
        x2, hf, scores = _outproj(o_a, o_b, o_c, w_out[l].astype(BF16), x2, g_a, norm_ffn[l].reshape(1, d),
                                  sc_f, sh_f, w_router_b, seq)
        block_e, block_cnt, n_valid, row_tok, row_dst, row_w = _route(scores, router_bias)
        y2 = _moe(hf, block_e, block_cnt, n_valid, row_tok, row_dst, row_w,
                  w_gate[l].astype(BF16), w_up[l].astype(BF16), w_down[l].astype(BF16))
        x2 = _combine(x2, y2, g_f, seq)
    return x2.reshape(batch, seq, d)
```

```python
import functools
import math

import jax
import jax.numpy as jnp
from jax import lax
from jax.experimental import pallas as pl
from jax.experimental.pallas import tpu as pltpu

F32 = jnp.float32
BF16 = jnp.bfloat16

LANES = 128
EPS = 1e-6
N_MOD = 6

GDN_HEADS = 6
GDN_CONV = 4
GDN_CHUNK = 256
GDN_LOCKSTEP = 4
DIFF_HEADS = 4
DIFF_QK_DIM = 64
DIL_HEADS = 6
DILATED_PAIRS = ((128, 1), (512, 4), (2048, 16))
DIL_BLOCK = 128

N_EXPERTS = 64
N_GROUPS = 8
EXPERTS_PER_GROUP = N_EXPERTS // N_GROUPS
TOP_K = 2
MOE_ROWS = 256

SLAB_QA, SLAB_KA, SLAB_VA, SLAB_ZA = 0, 6, 12, 18
SLAB_QB, SLAB_KB, SLAB_VB = 24, 28, 32
SLAB_QC, SLAB_KC, SLAB_VC = 36, 42, 48
N_SLABS = 54

VMEM_LIMIT = 56 * 1024 * 1024


def _cparams(sem, vmem=VMEM_LIMIT, **kw):
    return pltpu.CompilerParams(dimension_semantics=sem, vmem_limit_bytes=vmem, **kw)


def _silu(v):
    return v * jax.nn.sigmoid(v)


def _dot(a, b):
    return jnp.dot(a, b, preferred_element_type=F32)


def _dot_nt(a, b):
    return lax.dot_general(a, b, (((1,), (1,)), ((), ())), preferred_element_type=F32)


def _dot_tn(a, b):
    return lax.dot_general(a, b, (((0,), (0,)), ((), ())), preferred_element_type=F32)


def _ada_kernel(c_ref, w_ref, b_ref, o_ref):
    cact = _silu(c_ref[...]).astype(BF16)
    o_ref[0] = _dot(cact, w_ref[0].astype(BF16)) + b_ref[0]


def _ada(c, w_ada, b_ada):
    depth, d, n = w_ada.shape
    b = c.shape[0]
    tn = 1024
    return pl.pallas_call(
        _ada_kernel,
        grid=(depth, n // tn),
        in_specs=[
            pl.BlockSpec((b, d), lambda l, j: (0, 0)),
            pl.BlockSpec((1, d, tn), lambda l, j: (l, 0, j)),
            pl.BlockSpec((1, 1, tn), lambda l, j: (l, 0, j)),
        ],
        out_specs=pl.BlockSpec((1, b, tn), lambda l, j: (l, 0, j)),
        out_shape=jax.ShapeDtypeStruct((depth, b, n), F32),
        compiler_params=_cparams(("arbitrary", "arbitrary")),
        name="ada_mod",
    )(c, w_ada, b_ada.reshape(depth, 1, n))


def _modulated_norm(x, nw, sc, sh):
    ms = jnp.mean(x * x, axis=-1, keepdims=True)
    return (x * lax.rsqrt(ms + EPS) * nw) * (1.0 + sc) + sh


def _inproj_kernel(x_ref, nw_ref, sc_ref, sh_ref, w_ref, wba_ref, o_ref, ba_ref, h_scr, *, n_sub):
    @pl.when(pl.program_id(1) == 0)
    def _():
        h = _modulated_norm(x_ref[...], nw_ref[...], sc_ref[0], sh_ref[0]).astype(BF16)
        h_scr[...] = h
        ba_ref[...] = _dot(h, wba_ref[...])

    acc = _dot(h_scr[...], w_ref[...])
    for k in range(n_sub):
        o_ref[k] = acc[:, k * LANES:(k + 1) * LANES].astype(BF16)


def _inproj(x2, nw, sc, sh, w_main, w_ba, seq):
    t, d = x2.shape
    n = w_main.shape[1]
    tm = min(1024, seq)
    tn = 768
    n_sub = tn // LANES
    per_b = seq // tm
    return pl.pallas_call(
        functools.partial(_inproj_kernel, n_sub=n_sub),
        grid=(t // tm, n // tn),
        in_specs=[
            pl.BlockSpec((tm, d), lambda i, j: (i, 0)),
            pl.BlockSpec((1, d), lambda i, j: (0, 0)),
            pl.BlockSpec((1, 1, d), lambda i, j: (i // per_b, 0, 0)),
            pl.BlockSpec((1, 1, d), lambda i, j: (i // per_b, 0, 0)),
            pl.BlockSpec((d, tn), lambda i, j: (0, j)),
            pl.BlockSpec((d, LANES), lambda i, j: (0, 0)),
        ],
        out_specs=[
            pl.BlockSpec((n_sub, tm, LANES), lambda i, j: (j, i, 0)),
            pl.BlockSpec((tm, LANES), lambda i, j: (i, 0)),
        ],
        out_shape=[
            jax.ShapeDtypeStruct((n // LANES, t, LANES), BF16),
            jax.ShapeDtypeStruct((t, LANES), F32),
        ],
        scratch_shapes=[pltpu.VMEM((tm, d), BF16)],
        compiler_params=_cparams(("arbitrary", "arbitrary")),
        name="inproj",
    )(x2, nw, sc, sh, w_main, w_ba)


def _unit_lower_inverse(lms, xr):
    c = lms[0].shape[0]

    def mm(a, b):
        return [_dot(ai.astype(BF16), bi.astype(BF16)) for ai, bi in zip(a, b)]

    d1 = [jnp.where(xr < 16, lm, 0.0) for lm in lms]
    d2 = mm(d1, d1)
    d4 = mm(d2, d2)
    d8 = mm(d4, d4)
    p = [b - a - t for a, b, t in zip(d1, d2, mm(d1, d2))]
    p = [pi + di + t for pi, di, t in zip(p, d4, mm(p, d4))]
    p = [pi + di + t for pi, di, t in zip(p, d8, mm(p, d8))]
    eye = jnp.where(xr == 0, 1.0, 0.0)
    x = [eye + pi for pi in p]
    blk = 32
    while blk <= c:
        e = [jnp.where((xr < blk) & (xr >= blk // 2), lm, 0.0) for lm in lms]
        x = [xi - t for xi, t in zip(x, mm(x, mm(e, x)))]
        blk *= 2
    return x


def _gdn_kernel(q_ref, k_ref, v_ref, z_ref, ba_ref, cwq_ref, cwk_ref, cwv_ref, al_ref, dtb_ref, nw_ref,
                o_ref, xq, xk, xv, u_s, w_s, a_s, qd_s, kd_s, cd_s, *, seq, chunk):
    c = chunk
    n_chunks = seq // c
    head = pl.program_id(1)
    dk = LANES

    lane1 = lax.broadcasted_iota(jnp.int32, (1, LANES), 1)
    a_exp = jnp.exp(jnp.sum(jnp.where(lane1 == head, al_ref[...], 0.0), axis=-1, keepdims=True))
    dtb = jnp.sum(jnp.where(lane1 == head, dtb_ref[...], 0.0), axis=-1, keepdims=True)

    zeros8 = jnp.zeros((8, LANES), F32)
    for src, dst in ((q_ref, xq), (k_ref, xk), (v_ref, xv)):
        dst[0:8, :] = zeros8

        def stage(i, carry, src=src, dst=dst):
            r0 = pl.multiple_of(i * c, c)
            dst[pl.ds(r0 + 8, c), :] = src[0, pl.ds(r0, c), :].astype(F32)
            return carry

        lax.fori_loop(0, n_chunks, stage, 0)

    row = lax.broadcasted_iota(jnp.int32, (c, c), 0)
    col = lax.broadcasted_iota(jnp.int32, (c, c), 1)
    causal = row >= col
    strict = row > col
    xr = row ^ col
    tril = jnp.where(causal, 1.0, 0.0).astype(BF16)
    lane_c = lax.broadcasted_iota(jnp.int32, (c, LANES), 1)

    def conv(xs, cw_ref, r0):
        acc = cw_ref[0, GDN_CONV - 1:GDN_CONV, :] * xs[pl.ds(r0 + 8, c), :]
        for back in range(1, GDN_CONV):
            tap = GDN_CONV - 1 - back
            acc = acc + cw_ref[0, tap:tap + 1, :] * xs[pl.ds(r0 + 8 - back, c), :]
        return _silu(acc)

    def l2n(t):
        return t * lax.rsqrt(jnp.sum(t * t, axis=-1, keepdims=True) + EPS)

    def log_decay(r0):
        bat = ba_ref[pl.ds(r0, c), :]
        bcol = jnp.sum(jnp.where(lane_c == head, bat, 0.0), axis=-1, keepdims=True)
        acol = jnp.sum(jnp.where(lane_c == head + GDN_HEADS, bat, 0.0), axis=-1, keepdims=True)
        xsp = acol + dtb
        softplus = jnp.maximum(xsp, 0.0) + jnp.log1p(jnp.exp(-jnp.abs(xsp)))
        return jax.nn.sigmoid(bcol), jnp.broadcast_to(-a_exp * softplus, (c, LANES))

    def intra(step, carry):
        par = range(GDN_LOCKSTEP)
        ns = [step * GDN_LOCKSTEP + j for j in par]
        r0s = [pl.multiple_of(n * c, c) for n in ns]
        qn = [l2n(conv(xq, cwq_ref, r0)) * (dk ** -0.5) for r0 in r0s]
        kn = [l2n(conv(xk, cwk_ref, r0)) for r0 in r0s]
        vc = [conv(xv, cwv_ref, r0) for r0 in r0s]
        bg = [log_decay(r0) for r0 in r0s]
        beta = [t[0] for t in bg]
        g_hi = [t[1].astype(BF16) for t in bg]
        g_lo = [(t[1] - h.astype(F32)).astype(BF16) for t, h in zip(bg, g_hi)]
        gc = [_dot(tril, h) + _dot(tril, lo_) for h, lo_ in zip(g_hi, g_lo)]
        decay = []
        for gci in gc:
            gc_row = jnp.transpose(gci)[0:1, :]
            diff = jnp.concatenate([gci] * (c // LANES), axis=1) - gc_row
            decay.append(jnp.where(causal, jnp.exp(jnp.where(causal, diff, 0.0)), 0.0))
        kb = [k * b for k, b in zip(kn, beta)]
        knb = [k.astype(BF16) for k in kn]
        lm = [jnp.where(strict, _dot_nt(kbi.astype(BF16), ki) * dc, 0.0) for kbi, ki, dc in zip(kb, knb, decay)]
        attn = [jnp.where(causal, _dot_nt(q.astype(BF16), ki) * dc, 0.0) for q, ki, dc in zip(qn, knb, decay)]
        tinv = _unit_lower_inverse(lm, xr)
        egc = [jnp.exp(g) for g in gc]
        sol = [_dot(ti.astype(BF16), jnp.concatenate([v * b, kbi * e], axis=1).astype(BF16))
               for ti, v, b, kbi, e in zip(tinv, vc, beta, kb, egc)]
        for j in par:
            rows = pl.ds(r0s[j], c)
            u_s[rows, :] = sol[j][:, :LANES]
            w_s[rows, :] = sol[j][:, LANES:].astype(BF16)
            a_s[rows, :] = attn[j].astype(BF16)
            qd_s[rows, :] = (qn[j] * egc[j]).astype(BF16)
            g_last = gc[j][c - 1:c, :]
            kd_s[rows, :] = (kn[j] * jnp.exp(g_last - gc[j])).astype(BF16)
            cd_s[pl.ds(pl.multiple_of(ns[j] * 8, 8), 8), :] = jnp.broadcast_to(jnp.exp(g_last), (8, LANES))
        return carry

    lax.fori_loop(0, n_chunks // GDN_LOCKSTEP, intra, 0)

    def inter(n, state):
        r0 = pl.multiple_of(n * c, c)
        rows = pl.ds(r0, c)
        sb = state.astype(BF16)
        v_new = u_s[rows, :] - _dot(w_s[rows, :], sb)
        vb = v_new.astype(BF16)
        o = _dot(qd_s[rows, :], sb) + _dot(a_s[rows, :], vb)
        cd = cd_s[pl.ds(pl.multiple_of(n * 8, 8), 1), :]
        new_state = state * cd + _dot_tn(kd_s[rows, :], vb)
        on = o * lax.rsqrt(jnp.mean(o * o, axis=-1, keepdims=True) + EPS) * nw_ref[...]
        z = z_ref[0, rows, :].astype(F32)
        o_ref[0, rows, :] = (on * _silu(z)).astype(BF16)
        return new_state

    lax.fori_loop(0, n_chunks, inter, jnp.zeros((dk, LANES), F32))


def _gdn(proj, ba, conv_w, a_log, dt_bias, gdn_norm, batch, seq):
    c = GDN_CHUNK
    per = seq
    cw = conv_w.reshape(GDN_CONV, 3 * GDN_HEADS, LANES).transpose(1, 0, 2)
    pad = lambda v: jnp.pad(v, (0, LANES - v.shape[0])).reshape(1, LANES)

    def slab(base):
        return pl.BlockSpec((1, per, LANES), lambda b, h, base=base: (base + h, b, 0))

    def cwspec(base):
        return pl.BlockSpec((1, GDN_CONV, LANES), lambda b, h, base=base: (base + h, 0, 0))

    vec = pl.BlockSpec((1, LANES), lambda b, h: (0, 0))
    return pl.pallas_call(
        functools.partial(_gdn_kernel, seq=seq, chunk=c),
        grid=(batch, GDN_HEADS),
        in_specs=[slab(SLAB_QA), slab(SLAB_KA), slab(SLAB_VA), slab(SLAB_ZA),
                  pl.BlockSpec((per, LANES), lambda b, h: (b, 0)),
                  cwspec(0), cwspec(GDN_HEADS), cwspec(2 * GDN_HEADS), vec, vec, vec],
        out_specs=pl.BlockSpec((1, per, LANES), lambda b, h: (h, b, 0)),
        out_shape=jax.ShapeDtypeStruct((GDN_HEADS, batch * seq, LANES), BF16),
        scratch_shapes=[
            pltpu.VMEM((seq + 8, LANES), F32), pltpu.VMEM((seq + 8, LANES), F32),
            pltpu.VMEM((seq + 8, LANES), F32),
            pltpu.VMEM((seq, LANES), F32),
            pltpu.VMEM((seq, LANES), BF16),
            pltpu.VMEM((seq, c), BF16),
            pltpu.VMEM((seq, LANES), BF16),
            pltpu.VMEM((seq, LANES), BF16),
            pltpu.VMEM((seq // c * 8, LANES), F32),
        ],
        compiler_params=_cparams(("arbitrary", "arbitrary")),
        name="gdn_mixer",
    )(proj, proj, proj, proj, ba, cw, cw, cw, pad(a_log), pad(dt_bias), gdn_norm.reshape(1, LANES))


def _diff_kernel(q_ref, k_ref, v_ref, qw_ref, kw_ref, sw_ref, lam_ref, o_ref,
                 q1_s, q2_s, kn_s, ve_s, sc1, sc2, acc1, acc2, m1, m2, *, seq, tq, tk, lam_init):
    dh = DIFF_QK_DIM
    n_blk = seq // tq
    per_q = tq // tk
    lane = lax.broadcasted_iota(jnp.int32, (1, LANES), 1)
    lo = lane < dh

    lam = (jnp.exp(jnp.sum(lam_ref[0:1, :] * lam_ref[1:2, :], axis=-1, keepdims=True))
           - jnp.exp(jnp.sum(lam_ref[2:3, :] * lam_ref[3:4, :], axis=-1, keepdims=True)) + lam_init)

    def halfnorm(t, w):
        sq = t * t
        s_lo = jnp.sum(jnp.where(lo, sq, 0.0), axis=-1, keepdims=True)
        s_hi = jnp.sum(jnp.where(lo, 0.0, sq), axis=-1, keepdims=True)
        r = jnp.where(lo, lax.rsqrt(s_lo / dh + EPS), lax.rsqrt(s_hi / dh + EPS))
        return t * r * w

    ones_blk = jnp.ones((tk, LANES), BF16)

    def prep(i, carry):
        rows = pl.ds(pl.multiple_of(i * tk, tk), tk)
        kn_s[rows, :] = halfnorm(k_ref[0, rows, :].astype(F32), kw_ref[...]).astype(BF16)
        qn = halfnorm(q_ref[0, rows, :].astype(F32), qw_ref[...]) * (dh ** -0.5)
        q1_s[rows, :] = jnp.where(lo, qn, 0.0).astype(BF16)
        q2_s[rows, :] = jnp.where(lo, 0.0, qn).astype(BF16)
        ve_s[rows, 0:LANES] = v_ref[0, rows, :]
        ve_s[rows, LANES:2 * LANES] = ones_blk
        return carry

    lax.fori_loop(0, seq // tk, prep, 0)

    row = lax.broadcasted_iota(jnp.int32, (tq, tk), 0)
    col = lax.broadcasted_iota(jnp.int32, (tq, tk), 1)

    n_parts = tk // LANES

    def q_block(qi, carry):
        qrows = pl.ds(pl.multiple_of(qi * tq, tq), tq)
        maps = ((q1_s[qrows, :], sc1, m1, acc1), (q2_s[qrows, :], sc2, m2, acc2))
        for _, _, m_r, a_r in maps:
            m_r[...] = jnp.full((tq, LANES), -jnp.inf, F32)
            a_r[...] = jnp.zeros((tq, 2 * LANES), F32)

        def score_step(kj, masked):
            kb = kn_s[pl.ds(pl.multiple_of(kj * tk, tk), tk), :]
            for qq, s_c, m_r, _ in maps:
                s = _dot_nt(qq, kb)
                if masked:
                    s = jnp.where(row + qi * tq >= col + kj * tk, s, -jnp.inf)
                s_c[kj] = s
                smax = m_r[...]
                for part in range(n_parts):
                    smax = jnp.maximum(smax, s[:, part * LANES:(part + 1) * LANES])
                m_r[...] = smax

        def off_diag(kj, c2):
            score_step(kj, False)
            return c2

        lax.fori_loop(0, qi * per_q, off_diag, 0)
        for dblk in range(per_q):
            score_step(qi * per_q + dblk, True)
        for _, _, m_r, _ in maps:
            m_r[...] = jnp.broadcast_to(jnp.max(m_r[...], axis=-1, keepdims=True), (tq, LANES))

        def value_step(kj, c2):
            vb = ve_s[pl.ds(pl.multiple_of(kj * tk, tk), tk), :]
            for _, s_c, m_r, a_r in maps:
                mrep = m_r[...]
                s = s_c[kj]
                p = jnp.concatenate(
                    [jnp.exp(s[:, part * LANES:(part + 1) * LANES] - mrep) for part in range(n_parts)], axis=1)
                a_r[...] += _dot(p.astype(BF16), vb)
            return c2

        lax.fori_loop(0, (qi + 1) * per_q, value_step, 0)
        o = (acc1[:, 0:LANES] / acc1[:, LANES:2 * LANES]
             - lam * (acc2[:, 0:LANES] / acc2[:, LANES:2 * LANES]))
        o = o * lax.rsqrt(jnp.mean(o * o, axis=-1, keepdims=True) + EPS) * sw_ref[...] * (1.0 - lam_init)
        o_ref[0, qrows, :] = o.astype(BF16)
        return carry

    lax.fori_loop(0, n_blk, q_block, 0)


def _diff(proj, q_norm, k_norm, subln, lam_vecs, lam_init, batch, seq):
    tq = 512
    tk = 256
    per = seq

    def slab(base):
        return pl.BlockSpec((1, per, LANES), lambda b, h, base=base: (base + h, b, 0))

    vec = pl.BlockSpec((1, LANES), lambda b, h: (0, 0))
    tile2 = lambda w: jnp.concatenate([w, w]).reshape(1, LANES)
    return pl.pallas_call(
        functools.partial(_diff_kernel, seq=seq, tq=tq, tk=tk, lam_init=lam_init),
        grid=(batch, DIFF_HEADS),
        in_specs=[slab(SLAB_QB), slab(SLAB_KB), slab(SLAB_VB), vec, vec, vec,
                  pl.BlockSpec((4, DIFF_QK_DIM), lambda b, h: (0, 0))],
        out_specs=pl.BlockSpec((1, per, LANES), lambda b, h: (h, b, 0)),
        out_shape=jax.ShapeDtypeStruct((DIFF_HEADS, batch * seq, LANES), BF16),
        scratch_shapes=[
            pltpu.VMEM((seq, LANES), BF16), pltpu.VMEM((seq, LANES), BF16), pltpu.VMEM((seq, LANES), BF16),
            pltpu.VMEM((seq, 2 * LANES), BF16),
            pltpu.VMEM((seq // tk, tq, tk), F32), pltpu.VMEM((seq // tk, tq, tk), F32),
            pltpu.VMEM((tq, 2 * LANES), F32), pltpu.VMEM((tq, 2 * LANES), F32),
            pltpu.VMEM((tq, LANES), F32), pltpu.VMEM((tq, LANES), F32),
        ],
        compiler_params=_cparams(("arbitrary", "arbitrary")),
        name="diff_mixer",
    )(proj, proj, proj, tile2(q_norm), tile2(k_norm), subln.reshape(1, LANES), lam_vecs)


def _dil_kernel(q_ref, k_ref, v_ref, qw_ref, kw_ref, o_ref, qn_s, kn_s, vf_s, sc_s, mb_s, acc_s, l_s, m_s, *, seq):
    blk = DIL_BLOCK
    dh = LANES
    pc = 256
    n_pc = seq // pc
    unroll = 4

    def rms(t, w):
        return t * lax.rsqrt(jnp.mean(t * t, axis=-1, keepdims=True) + EPS) * w

    ones_kv = jnp.ones((2 * blk, LANES), BF16)

    def prep(i, carry):
        rows = pl.ds(pl.multiple_of(i * pc, pc), pc)
        qn_s[rows, :] = rms(q_ref[0, rows, :].astype(F32), qw_ref[...]) * (dh ** -0.5)
        kn_s[rows, :] = rms(k_ref[0, rows, :].astype(F32), kw_ref[...])
        vf_s[rows, :] = v_ref[0, rows, :].astype(F32)
        return carry

    lax.fori_loop(0, n_pc, prep, 0)

    qi = lax.broadcasted_iota(jnp.int32, (blk, 2 * blk), 0)
    kj = lax.broadcasted_iota(jnp.int32, (blk, 2 * blk), 1)

    for set_idx, (window, dil) in enumerate(DILATED_PAIRS):
        steps = window // dil
        n_sub = seq // dil
        nb = n_sub // blk
        assert steps == blk and nb >= 2 and (dil * nb) % unroll == 0

        def slices(idx, dil=dil, nb=nb):
            r = idx // nb
            n = idx - r * nb
            kbase = jnp.maximum(n - 1, 0) * blk
            if dil > 1:
                return n, kbase, pl.ds(r + dil * blk * n, blk, stride=dil), pl.ds(r + dil * kbase, 2 * blk, stride=dil)
            return n, kbase, pl.ds(pl.multiple_of(blk * n, blk), blk), pl.ds(pl.multiple_of(kbase, blk), 2 * blk)

        def score_block(idx, carry):
            n, kbase, qsl, ksl = slices(idx)
            s = _dot_nt(qn_s[qsl, :].astype(BF16), kn_s[ksl, :].astype(BF16))
            dist = (n * blk + qi) - (kbase + kj)
            s = jnp.where((dist >= 0) & (dist <= steps), s, -jnp.inf)
            sc_s[idx] = s
            m = jnp.max(jnp.maximum(s[:, 0:blk], s[:, blk:2 * blk]), axis=-1, keepdims=True)
            mb_s[idx] = jnp.broadcast_to(m, (blk, LANES))
            return carry

        lax.fori_loop(0, dil * nb, score_block, 0, unroll=unroll)

        def value_block(idx, carry, first=(set_idx == 0)):
            _, _, qsl, ksl = slices(idx)
            s = sc_s[idx]
            m = mb_s[idx]
            p = jnp.concatenate([jnp.exp(s[:, 0:blk] - m), jnp.exp(s[:, blk:2 * blk] - m)], axis=1)
            ve = jnp.concatenate([vf_s[ksl, :].astype(BF16), ones_kv], axis=1)
            ol = _dot(p.astype(BF16), ve)
            o = ol[:, 0:LANES]
            l = ol[:, LANES:2 * LANES]
            if first:
                acc_s[qsl, :] = o
                l_s[qsl, :] = l
                m_s[qsl, :] = m
            else:
                m_old = m_s[qsl, :]
                m_new = jnp.maximum(m_old, m)
                a_old = jnp.exp(m_old - m_new)
                a_cur = jnp.exp(m - m_new)
                acc_s[qsl, :] = a_old * acc_s[qsl, :] + a_cur * o
                l_s[qsl, :] = a_old * l_s[qsl, :] + a_cur * l
                m_s[qsl, :] = m_new
            return carry

        lax.fori_loop(0, dil * nb, value_block, 0, unroll=unroll)

    def fin(i, carry):
        rows = pl.ds(pl.multiple_of(i * pc, pc), pc)
        o_ref[0, rows, :] = (acc_s[rows, :] / l_s[rows, :]).astype(BF16)
        return carry

    lax.fori_loop(0, n_pc, fin, 0)


def _dil(proj, q_norm, k_norm, batch, seq):
    per = seq

    def slab(base):
        return pl.BlockSpec((1, per, LANES), lambda b, h, base=base: (base + h, b, 0))

    vec = pl.BlockSpec((1, LANES), lambda b, h: (0, 0))
    return pl.pallas_call(
        functools.partial(_dil_kernel, seq=seq),
        grid=(batch, DIL_HEADS),
        in_specs=[slab(SLAB_QC), slab(SLAB_KC), slab(SLAB_VC), vec, vec],
        out_specs=pl.BlockSpec((1, per, LANES), lambda b, h: (h, b, 0)),
        out_shape=jax.ShapeDtypeStruct((DIL_HEADS, batch * seq, LANES), BF16),
        scratch_shapes=[
            pltpu.VMEM((seq, LANES), F32), pltpu.VMEM((seq, LANES), F32),
            pltpu.VMEM((seq, LANES), F32),
            pltpu.VMEM((seq // DIL_BLOCK, DIL_BLOCK, 2 * DIL_BLOCK), F32),
            pltpu.VMEM((seq // DIL_BLOCK, DIL_BLOCK, LANES), F32),
            pltpu.VMEM((seq, LANES), F32), pltpu.VMEM((seq, LANES), F32),
            pltpu.VMEM((seq, LANES), F32),
        ],
        compiler_params=_cparams(("arbitrary", "arbitrary")),
        name="dil_mixer",
    )(proj, proj, proj, q_norm.reshape(1, LANES), k_norm.reshape(1, LANES))


def _route_tile(sc, bias, run):
    ne, tm = sc.shape
    epg = EXPERTS_PER_GROUP
    sel = sc + bias
    jrow = lax.broadcasted_iota(jnp.int32, (epg, tm), 0)
    gs, i1s, i2s = [], [], []
    for g in range(N_GROUPS):
        sg = sel[g * epg:(g + 1) * epg, :]
        m1 = jnp.max(sg, axis=0, keepdims=True)
        i1 = jnp.min(jnp.where(sg == m1, jrow, epg), axis=0, keepdims=True)
        rest = jnp.where(jrow == i1, -jnp.inf, sg)
        m2 = jnp.max(rest, axis=0, keepdims=True)
        i2 = jnp.min(jnp.where(rest == m2, jrow, epg), axis=0, keepdims=True)
        gs.append(m1 + m2)
        i1s.append(i1)
        i2s.append(i2)
    gmax = functools.reduce(jnp.maximum, gs)
    gidx = jnp.full((1, tm), N_GROUPS - 1, jnp.int32)
    for g in reversed(range(N_GROUPS - 1)):
        gidx = jnp.where(gs[g] == gmax, g, gidx)
    loc1 = jnp.zeros((1, tm), jnp.int32)
    loc2 = jnp.zeros((1, tm), jnp.int32)
    for g in range(N_GROUPS):
        loc1 = jnp.where(gidx == g, i1s[g], loc1)
        loc2 = jnp.where(gidx == g, i2s[g], loc2)
    e1 = gidx * epg + loc1
    e2 = gidx * epg + loc2
    erow = lax.broadcasted_iota(jnp.int32, (ne, tm), 0)
    oh1 = erow == e1
    oh2 = erow == e2
    s1 = jnp.sum(jnp.where(oh1, sc, 0.0), axis=0, keepdims=True)
    s2 = jnp.sum(jnp.where(oh2, sc, 0.0), axis=0, keepdims=True)
    den = s1 + s2
    tr = lax.broadcasted_iota(jnp.int32, (tm, tm), 0)
    tc = lax.broadcasted_iota(jnp.int32, (tm, tm), 1)
    before = jnp.where(tr < tc, 1.0, 0.0).astype(BF16)
    f1 = jnp.where(oh1, 1.0, 0.0)
    f2 = jnp.where(oh2, 1.0, 0.0)
    pre1 = _dot(f1.astype(BF16), before)
    pre2 = _dot(f2.astype(BF16), before)
    tot1 = jnp.sum(f1, axis=1, keepdims=True)
    tot2 = jnp.sum(f2, axis=1, keepdims=True)
    r1 = jnp.sum(jnp.where(oh1, pre1 + run, 0.0), axis=0, keepdims=True)
    r2 = jnp.sum(jnp.where(oh2, pre2 + (run + tot1), 0.0), axis=0, keepdims=True)
    return e1, e2, s1 / den, s2 / den, r1, r2, run + tot1 + tot2


def _outproj_kernel(oa_ref, ob_ref, oc_ref, w_ref, x_ref, g_ref, nw_ref, sc_ref, sh_ref, wrt_ref, rb_ref,
                    xo_ref, hf_ref, ri_ref, rw_ref, cnt_ref, mix_s, run_s):
    @pl.when(pl.program_id(0) == 0)
    def _():
        run_s[...] = jnp.zeros(run_s.shape, F32)

    off = 0
    for ref, heads in ((oa_ref, GDN_HEADS), (ob_ref, DIFF_HEADS), (oc_ref, DIL_HEADS)):
        for k in range(heads):
            mix_s[:, off:off + LANES] = ref[k]
            off += LANES
    xn = x_ref[...] + g_ref[0] * _dot(mix_s[...], w_ref[...])
    xo_ref[...] = xn
    h = _modulated_norm(xn, nw_ref[...], sc_ref[0], sh_ref[0])
    hf_ref[...] = h
    scores_t = jax.nn.sigmoid(_dot_nt(wrt_ref[...], h.astype(BF16)))
    e1, e2, w1, w2, r1, r2, run = _route_tile(scores_t, rb_ref[...], run_s[...])
    run_s[...] = run
    cnt_ref[...] = jnp.broadcast_to(run, cnt_ref.shape)
    tm = e1.shape[1]
    ri_ref[...] = jnp.concatenate([e1, e2, r1.astype(jnp.int32), r2.astype(jnp.int32),
                                   jnp.zeros((4, tm), jnp.int32)], axis=0)
    rw_ref[...] = jnp.concatenate([w1, w2, jnp.zeros((6, tm), F32)], axis=0)


def _outproj(o_a, o_b, o_c, w_out, x2, g_a, nw, sc, sh, w_router_t, router_bias, seq):
    t, d = x2.shape
    tm = 256
    per_b = seq // tm
    ne = w_router_t.shape[0]
    row = pl.BlockSpec((tm, d), lambda i: (i, 0))
    bvec = pl.BlockSpec((1, 1, d), lambda i: (i // per_b, 0, 0))
    return pl.pallas_call(
        _outproj_kernel,
        grid=(t // tm,),
        in_specs=[
            pl.BlockSpec((GDN_HEADS, tm, LANES), lambda i: (0, i, 0)),
            pl.BlockSpec((DIFF_HEADS, tm, LANES), lambda i: (0, i, 0)),
            pl.BlockSpec((DIL_HEADS, tm, LANES), lambda i: (0, i, 0)),
            pl.BlockSpec((d, d), lambda i: (0, 0)),
            row, bvec,
            pl.BlockSpec((1, d), lambda i: (0, 0)),
            bvec, bvec,
            pl.BlockSpec((ne, d), lambda i: (0, 0)),
            pl.BlockSpec((ne, 1), lambda i: (0, 0)),
        ],
        out_specs=[row, row,
                   pl.BlockSpec((8, tm), lambda i: (0, i)),
                   pl.BlockSpec((8, tm), lambda i: (0, i)),
                   pl.BlockSpec((ne, LANES), lambda i: (0, 0))],
        out_shape=[jax.ShapeDtypeStruct((t, d), F32), jax.ShapeDtypeStruct((t, d), F32),
                   jax.ShapeDtypeStruct((8, t), jnp.int32), jax.ShapeDtypeStruct((8, t), F32),
                   jax.ShapeDtypeStruct((ne, LANES), F32)],
        scratch_shapes=[pltpu.VMEM((tm, d), BF16), pltpu.VMEM((ne, 1), F32)],
        compiler_params=_cparams(("arbitrary",)),
        name="outproj",
    )(o_a, o_b, o_c, w_out, x2, g_a, nw, sc, sh, w_router_t, router_bias.reshape(ne, 1).astype(F32))


DMA_GROUP = 8


def _moe_kernel(be_ref, cnt_ref, nv_ref, tok_ref, tok_next_ref, dst_ref, rw_ref, hf_hbm, wg_ref, wu_ref, wd_ref,
                y_hbm, xbuf, ybuf, gsem, ssem):
    i = pl.program_id(0)
    nv = nv_ref[0]
    slot = i % 2

    def gather_copy(tok_r, r, s):
        return pltpu.make_async_copy(hf_hbm.at[pl.ds(tok_r[0, 0, r], 1)], xbuf.at[s, pl.ds(r, 1)], gsem.at[s])

    def scatter_copy(r):
        return pltpu.make_async_copy(ybuf.at[pl.ds(r, 1)], y_hbm.at[pl.ds(dst_ref[0, 0, r], 1)], ssem.at[0])

    def for_rows(count, fn, round_up):
        full = (count + (DMA_GROUP - 1 if round_up else 0)) // DMA_GROUP

        def group(gi, c2):
            base = gi * DMA_GROUP
            for k in range(DMA_GROUP):
                fn(base + k)
            return c2

        lax.fori_loop(0, full, group, 0)
        if not round_up:
            def single(r, c2):
                fn(r)
                return c2
            lax.fori_loop(full * DMA_GROUP, count, single, 0)

    @pl.when(i == 0)
    def _():
        xbuf[...] = jnp.zeros(xbuf.shape, F32)
        for_rows(cnt_ref[0], lambda r: gather_copy(tok_ref, r, 0).start(), True)

    @pl.when(i + 1 < nv)
    def _():
        for_rows(cnt_ref[i + 1], lambda r: gather_copy(tok_next_ref, r, 1 - slot).start(), True)

    @pl.when(i < nv)
    def _():
        cnt = cnt_ref[i]
        for_rows(cnt, lambda r: gather_copy(tok_ref, r, slot).wait(), True)
        x = xbuf[slot].astype(BF16)
        a = _dot(x, wg_ref[0])
        u = _dot(x, wu_ref[0])
        y = _dot((_silu(a) * u).astype(BF16), wd_ref[0])

        @pl.when(i > 0)
        def _():
            for_rows(cnt_ref[i - 1], lambda r: scatter_copy(r).wait(), False)

        ybuf[...] = y * rw_ref[...]
        for_rows(cnt, lambda r: scatter_copy(r).start(), False)

        @pl.when(i == nv - 1)
        def _():
            for_rows(cnt, lambda r: scatter_copy(r).wait(), False)


def _moe(hf, block_e, block_cnt, n_valid, row_tok, row_dst, row_w, wg, wu, wd):
    t, d = hf.shape
    f = wg.shape[2]
    rows = MOE_ROWS
    nb = block_e.shape[0]
    tok3 = row_tok.reshape(nb, 1, rows)
    dst3 = row_dst.reshape(nb, 1, rows)
    smem_blk = lambda fn: pl.BlockSpec((1, 1, rows), fn, memory_space=pltpu.SMEM)
    grid_spec = pltpu.PrefetchScalarGridSpec(
        num_scalar_prefetch=3,
        grid=(nb,),
        in_specs=[
            smem_blk(lambda i, be, cnt, nv: (i, 0, 0)),
            smem_blk(lambda i, be, cnt, nv: (jnp.minimum(i + 1, nb - 1), 0, 0)),
            smem_blk(lambda i, be, cnt, nv: (i, 0, 0)),
            pl.BlockSpec((rows, 1), lambda i, be, cnt, nv: (i, 0)),
            pl.BlockSpec(memory_space=pl.ANY),
            pl.BlockSpec((1, d, f), lambda i, be, cnt, nv: (be[i], 0, 0)),
            pl.BlockSpec((1, d, f), lambda i, be, cnt, nv: (be[i], 0, 0)),
            pl.BlockSpec((1, f, d), lambda i, be, cnt, nv: (be[i], 0, 0)),
        ],
        out_specs=pl.BlockSpec(memory_space=pl.ANY),
        scratch_shapes=[
            pltpu.VMEM((2, rows, d), F32),
            pltpu.VMEM((rows, d), F32),
            pltpu.SemaphoreType.DMA((2,)),
            pltpu.SemaphoreType.DMA((1,)),
        ],
    )
    return pl.pallas_call(
        _moe_kernel,
        grid_spec=grid_spec,
        out_shape=jax.ShapeDtypeStruct((TOP_K * t, d), F32),
        compiler_params=_cparams(("arbitrary",), has_side_effects=True, disable_bounds_checks=True),
        name="moe_experts",
    )(block_e, block_cnt, n_valid, tok3, tok3, dst3, row_w.reshape(nb * rows, 1), hf, wg, wu, wd)


def _block_layout(route_i, route_w, counts_f):
    t = route_i.shape[1]
    rows = MOE_ROWS
    n_assign = TOP_K * t
    counts = counts_f[:, 0].astype(jnp.int32)
    blocks_per_e = (counts + rows - 1) // rows
    blk_end = jnp.cumsum(blocks_per_e)
    blk_start = blk_end - blocks_per_e
    flat_e = route_i[0:TOP_K].reshape(n_assign)
    dest = blk_start[flat_e] * rows + route_i[TOP_K:2 * TOP_K].reshape(n_assign)
    nb = n_assign // rows + N_EXPERTS
    n_rows = nb * rows
    assign = jnp.arange(n_assign, dtype=jnp.int32)
    row_dst = jnp.zeros((n_rows,), jnp.int32).at[dest].set(assign)
    row_tok = row_dst % t
    row_w = jnp.zeros((n_rows,), F32).at[dest].set(route_w[0:TOP_K].reshape(n_assign))
    n_valid = blk_end[-1].astype(jnp.int32)
    bidx = jnp.arange(nb, dtype=jnp.int32)
    block_e = jnp.minimum(jnp.searchsorted(blk_end, bidx, side='right'), N_EXPERTS - 1).astype(jnp.int32)
    last_e = block_e[jnp.maximum(n_valid - 1, 0)]
    block_e = jnp.where(bidx < n_valid, block_e, last_e)
    block_cnt = jnp.clip(counts[block_e] - (bidx - blk_start[block_e]) * rows, 0, rows)
    block_cnt = jnp.where(bidx < n_valid, block_cnt, 0).astype(jnp.int32)
    return block_e, block_cnt, n_valid.reshape(1), row_tok, row_dst, row_w


def _combine_kernel(x_ref, y0_ref, y1_ref, g_ref, o_ref):
    o_ref[...] = x_ref[...] + g_ref[0] * (y0_ref[0] + y1_ref[0])


def _combine(x2, y2, g_f, seq):
    t, d = x2.shape
    tm = 256
    per_b = seq // tm
    y3 = y2.reshape(TOP_K, t, d)
    return pl.pallas_call(
        _combine_kernel,
        grid=(t // tm,),
        in_specs=[pl.BlockSpec((tm, d), lambda i: (i, 0)),
                  pl.BlockSpec((1, tm, d), lambda i: (0, i, 0)),
                  pl.BlockSpec((1, tm, d), lambda i: (1, i, 0)),
                  pl.BlockSpec((1, 1, d), lambda i: (i // per_b, 0, 0))],
        out_specs=pl.BlockSpec((tm, d), lambda i: (i, 0)),
        out_shape=jax.ShapeDtypeStruct((t, d), F32),
        compiler_params=_cparams(("arbitrary",)),
        name="moe_combine",
    )(x2, y3, y3, g_f)


def kernel(x, c, norm_mix, norm_ffn, w_ada, b_ada, w_in, conv_w, a_log, dt_bias, gdn_norm, diff_q_norm, diff_k_norm, lam_q1, lam_k1, lam_q2, lam_k2, diff_subln, dil_q_norm, dil_k_norm, w_out, w_router, router_bias, w_gate, w_up, w_down):
    batch, seq, d = x.shape
    depth = w_ada.shape[0]
    t = batch * seq
    x2 = x.reshape(t, d)
    mod = _ada(c, w_ada, b_ada).reshape(depth, batch, N_MOD, 1, d)
    ba_lo = 3 * GDN_HEADS * LANES + GDN_HEADS * LANES
    ba_hi = ba_lo + 2 * GDN_HEADS
    w_router_t = w_router.T.astype(BF16)
    for l in range(depth):
        sh_a, sc_a, g_a, sh_f, sc_f, g_f = (mod[l, :, k] for k in range(N_MOD))
        w_main = jnp.concatenate([w_in[l, :, :ba_lo], w_in[l, :, ba_hi:]], axis=1).astype(BF16)
        w_ba = jnp.pad(w_in[l, :, ba_lo:ba_hi], ((0, 0), (0, LANES - 2 * GDN_HEADS))).astype(BF16)
        proj, ba = _inproj(x2, norm_mix[l].reshape(1, d), sc_a, sh_a, w_main, w_ba, seq)
        o_a = _gdn(proj, ba, conv_w[l], a_log[l], dt_bias[l], gdn_norm[l], batch, seq)
        lam_init = 0.8 - 0.6 * math.exp(-0.3 * l)
        lam_vecs = jnp.stack([lam_q1[l], lam_k1[l], lam_q2[l], lam_k2[l]]).astype(F32)
        o_b = _diff(proj, diff_q_norm[l], diff_k_norm[l], diff_subln[l], lam_vecs, lam_init, batch, seq)
        o_c = _dil(proj, dil_q_norm[l], dil_k_norm[l], batch, seq)
        x2, hf, route_i, route_w, counts = _outproj(o_a, o_b, o_c, w_out[l].astype(BF16), x2, g_a,
                                                    norm_ffn[l].reshape(1, d), sc_f, sh_f, w_router_t, router_bias, seq)
        block_e, block_cnt, n_valid, row_tok, row_dst, row_w = _block_layout(route_i, route_w, counts)
        y2 = _moe(hf, block_e, block_cnt, n_valid, row_tok, row_dst, row_w,
                  w_gate[l].astype(BF16), w_up[l].astype(BF16), w_down[l].astype(BF16))
        x2 = _combine(x2, y2, g_f, seq)
    return x2.reshape(batch, seq, d)
```

```python
import functools
import math

import jax
import jax.numpy as jnp
from jax import lax
from jax.experimental import pallas as pl
from jax.experimental.pallas import tpu as pltpu

F32 = jnp.float32
BF16 = jnp.bfloat16

LANES = 128
EPS = 1e-6
N_MOD = 6

GDN_HEADS = 6
GDN_CONV = 4
GDN_CHUNK = 256
GDN_LOCKSTEP = 4
DIFF_HEADS = 4
DIFF_QK_DIM = 64
DIL_HEADS = 6
DILATED_PAIRS = ((128, 1), (512, 4), (2048, 16))
DIL_BLOCK = 128

N_EXPERTS = 64
N_GROUPS = 8
EXPERTS_PER_GROUP = N_EXPERTS // N_GROUPS
TOP_K = 2
MOE_ROWS = 256

SLAB_QA, SLAB_KA, SLAB_VA, SLAB_ZA = 0, 6, 12, 18
SLAB_QB, SLAB_KB, SLAB_VB = 24, 28, 32
SLAB_QC, SLAB_KC, SLAB_VC = 36, 42, 48
N_SLABS = 54

VMEM_LIMIT = 56 * 1024 * 1024


def _cparams(sem, vmem=VMEM_LIMIT, **kw):
    return pltpu.CompilerParams(dimension_semantics=sem, vmem_limit_bytes=vmem, **kw)


def _silu(v):
    return v * jax.nn.sigmoid(v)


def _dot(a, b):
    return jnp.dot(a, b, preferred_element_type=F32)


def _dot_nt(a, b):
    return lax.dot_general(a, b, (((1,), (1,)), ((), ())), preferred_element_type=F32)


def _dot_tn(a, b):
    return lax.dot_general(a, b, (((0,), (0,)), ((), ())), preferred_element_type=F32)


def _ada_kernel(c_ref, w_ref, b_ref, o_ref):
    cact = _silu(c_ref[...]).astype(BF16)
    o_ref[0] = _dot(cact, w_ref[0].astype(BF16)) + b_ref[0]


def _ada(c, w_ada, b_ada):
    depth, d, n = w_ada.shape
    b = c.shape[0]
    tn = 1024
    return pl.pallas_call(
        _ada_kernel,
        grid=(depth, n // tn),
        in_specs=[
            pl.BlockSpec((b, d), lambda l, j: (0, 0)),
            pl.BlockSpec((1, d, tn), lambda l, j: (l, 0, j)),
            pl.BlockSpec((1, 1, tn), lambda l, j: (l, 0, j)),
        ],
        out_specs=pl.BlockSpec((1, b, tn), lambda l, j: (l, 0, j)),
        out_shape=jax.ShapeDtypeStruct((depth, b, n), F32),
        compiler_params=_cparams(("arbitrary", "arbitrary")),
        name="ada_mod",
    )(c, w_ada, b_ada.reshape(depth, 1, n))


def _modulated_norm(x, nw, sc, sh):
    ms = jnp.mean(x * x, axis=-1, keepdims=True)
    return (x * lax.rsqrt(ms + EPS) * nw) * (1.0 + sc) + sh


def _inproj_kernel(x_ref, nw_ref, sc_ref, sh_ref, w_ref, wba_ref, o_ref, ba_ref, h_scr, *, n_sub):
    @pl.when(pl.program_id(1) == 0)
    def _():
        h = _modulated_norm(x_ref[...], nw_ref[...], sc_ref[0], sh_ref[0]).astype(BF16)
        h_scr[...] = h
        ba_ref[...] = _dot(h, wba_ref[...])

    acc = _dot(h_scr[...], w_ref[...])
    for k in range(n_sub):
        o_ref[k] = acc[:, k * LANES:(k + 1) * LANES].astype(BF16)


def _inproj(x2, nw, sc, sh, w_main, w_ba, seq):
    t, d = x2.shape
    n = w_main.shape[1]
    tm = min(1024, seq)
    tn = 768
    n_sub = tn // LANES
    per_b = seq // tm
    return pl.pallas_call(
        functools.partial(_inproj_kernel, n_sub=n_sub),
        grid=(t // tm, n // tn),
        in_specs=[
            pl.BlockSpec((tm, d), lambda i, j: (i, 0)),
            pl.BlockSpec((1, d), lambda i, j: (0, 0)),
            pl.BlockSpec((1, 1, d), lambda i, j: (i // per_b, 0, 0)),
            pl.BlockSpec((1, 1, d), lambda i, j: (i // per_b, 0, 0)),
            pl.BlockSpec((d, tn), lambda i, j: (0, j)),
            pl.BlockSpec((d, LANES), lambda i, j: (0, 0)),
        ],
        out_specs=[
            pl.BlockSpec((n_sub, tm, LANES), lambda i, j: (j, i, 0)),
            pl.BlockSpec((tm, LANES), lambda i, j: (i, 0)),
        ],
        out_shape=[
            jax.ShapeDtypeStruct((n // LANES, t, LANES), BF16),
            jax.ShapeDtypeStruct((t, LANES), F32),
        ],
        scratch_shapes=[pltpu.VMEM((tm, d), BF16)],
        compiler_params=_cparams(("arbitrary", "arbitrary")),
        name="inproj",
    )(x2, nw, sc, sh, w_main, w_ba)


def _unit_lower_inverse(lms, xr):
    c = lms[0].shape[0]

    def mm(a, b):
        return [_dot(ai.astype(BF16), bi.astype(BF16)) for ai, bi in zip(a, b)]

    d1 = [jnp.where(xr < 16, lm, 0.0) for lm in lms]
    d2 = mm(d1, d1)
    d4 = mm(d2, d2)
    d8 = mm(d4, d4)
    p = [b - a - t for a, b, t in zip(d1, d2, mm(d1, d2))]
    p = [pi + di + t for pi, di, t in zip(p, d4, mm(p, d4))]
    p = [pi + di + t for pi, di, t in zip(p, d8, mm(p, d8))]
    eye = jnp.where(xr == 0, 1.0, 0.0)
    x = [eye + pi for pi in p]
    blk = 32
    while blk <= c:
        e = [jnp.where((xr < blk) & (xr >= blk // 2), lm, 0.0) for lm in lms]
        x = [xi - t for xi, t in zip(x, mm(x, mm(e, x)))]
        blk *= 2
    return x


def _gdn_kernel(q_ref, k_ref, v_ref, z_ref, ba_ref, cwq_ref, cwk_ref, cwv_ref, al_ref, dtb_ref, nw_ref,
                o_ref, xq, xk, xv, u_s, w_s, a_s, qd_s, kd_s, cd_s, *, seq, chunk):
    c = chunk
    n_chunks = seq // c
    head = pl.program_id(1)
    dk = LANES

    lane1 = lax.broadcasted_iota(jnp.int32, (1, LANES), 1)
    a_exp = jnp.exp(jnp.sum(jnp.where(lane1 == head, al_ref[...], 0.0), axis=-1, keepdims=True))
    dtb = jnp.sum(jnp.where(lane1 == head, dtb_ref[...], 0.0), axis=-1, keepdims=True)

    zeros8 = jnp.zeros((8, LANES), F32)
    for src, dst in ((q_ref, xq), (k_ref, xk), (v_ref, xv)):
        dst[0:8, :] = zeros8

        def stage(i, carry, src=src, dst=dst):
            r0 = pl.multiple_of(i * c, c)
            dst[pl.ds(r0 + 8, c), :] = src[0, pl.ds(r0, c), :].astype(F32)
            return carry

        lax.fori_loop(0, n_chunks, stage, 0)

    row = lax.broadcasted_iota(jnp.int32, (c, c), 0)
    col = lax.broadcasted_iota(jnp.int32, (c, c), 1)
    causal = row >= col
    strict = row > col
    xr = row ^ col
    tril = jnp.where(causal, 1.0, 0.0).astype(BF16)
    lane_c = lax.broadcasted_iota(jnp.int32, (c, LANES), 1)

    def conv(xs, cw_ref, r0):
        acc = cw_ref[0, GDN_CONV - 1:GDN_CONV, :] * xs[pl.ds(r0 + 8, c), :]
        for back in range(1, GDN_CONV):
            tap = GDN_CONV - 1 - back
            acc = acc + cw_ref[0, tap:tap + 1, :] * xs[pl.ds(r0 + 8 - back, c), :]
        return _silu(acc)

    def l2n(t):
        return t * lax.rsqrt(jnp.sum(t * t, axis=-1, keepdims=True) + EPS)

    def log_decay(r0):
        bat = ba_ref[pl.ds(r0, c), :]
        bcol = jnp.sum(jnp.where(lane_c == head, bat, 0.0), axis=-1, keepdims=True)
        acol = jnp.sum(jnp.where(lane_c == head + GDN_HEADS, bat, 0.0), axis=-1, keepdims=True)
        xsp = acol + dtb
        softplus = jnp.maximum(xsp, 0.0) + jnp.log1p(jnp.exp(-jnp.abs(xsp)))
        return jax.nn.sigmoid(bcol), jnp.broadcast_to(-a_exp * softplus, (c, LANES))

    def intra(step, carry):
        par = range(GDN_LOCKSTEP)
        ns = [step * GDN_LOCKSTEP + j for j in par]
        r0s = [pl.multiple_of(n * c, c) for n in ns]
        qn = [l2n(conv(xq, cwq_ref, r0)) * (dk ** -0.5) for r0 in r0s]
        kn = [l2n(conv(xk, cwk_ref, r0)) for r0 in r0s]
        vc = [conv(xv, cwv_ref, r0) for r0 in r0s]
        bg = [log_decay(r0) for r0 in r0s]
        beta = [t[0] for t in bg]
        g_hi = [t[1].astype(BF16) for t in bg]
        g_lo = [(t[1] - h.astype(F32)).astype(BF16) for t, h in zip(bg, g_hi)]
        gc = [_dot(tril, h) + _dot(tril, lo_) for h, lo_ in zip(g_hi, g_lo)]
        decay = []
        for gci in gc:
            gc_row = jnp.transpose(gci)[0:1, :]
            diff = jnp.concatenate([gci] * (c // LANES), axis=1) - gc_row
            decay.append(jnp.where(causal, jnp.exp(jnp.where(causal, diff, 0.0)), 0.0))
        kb = [k * b for k, b in zip(kn, beta)]
        knb = [k.astype(BF16) for k in kn]
        lm = [jnp.where(strict, _dot_nt(kbi.astype(BF16), ki) * dc, 0.0) for kbi, ki, dc in zip(kb, knb, decay)]
        attn = [jnp.where(causal, _dot_nt(q.astype(BF16), ki) * dc, 0.0) for q, ki, dc in zip(qn, knb, decay)]
        tinv = _unit_lower_inverse(lm, xr)
        egc = [jnp.exp(g) for g in gc]
        sol = [_dot(ti.astype(BF16), jnp.concatenate([v * b, kbi * e], axis=1).astype(BF16))
               for ti, v, b, kbi, e in zip(tinv, vc, beta, kb, egc)]
        for j in par:
            rows = pl.ds(r0s[j], c)
            u_s[rows, :] = sol[j][:, :LANES]
            w_s[rows, :] = sol[j][:, LANES:].astype(BF16)
            a_s[rows, :] = attn[j].astype(BF16)
            qd_s[rows, :] = (qn[j] * egc[j]).astype(BF16)
            g_last = gc[j][c - 1:c, :]
            kd_s[rows, :] = (kn[j] * jnp.exp(g_last - gc[j])).astype(BF16)
            cd_s[pl.ds(pl.multiple_of(ns[j] * 8, 8), 8), :] = jnp.broadcast_to(jnp.exp(g_last), (8, LANES))
        return carry

    lax.fori_loop(0, n_chunks // GDN_LOCKSTEP, intra, 0)

    def inter(n, state):
        r0 = pl.multiple_of(n * c, c)
        rows = pl.ds(r0, c)
        sb = state.astype(BF16)
        v_new = u_s[rows, :] - _dot(w_s[rows, :], sb)
        vb = v_new.astype(BF16)
        o = _dot(qd_s[rows, :], sb) + _dot(a_s[rows, :], vb)
        cd = cd_s[pl.ds(pl.multiple_of(n * 8, 8), 1), :]
        new_state = state * cd + _dot_tn(kd_s[rows, :], vb)
        on = o * lax.rsqrt(jnp.mean(o * o, axis=-1, keepdims=True) + EPS) * nw_ref[...]
        z = z_ref[0, rows, :].astype(F32)
        o_ref[0, rows, :] = (on * _silu(z)).astype(BF16)
        return new_state

    lax.fori_loop(0, n_chunks, inter, jnp.zeros((dk, LANES), F32))


def _gdn(proj, ba, conv_w, a_log, dt_bias, gdn_norm, batch, seq):
    c = GDN_CHUNK
    per = seq
    cw = conv_w.reshape(GDN_CONV, 3 * GDN_HEADS, LANES).transpose(1, 0, 2)
    pad = lambda v: jnp.pad(v, (0, LANES - v.shape[0])).reshape(1, LANES)

    def slab(base):
        return pl.BlockSpec((1, per, LANES), lambda b, h, base=base: (base + h, b, 0))

    def cwspec(base):
        return pl.BlockSpec((1, GDN_CONV, LANES), lambda b, h, base=base: (base + h, 0, 0))

    vec = pl.BlockSpec((1, LANES), lambda b, h: (0, 0))
    return pl.pallas_call(
        functools.partial(_gdn_kernel, seq=seq, chunk=c),
        grid=(batch, GDN_HEADS),
        in_specs=[slab(SLAB_QA), slab(SLAB_KA), slab(SLAB_VA), slab(SLAB_ZA),
                  pl.BlockSpec((per, LANES), lambda b, h: (b, 0)),
                  cwspec(0), cwspec(GDN_HEADS), cwspec(2 * GDN_HEADS), vec, vec, vec],
        out_specs=pl.BlockSpec((1, per, LANES), lambda b, h: (h, b, 0)),
        out_shape=jax.ShapeDtypeStruct((GDN_HEADS, batch * seq, LANES), BF16),
        scratch_shapes=[
            pltpu.VMEM((seq + 8, LANES), F32), pltpu.VMEM((seq + 8, LANES), F32),
            pltpu.VMEM((seq + 8, LANES), F32),
            pltpu.VMEM((seq, LANES), F32),
            pltpu.VMEM((seq, LANES), BF16),
            pltpu.VMEM((seq, c), BF16),
            pltpu.VMEM((seq, LANES), BF16),
            pltpu.VMEM((seq, LANES), BF16),
            pltpu.VMEM((seq // c * 8, LANES), F32),
        ],
        compiler_params=_cparams(("arbitrary", "arbitrary")),
        name="gdn_mixer",
    )(proj, proj, proj, proj, ba, cw, cw, cw, pad(a_log), pad(dt_bias), gdn_norm.reshape(1, LANES))


def _diff_kernel(q_ref, k_ref, v_ref, qw_ref, kw_ref, sw_ref, lam_ref, o_ref,
                 q1_s, q2_s, kn_s, ve_s, sc1, sc2, acc1, acc2, m1, m2, *, seq, tq, tk, lam_init):
    dh = DIFF_QK_DIM
    n_blk = seq // tq
    per_q = tq // tk
    lane = lax.broadcasted_iota(jnp.int32, (1, LANES), 1)
    lo = lane < dh

    lam = (jnp.exp(jnp.sum(lam_ref[0:1, :] * lam_ref[1:2, :], axis=-1, keepdims=True))
           - jnp.exp(jnp.sum(lam_ref[2:3, :] * lam_ref[3:4, :], axis=-1, keepdims=True)) + lam_init)

    def halfnorm(t, w):
        sq = t * t
        s_lo = jnp.sum(jnp.where(lo, sq, 0.0), axis=-1, keepdims=True)
        s_hi = jnp.sum(jnp.where(lo, 0.0, sq), axis=-1, keepdims=True)
        r = jnp.where(lo, lax.rsqrt(s_lo / dh + EPS), lax.rsqrt(s_hi / dh + EPS))
        return t * r * w

    ones_blk = jnp.ones((tk, LANES), BF16)

    def prep(i, carry):
        rows = pl.ds(pl.multiple_of(i * tk, tk), tk)
        kn_s[rows, :] = halfnorm(k_ref[0, rows, :].astype(F32), kw_ref[...]).astype(BF16)
        qn = halfnorm(q_ref[0, rows, :].astype(F32), qw_ref[...]) * (dh ** -0.5)
        q1_s[rows, :] = jnp.where(lo, qn, 0.0).astype(BF16)
        q2_s[rows, :] = jnp.where(lo, 0.0, qn).astype(BF16)
        ve_s[rows, 0:LANES] = v_ref[0, rows, :]
        ve_s[rows, LANES:2 * LANES] = ones_blk
        return carry

    lax.fori_loop(0, seq // tk, prep, 0)

    row = lax.broadcasted_iota(jnp.int32, (tq, tk), 0)
    col = lax.broadcasted_iota(jnp.int32, (tq, tk), 1)

    n_parts = tk // LANES

    def q_block(qi, carry):
        qrows = pl.ds(pl.multiple_of(qi * tq, tq), tq)
        maps = ((q1_s[qrows, :], sc1, m1, acc1), (q2_s[qrows, :], sc2, m2, acc2))
        for _, _, m_r, a_r in maps:
            m_r[...] = jnp.full((tq, LANES), -jnp.inf, F32)
            a_r[...] = jnp.zeros((tq, 2 * LANES), F32)

        def score_step(kj, masked):
            kb = kn_s[pl.ds(pl.multiple_of(kj * tk, tk), tk), :]
            for qq, s_c, m_r, _ in maps:
                s = _dot_nt(qq, kb)
                if masked:
                    s = jnp.where(row + qi * tq >= col + kj * tk, s, -jnp.inf)
                s_c[kj] = s
                smax = m_r[...]
                for part in range(n_parts):
                    smax = jnp.maximum(smax, s[:, part * LANES:(part + 1) * LANES])
                m_r[...] = smax

        def off_diag(kj, c2):
            score_step(kj, False)
            return c2

        lax.fori_loop(0, qi * per_q, off_diag, 0)
        for dblk in range(per_q):
            score_step(qi * per_q + dblk, True)
        for _, _, m_r, _ in maps:
            m_r[...] = jnp.broadcast_to(jnp.max(m_r[...], axis=-1, keepdims=True), (tq, LANES))

        def value_step(kj, c2):
            vb = ve_s[pl.ds(pl.multiple_of(kj * tk, tk), tk), :]
            for _, s_c, m_r, a_r in maps:
                mrep = m_r[...]
                s = s_c[kj]
                p = jnp.concatenate(
                    [jnp.exp(s[:, part * LANES:(part + 1) * LANES] - mrep) for part in range(n_parts)], axis=1)
                a_r[...] += _dot(p.astype(BF16), vb)
            return c2

        lax.fori_loop(0, (qi + 1) * per_q, value_step, 0)
        o = (acc1[:, 0:LANES] / acc1[:, LANES:2 * LANES]
             - lam * (acc2[:, 0:LANES] / acc2[:, LANES:2 * LANES]))
        o = o * lax.rsqrt(jnp.mean(o * o, axis=-1, keepdims=True) + EPS) * sw_ref[...] * (1.0 - lam_init)
        o_ref[0, qrows, :] = o.astype(BF16)
        return carry

    lax.fori_loop(0, n_blk, q_block, 0)


def _diff(proj, q_norm, k_norm, subln, lam_vecs, lam_init, batch, seq):
    tq = 512
    tk = 256
    per = seq

    def slab(base):
        return pl.BlockSpec((1, per, LANES), lambda b, h, base=base: (base + h, b, 0))

    vec = pl.BlockSpec((1, LANES), lambda b, h: (0, 0))
    tile2 = lambda w: jnp.concatenate([w, w]).reshape(1, LANES)
    return pl.pallas_call(
        functools.partial(_diff_kernel, seq=seq, tq=tq, tk=tk, lam_init=lam_init),
        grid=(batch, DIFF_HEADS),
        in_specs=[slab(SLAB_QB), slab(SLAB_KB), slab(SLAB_VB), vec, vec, vec,
                  pl.BlockSpec((4, DIFF_QK_DIM), lambda b, h: (0, 0))],
        out_specs=pl.BlockSpec((1, per, LANES), lambda b, h: (h, b, 0)),
        out_shape=jax.ShapeDtypeStruct((DIFF_HEADS, batch * seq, LANES), BF16),
        scratch_shapes=[
            pltpu.VMEM((seq, LANES), BF16), pltpu.VMEM((seq, LANES), BF16), pltpu.VMEM((seq, LANES), BF16),
            pltpu.VMEM((seq, 2 * LANES), BF16),
            pltpu.VMEM((seq // tk, tq, tk), F32), pltpu.VMEM((seq // tk, tq, tk), F32),
            pltpu.VMEM((tq, 2 * LANES), F32), pltpu.VMEM((tq, 2 * LANES), F32),
            pltpu.VMEM((tq, LANES), F32), pltpu.VMEM((tq, LANES), F32),
        ],
        compiler_params=_cparams(("arbitrary", "arbitrary")),
        name="diff_mixer",
    )(proj, proj, proj, tile2(q_norm), tile2(k_norm), subln.reshape(1, LANES), lam_vecs)


def _dil_kernel(q_ref, k_ref, v_ref, qw_ref, kw_ref, o_ref, qn_s, kn_s, vf_s, sc_s, mb_s, acc_s, l_s, m_s, *, seq):
    blk = DIL_BLOCK
    dh = LANES
    pc = 256
    n_pc = seq // pc
    unroll = 4

    def rms(t, w):
        return t * lax.rsqrt(jnp.mean(t * t, axis=-1, keepdims=True) + EPS) * w

    ones_kv = jnp.ones((2 * blk, LANES), BF16)

    def prep(i, carry):
        rows = pl.ds(pl.multiple_of(i * pc, pc), pc)
        qn_s[rows, :] = rms(q_ref[0, rows, :].astype(F32), qw_ref[...]) * (dh ** -0.5)
        kn_s[rows, :] = rms(k_ref[0, rows, :].astype(F32), kw_ref[...])
        vf_s[rows, :] = v_ref[0, rows, :].astype(F32)
        return carry

    lax.fori_loop(0, n_pc, prep, 0)

    qi = lax.broadcasted_iota(jnp.int32, (blk, 2 * blk), 0)
    kj = lax.broadcasted_iota(jnp.int32, (blk, 2 * blk), 1)

    for set_idx, (window, dil) in enumerate(DILATED_PAIRS):
        steps = window // dil
        n_sub = seq // dil
        nb = n_sub // blk
        assert steps == blk and nb >= 2 and (dil * nb) % unroll == 0

        def slices(idx, dil=dil, nb=nb):
            r = idx // nb
            n = idx - r * nb
            kbase = jnp.maximum(n - 1, 0) * blk
            if dil > 1:
                return n, kbase, pl.ds(r + dil * blk * n, blk, stride=dil), pl.ds(r + dil * kbase, 2 * blk, stride=dil)
            return n, kbase, pl.ds(pl.multiple_of(blk * n, blk), blk), pl.ds(pl.multiple_of(kbase, blk), 2 * blk)

        def score_block(idx, carry):
            n, kbase, qsl, ksl = slices(idx)
            s = _dot_nt(qn_s[qsl, :].astype(BF16), kn_s[ksl, :].astype(BF16))
            dist = (n * blk + qi) - (kbase + kj)
            s = jnp.where((dist >= 0) & (dist <= steps), s, -jnp.inf)
            sc_s[idx] = s
            m = jnp.max(jnp.maximum(s[:, 0:blk], s[:, blk:2 * blk]), axis=-1, keepdims=True)
            mb_s[idx] = jnp.broadcast_to(m, (blk, LANES))
            return carry

        lax.fori_loop(0, dil * nb, score_block, 0, unroll=unroll)

        def value_block(idx, carry, first=(set_idx == 0)):
            _, _, qsl, ksl = slices(idx)
            s = sc_s[idx]
            m = mb_s[idx]
            p = jnp.concatenate([jnp.exp(s[:, 0:blk] - m), jnp.exp(s[:, blk:2 * blk] - m)], axis=1)
            ve = jnp.concatenate([vf_s[ksl, :].astype(BF16), ones_kv], axis=1)
            ol = _dot(p.astype(BF16), ve)
            o = ol[:, 0:LANES]
            l = ol[:, LANES:2 * LANES]
            if first:
                acc_s[qsl, :] = o
                l_s[qsl, :] = l
                m_s[qsl, :] = m
            else:
                m_old = m_s[qsl, :]
                m_new = jnp.maximum(m_old, m)
                a_old = jnp.exp(m_old - m_new)
                a_cur = jnp.exp(m - m_new)
                acc_s[qsl, :] = a_old * acc_s[qsl, :] + a_cur * o
                l_s[qsl, :] = a_old * l_s[qsl, :] + a_cur * l
                m_s[qsl, :] = m_new
            return carry

        lax.fori_loop(0, dil * nb, value_block, 0, unroll=unroll)

    def fin(i, carry):
        rows = pl.ds(pl.multiple_of(i * pc, pc), pc)
        o_ref[0, rows, :] = (acc_s[rows, :] / l_s[rows, :]).astype(BF16)
        return carry

    lax.fori_loop(0, n_pc, fin, 0)


def _dil(proj, q_norm, k_norm, batch, seq):
    per = seq

    def slab(base):
        return pl.BlockSpec((1, per, LANES), lambda b, h, base=base: (base + h, b, 0))

    vec = pl.BlockSpec((1, LANES), lambda b, h: (0, 0))
    return pl.pallas_call(
        functools.partial(_dil_kernel, seq=seq),
        grid=(batch, DIL_HEADS),
        in_specs=[slab(SLAB_QC), slab(SLAB_KC), slab(SLAB_VC), vec, vec],
        out_specs=pl.BlockSpec((1, per, LANES), lambda b, h: (h, b, 0)),
        out_shape=jax.ShapeDtypeStruct((DIL_HEADS, batch * seq, LANES), BF16),
        scratch_shapes=[
            pltpu.VMEM((seq, LANES), F32), pltpu.VMEM((seq, LANES), F32),
            pltpu.VMEM((seq, LANES), F32),
            pltpu.VMEM((seq // DIL_BLOCK, DIL_BLOCK, 2 * DIL_BLOCK), F32),
            pltpu.VMEM((seq // DIL_BLOCK, DIL_BLOCK, LANES), F32),
            pltpu.VMEM((seq, LANES), F32), pltpu.VMEM((seq, LANES), F32),
            pltpu.VMEM((seq, LANES), F32),
        ],
        compiler_params=_cparams(("arbitrary", "arbitrary")),
        name="dil_mixer",
    )(proj, proj, proj, q_norm.reshape(1, LANES), k_norm.reshape(1, LANES))


def _route_tile(sc, bias, run):
    ne, tm = sc.shape
    epg = EXPERTS_PER_GROUP
    sel = sc + bias
    jrow = lax.broadcasted_iota(jnp.int32, (epg, tm), 0)
    gs, i1s, i2s = [], [], []
    for g in range(N_GROUPS):
        sg = sel[g * epg:(g + 1) * epg, :]
        m1 = jnp.max(sg, axis=0, keepdims=True)
        i1 = jnp.min(jnp.where(sg == m1, jrow, epg), axis=0, keepdims=True)
        rest = jnp.where(jrow == i1, -jnp.inf, sg)
        m2 = jnp.max(rest, axis=0, keepdims=True)
        i2 = jnp.min(jnp.where(rest == m2, jrow, epg), axis=0, keepdims=True)
        gs.append(m1 + m2)
        i1s.append(i1)
        i2s.append(i2)
    gmax = functools.reduce(jnp.maximum, gs)
    gidx = jnp.full((1, tm), N_GROUPS - 1, jnp.int32)
    for g in reversed(range(N_GROUPS - 1)):
        gidx = jnp.where(gs[g] == gmax, g, gidx)
    loc1 = jnp.zeros((1, tm), jnp.int32)
    loc2 = jnp.zeros((1, tm), jnp.int32)
    for g in range(N_GROUPS):
        loc1 = jnp.where(gidx == g, i1s[g], loc1)
        loc2 = jnp.where(gidx == g, i2s[g], loc2)
    e1 = gidx * epg + loc1
    e2 = gidx * epg + loc2
    erow = lax.broadcasted_iota(jnp.int32, (ne, tm), 0)
    oh1 = erow == e1
    oh2 = erow == e2
    s1 = jnp.sum(jnp.where(oh1, sc, 0.0), axis=0, keepdims=True)
    s2 = jnp.sum(jnp.where(oh2, sc, 0.0), axis=0, keepdims=True)
    den = s1 + s2
    tr = lax.broadcasted_iota(jnp.int32, (tm, tm), 0)
    tc = lax.broadcasted_iota(jnp.int32, (tm, tm), 1)
    before = jnp.where(tr < tc, 1.0, 0.0).astype(BF16)
    f1 = jnp.where(oh1, 1.0, 0.0)
    f2 = jnp.where(oh2, 1.0, 0.0)
    pre1 = _dot(f1.astype(BF16), before)
    pre2 = _dot(f2.astype(BF16), before)
    tot1 = jnp.sum(f1, axis=1, keepdims=True)
    tot2 = jnp.sum(f2, axis=1, keepdims=True)
    r1 = jnp.sum(jnp.where(oh1, pre1 + run, 0.0), axis=0, keepdims=True)
    r2 = jnp.sum(jnp.where(oh2, pre2 + (run + tot1), 0.0), axis=0, keepdims=True)
    return e1, e2, s1 / den, s2 / den, r1, r2, run + tot1 + tot2


def _outproj_kernel(oa_ref, ob_ref, oc_ref, w_ref, x_ref, g_ref, nw_ref, sc_ref, sh_ref, wrt_ref, rb_ref,
                    xo_ref, hf_ref, ri_ref, rw_ref, cnt_ref, mix_s, run_s):
    @pl.when(pl.program_id(0) == 0)
    def _():
        run_s[...] = jnp.zeros(run_s.shape, F32)

    off = 0
    for ref, heads in ((oa_ref, GDN_HEADS), (ob_ref, DIFF_HEADS), (oc_ref, DIL_HEADS)):
        for k in range(heads):
            mix_s[:, off:off + LANES] = ref[k]
            off += LANES
    xn = x_ref[...] + g_ref[0] * _dot(mix_s[...], w_ref[...])
    xo_ref[...] = xn
    h = _modulated_norm(xn, nw_ref[...], sc_ref[0], sh_ref[0])
    hf_ref[...] = h
    scores_t = jax.nn.sigmoid(_dot_nt(wrt_ref[...], h.astype(BF16)))
    e1, e2, w1, w2, r1, r2, run = _route_tile(scores_t, rb_ref[...], run_s[...])
    run_s[...] = run
    cnt_ref[...] = jnp.broadcast_to(run, cnt_ref.shape)
    tm = e1.shape[1]
    ri_ref[...] = jnp.concatenate([e1, e2, r1.astype(jnp.int32), r2.astype(jnp.int32),
                                   jnp.zeros((4, tm), jnp.int32)], axis=0)
    rw_ref[...] = jnp.transpose(jnp.concatenate([w1, w2, jnp.zeros((6, tm), F32)], axis=0))


def _outproj(o_a, o_b, o_c, w_out, x2, g_a, nw, sc, sh, w_router_t, router_bias, seq):
    t, d = x2.shape
    tm = 256
    per_b = seq // tm
    ne = w_router_t.shape[0]
    row = pl.BlockSpec((tm, d), lambda i: (i, 0))
    bvec = pl.BlockSpec((1, 1, d), lambda i: (i // per_b, 0, 0))
    return pl.pallas_call(
        _outproj_kernel,
        grid=(t // tm,),
        in_specs=[
            pl.BlockSpec((GDN_HEADS, tm, LANES), lambda i: (0, i, 0)),
            pl.BlockSpec((DIFF_HEADS, tm, LANES), lambda i: (0, i, 0)),
            pl.BlockSpec((DIL_HEADS, tm, LANES), lambda i: (0, i, 0)),
            pl.BlockSpec((d, d), lambda i: (0, 0)),
            row, bvec,
            pl.BlockSpec((1, d), lambda i: (0, 0)),
            bvec, bvec,
            pl.BlockSpec((ne, d), lambda i: (0, 0)),
            pl.BlockSpec((ne, 1), lambda i: (0, 0)),
        ],
        out_specs=[row, row,
                   pl.BlockSpec((8, tm), lambda i: (0, i)),
                   pl.BlockSpec((tm, 8), lambda i: (i, 0)),
                   pl.BlockSpec((ne, LANES), lambda i: (0, 0))],
        out_shape=[jax.ShapeDtypeStruct((t, d), F32), jax.ShapeDtypeStruct((t, d), F32),
                   jax.ShapeDtypeStruct((8, t), jnp.int32), jax.ShapeDtypeStruct((t, 8), F32),
                   jax.ShapeDtypeStruct((ne, LANES), F32)],
        scratch_shapes=[pltpu.VMEM((tm, d), BF16), pltpu.VMEM((ne, 1), F32)],
        compiler_params=_cparams(("arbitrary",)),
        name="outproj",
    )(o_a, o_b, o_c, w_out, x2, g_a, nw, sc, sh, w_router_t, router_bias.reshape(ne, 1).astype(F32))


def _moe_kernel(be_ref, nv_ref, tok_ref, tok_next_ref, dst_prev_ref, dst_ref, hf_hbm, wg_ref, wu_ref, wd_ref,
                y_hbm, xbuf, ybuf, wg_s, wu_s, wd_s, gsem, ssem):
    i = pl.program_id(0)
    nv = nv_ref[0]
    slot = i % 2
    rows = MOE_ROWS

    def gather_copy(tok_r, r, s):
        return pltpu.make_async_copy(hf_hbm.at[pl.ds(tok_r[0, 0, r], 1)], xbuf.at[s, pl.ds(r, 1)], gsem.at[s])

    def scatter_copy(dst_r, r, s):
        return pltpu.make_async_copy(ybuf.at[s, pl.ds(r, 1)], y_hbm.at[pl.ds(dst_r[0, 0, r], 1)], ssem.at[s])

    @pl.when(i == 0)
    def _():
        ybuf[...] = jnp.zeros(ybuf.shape, F32)
        for r in range(rows):
            gather_copy(tok_ref, r, 0).start()

    @pl.when((i == 0) | (be_ref[i] != be_ref[jnp.maximum(i - 1, 0)]))
    def _():
        wg_s[...] = wg_ref[0].astype(BF16)
        wu_s[...] = wu_ref[0].astype(BF16)
        wd_s[...] = wd_ref[0].astype(BF16)

    @pl.when((i >= 1) & (i < nv))
    def _():
        for r in range(rows):
            scatter_copy(dst_ref, r, slot).wait()

    @pl.when(i < nv)
    def _():
        for r in range(rows):
            gather_copy(tok_ref, r, slot).wait()
        x = xbuf[slot].astype(BF16)
        for r in range(rows):
            gather_copy(tok_next_ref, r, 1 - slot).start()
            scatter_copy(dst_prev_ref, r, 1 - slot).start()
        a = _dot(x, wg_s[...])
        u = _dot(x, wu_s[...])
        ybuf[slot] = _dot((_silu(a) * u).astype(BF16), wd_s[...])

    @pl.when(i == nv - 1)
    def _():
        for r in range(rows):
            gather_copy(tok_next_ref, r, 1 - slot).wait()
            scatter_copy(dst_prev_ref, r, 1 - slot).wait()
        for r in range(rows):
            scatter_copy(dst_ref, r, slot).start()
        for r in range(rows):
            scatter_copy(dst_ref, r, slot).wait()


def _moe(hf, block_e, n_valid, row_tok, row_dst, w_gate, w_up, w_down, layer):
    t, d = hf.shape
    f = w_gate.shape[3]
    rows = MOE_ROWS
    nb = block_e.shape[0]
    tok3 = row_tok.reshape(nb, 1, rows)
    spare = (TOP_K * t + jnp.arange(rows, dtype=jnp.int32)).reshape(1, 1, rows)
    dst3 = jnp.concatenate([spare, row_dst.reshape(nb, 1, rows)], axis=0)
    smem_blk = lambda fn: pl.BlockSpec((1, 1, rows), fn, memory_space=pltpu.SMEM)
    grid_spec = pltpu.PrefetchScalarGridSpec(
        num_scalar_prefetch=2,
        grid=(nb,),
        in_specs=[
            smem_blk(lambda i, be, nv: (i, 0, 0)),
            smem_blk(lambda i, be, nv: (jnp.minimum(i + 1, nb - 1), 0, 0)),
            smem_blk(lambda i, be, nv: (i, 0, 0)),
            smem_blk(lambda i, be, nv: (i + 1, 0, 0)),
            pl.BlockSpec(memory_space=pl.ANY),
            pl.BlockSpec((None, 1, d, f), lambda i, be, nv: (layer, be[i], 0, 0)),
            pl.BlockSpec((None, 1, d, f), lambda i, be, nv: (layer, be[i], 0, 0)),
            pl.BlockSpec((None, 1, f, d), lambda i, be, nv: (layer, be[i], 0, 0)),
        ],
        out_specs=pl.BlockSpec(memory_space=pl.ANY),
        scratch_shapes=[
            pltpu.VMEM((2, rows, d), F32),
            pltpu.VMEM((2, rows, d), F32),
            pltpu.VMEM((d, f), BF16), pltpu.VMEM((d, f), BF16), pltpu.VMEM((f, d), BF16),
            pltpu.SemaphoreType.DMA((2,)),
            pltpu.SemaphoreType.DMA((2,)),
        ],
    )
    return pl.pallas_call(
        _moe_kernel,
        grid_spec=grid_spec,
        out_shape=jax.ShapeDtypeStruct((TOP_K * t + rows, d), F32),
        compiler_params=_cparams(("arbitrary",), has_side_effects=True, disable_bounds_checks=True),
        name="moe_experts",
    )(block_e, n_valid, tok3, tok3, dst3, dst3, hf, w_gate, w_up, w_down)


def _block_layout(route_i, counts_f):
    t = route_i.shape[1]
    rows = MOE_ROWS
    n_assign = TOP_K * t
    counts = counts_f[:, 0].astype(jnp.int32)
    blocks_per_e = (counts + rows - 1) // rows
    blk_end = jnp.cumsum(blocks_per_e)
    blk_start = blk_end - blocks_per_e
    flat_e = route_i[0:TOP_K].reshape(n_assign)
    dest = blk_start[flat_e] * rows + route_i[TOP_K:2 * TOP_K].reshape(n_assign)
    nb = n_assign // rows + N_EXPERTS
    n_rows = nb * rows
    assign = jnp.arange(n_assign, dtype=jnp.int32)
    spare = n_assign + jnp.arange(n_rows, dtype=jnp.int32) % rows
    row_dst = spare.at[dest].set(assign)
    row_tok = jnp.where(row_dst < n_assign, row_dst % t, 0)
    n_valid = blk_end[-1].astype(jnp.int32)
    bidx = jnp.arange(nb, dtype=jnp.int32)
    block_e = jnp.sum((blk_end[None, :] <= bidx[:, None]).astype(jnp.int32), axis=1)
    block_e = jnp.minimum(block_e, N_EXPERTS - 1)
    last_e = block_e[jnp.maximum(n_valid - 1, 0)]
    block_e = jnp.where(bidx < n_valid, block_e, last_e)
    return block_e, n_valid.reshape(1), row_tok, row_dst


def _combine_kernel(x_ref, y0_ref, y1_ref, rw_ref, g_ref, o_ref):
    w = rw_ref[...]
    o_ref[...] = x_ref[...] + g_ref[0] * (y0_ref[...] * w[:, 0:1] + y1_ref[...] * w[:, 1:2])


def _combine(x2, y2, route_w, g_f, seq):
    t, d = x2.shape
    tm = 256
    per_b = seq // tm
    n_blk = t // tm
    return pl.pallas_call(
        _combine_kernel,
        grid=(n_blk,),
        in_specs=[pl.BlockSpec((tm, d), lambda i: (i, 0)),
                  pl.BlockSpec((tm, d), lambda i: (i, 0)),
                  pl.BlockSpec((tm, d), lambda i: (n_blk + i, 0)),
                  pl.BlockSpec((tm, 8), lambda i: (i, 0)),
                  pl.BlockSpec((1, 1, d), lambda i: (i // per_b, 0, 0))],
        out_specs=pl.BlockSpec((tm, d), lambda i: (i, 0)),
        out_shape=jax.ShapeDtypeStruct((t, d), F32),
        compiler_params=_cparams(("arbitrary",)),
        name="moe_combine",
    )(x2, y2, y2, route_w, g_f)


def kernel(x, c, norm_mix, norm_ffn, w_ada, b_ada, w_in, conv_w, a_log, dt_bias, gdn_norm, diff_q_norm, diff_k_norm, lam_q1, lam_k1, lam_q2, lam_k2, diff_subln, dil_q_norm, dil_k_norm, w_out, w_router, router_bias, w_gate, w_up, w_down):
    batch, seq, d = x.shape
    depth = w_ada.shape[0]
    t = batch * seq
    x2 = x.reshape(t, d)
    mod = _ada(c, w_ada, b_ada).reshape(depth, batch, N_MOD, 1, d)
    ba_lo = 3 * GDN_HEADS * LANES + GDN_HEADS * LANES
    ba_hi = ba_lo + 2 * GDN_HEADS
    w_router_t = w_router.T.astype(BF16)
    for l in range(depth):
        sh_a, sc_a, g_a, sh_f, sc_f, g_f = (mod[l, :, k] for k in range(N_MOD))
        w_main = jnp.concatenate([w_in[l, :, :ba_lo], w_in[l, :, ba_hi:]], axis=1).astype(BF16)
        w_ba = jnp.pad(w_in[l, :, ba_lo:ba_hi], ((0, 0), (0, LANES - 2 * GDN_HEADS))).astype(BF16)
        proj, ba = _inproj(x2, norm_mix[l].reshape(1, d), sc_a, sh_a, w_main, w_ba, seq)
        o_a = _gdn(proj, ba, conv_w[l], a_log[l], dt_bias[l], gdn_norm[l], batch, seq)
        lam_init = 0.8 - 0.6 * math.exp(-0.3 * l)
        lam_vecs = jnp.stack([lam_q1[l], lam_k1[l], lam_q2[l], lam_k2[l]]).astype(F32)
        o_b = _diff(proj, diff_q_norm[l], diff_k_norm[l], diff_subln[l], lam_vecs, lam_init, batch, seq)
        o_c = _dil(proj, dil_q_norm[l], dil_k_norm[l], batch, seq)
        x2, hf, route_i, route_w, counts = _outproj(o_a, o_b, o_c, w_out[l].astype(BF16), x2, g_a,
                                                    norm_ffn[l].reshape(1, d), sc_f, sh_f, w_router_t, router_bias, seq)
        block_e, n_valid, row_tok, row_dst = _block_layout(route_i, counts)
        y2 = _moe(hf, block_e, n_valid, row_tok, row_dst, w_gate, w_up, w_down, l)
        x2 = _combine(x2, y2, route_w, g_f, seq)
    return x2.reshape(batch, seq, d)
```

```python
import functools
import math

import jax
import jax.numpy as jnp
from jax import lax
from jax.experimental import pallas as pl
from jax.experimental.pallas import tpu as pltpu

F32 = jnp.float32
BF16 = jnp.bfloat16

LANES = 128
EPS = 1e-6
N_MOD = 6

GDN_HEADS = 6
GDN_CONV = 4
GDN_CHUNK = 256
GDN_LOCKSTEP = 4
DIFF_HEADS = 4
DIFF_QK_DIM = 64
DIL_HEADS = 6
DILATED_PAIRS = ((128, 1), (512, 4), (2048, 16))
DIL_BLOCK = 128

N_EXPERTS = 64
N_GROUPS = 8
EXPERTS_PER_GROUP = N_EXPERTS // N_GROUPS
TOP_K = 2
MOE_ROWS = 256

SLAB_QA, SLAB_KA, SLAB_VA, SLAB_ZA = 0, 6, 12, 18
SLAB_QB, SLAB_KB, SLAB_VB = 24, 28, 32
SLAB_QC, SLAB_KC, SLAB_VC = 36, 42, 48
N_SLABS = 54

VMEM_LIMIT = 56 * 1024 * 1024


def _cparams(sem, vmem=VMEM_LIMIT, **kw):
    return pltpu.CompilerParams(dimension_semantics=sem, vmem_limit_bytes=vmem, **kw)


def _silu(v):
    return v * jax.nn.sigmoid(v)


def _dot(a, b):
    return jnp.dot(a, b, preferred_element_type=F32)


def _dot_nt(a, b):
    return lax.dot_general(a, b, (((1,), (1,)), ((), ())), preferred_element_type=F32)


def _dot_tn(a, b):
    return lax.dot_general(a, b, (((0,), (0,)), ((), ())), preferred_element_type=F32)


def _pack_bf16_pairs(v):
    half = v.shape[1] // 2
    lo = lax.bitcast_convert_type(v[:, :half].astype(F32), jnp.int32)
    hi = lax.bitcast_convert_type(v[:, half:].astype(F32), jnp.int32)
    return lax.shift_right_logical(lo, 16) | (hi & -65536)


def _unpack_bf16_pairs(w):
    return jnp.concatenate([lax.bitcast_convert_type(lax.shift_left(w, 16), F32),
                            lax.bitcast_convert_type(w & -65536, F32)], axis=1)


def _ada_kernel(c_ref, w_ref, b_ref, o_ref):
    cact = _silu(c_ref[...]).astype(BF16)
    o_ref[0] = _dot(cact, w_ref[0].astype(BF16)) + b_ref[0]


def _ada(c, w_ada, b_ada):
    depth, d, n = w_ada.shape
    b = c.shape[0]
    tn = 1024
    return pl.pallas_call(
        _ada_kernel,
        grid=(depth, n // tn),
        in_specs=[
            pl.BlockSpec((b, d), lambda l, j: (0, 0)),
            pl.BlockSpec((1, d, tn), lambda l, j: (l, 0, j)),
            pl.BlockSpec((1, 1, tn), lambda l, j: (l, 0, j)),
        ],
        out_specs=pl.BlockSpec((1, b, tn), lambda l, j: (l, 0, j)),
        out_shape=jax.ShapeDtypeStruct((depth, b, n), F32),
        compiler_params=_cparams(("arbitrary", "arbitrary")),
        name="ada_mod",
    )(c, w_ada, b_ada.reshape(depth, 1, n))


def _modulated_norm(x, nw, sc, sh):
    ms = jnp.mean(x * x, axis=-1, keepdims=True)
    return (x * lax.rsqrt(ms + EPS) * nw) * (1.0 + sc) + sh


def _inproj_kernel(x_ref, nw_ref, sc_ref, sh_ref, w_ref, wba_ref, o_ref, ba_ref, h_scr, *, n_sub):
    @pl.when(pl.program_id(1) == 0)
    def _():
        h = _modulated_norm(x_ref[...], nw_ref[...], sc_ref[0], sh_ref[0]).astype(BF16)
        h_scr[...] = h
        ba_ref[...] = _dot(h, wba_ref[...])

    acc = _dot(h_scr[...], w_ref[...])
    for k in range(n_sub):
        o_ref[k] = acc[:, k * LANES:(k + 1) * LANES].astype(BF16)


def _inproj(x2, nw, sc, sh, w_main, w_ba, seq):
    t, d = x2.shape
    n = w_main.shape[1]
    tm = min(1024, seq)
    tn = 768
    n_sub = tn // LANES
    per_b = seq // tm
    return pl.pallas_call(
        functools.partial(_inproj_kernel, n_sub=n_sub),
        grid=(t // tm, n // tn),
        in_specs=[
            pl.BlockSpec((tm, d), lambda i, j: (i, 0)),
            pl.BlockSpec((1, d), lambda i, j: (0, 0)),
            pl.BlockSpec((1, 1, d), lambda i, j: (i // per_b, 0, 0)),
            pl.BlockSpec((1, 1, d), lambda i, j: (i // per_b, 0, 0)),
            pl.BlockSpec((d, tn), lambda i, j: (0, j)),
            pl.BlockSpec((d, LANES), lambda i, j: (0, 0)),
        ],
        out_specs=[
            pl.BlockSpec((n_sub, tm, LANES), lambda i, j: (j, i, 0)),
            pl.BlockSpec((tm, LANES), lambda i, j: (i, 0)),
        ],
        out_shape=[
            jax.ShapeDtypeStruct((n // LANES, t, LANES), BF16),
            jax.ShapeDtypeStruct((t, LANES), F32),
        ],
        scratch_shapes=[pltpu.VMEM((tm, d), BF16)],
        compiler_params=_cparams(("arbitrary", "arbitrary")),
        name="inproj",
    )(x2, nw, sc, sh, w_main, w_ba)


def _unit_lower_inverse(lms, xr):
    c = lms[0].shape[0]

    def mm(a, b):
        return [_dot(ai.astype(BF16), bi.astype(BF16)) for ai, bi in zip(a, b)]

    d1 = [jnp.where(xr < 16, lm, 0.0) for lm in lms]
    d2 = mm(d1, d1)
    d4 = mm(d2, d2)
    d8 = mm(d4, d4)
    p = [b - a - t for a, b, t in zip(d1, d2, mm(d1, d2))]
    p = [pi + di + t for pi, di, t in zip(p, d4, mm(p, d4))]
    p = [pi + di + t for pi, di, t in zip(p, d8, mm(p, d8))]
    eye = jnp.where(xr == 0, 1.0, 0.0)
    x = [eye + pi for pi in p]
    blk = 32
    while blk <= c:
        e = [jnp.where((xr < blk) & (xr >= blk // 2), lm, 0.0) for lm in lms]
        x = [xi - t for xi, t in zip(x, mm(x, mm(e, x)))]
        blk *= 2
    return x


def _gdn_kernel(q_ref, k_ref, v_ref, z_ref, ba_ref, cwq_ref, cwk_ref, cwv_ref, al_ref, dtb_ref, nw_ref,
                o_ref, xq, xk, xv, u_s, w_s, a_s, qd_s, kd_s, cd_s, *, seq, chunk):
    c = chunk
    n_chunks = seq // c
    head = pl.program_id(1)
    dk = LANES

    lane1 = lax.broadcasted_iota(jnp.int32, (1, LANES), 1)
    a_exp = jnp.exp(jnp.sum(jnp.where(lane1 == head, al_ref[...], 0.0), axis=-1, keepdims=True))
    dtb = jnp.sum(jnp.where(lane1 == head, dtb_ref[...], 0.0), axis=-1, keepdims=True)

    zeros8 = jnp.zeros((8, LANES), F32)
    for src, dst in ((q_ref, xq), (k_ref, xk), (v_ref, xv)):
        dst[0:8, :] = zeros8

        def stage(i, carry, src=src, dst=dst):
            r0 = pl.multiple_of(i * c, c)
            dst[pl.ds(r0 + 8, c), :] = src[0, pl.ds(r0, c), :].astype(F32)
            return carry

        lax.fori_loop(0, n_chunks, stage, 0)

    row = lax.broadcasted_iota(jnp.int32, (c, c), 0)
    col = lax.broadcasted_iota(jnp.int32, (c, c), 1)
    causal = row >= col
    strict = row > col
    xr = row ^ col
    tril = jnp.where(causal, 1.0, 0.0).astype(BF16)
    lane_c = lax.broadcasted_iota(jnp.int32, (c, LANES), 1)

    def conv(xs, cw_ref, r0):
        acc = cw_ref[0, GDN_CONV - 1:GDN_CONV, :] * xs[pl.ds(r0 + 8, c), :]
        for back in range(1, GDN_CONV):
            tap = GDN_CONV - 1 - back
            acc = acc + cw_ref[0, tap:tap + 1, :] * xs[pl.ds(r0 + 8 - back, c), :]
        return _silu(acc)

    def l2n(t):
        return t * lax.rsqrt(jnp.sum(t * t, axis=-1, keepdims=True) + EPS)

    def log_decay(r0):
        bat = ba_ref[pl.ds(r0, c), :]
        bcol = jnp.sum(jnp.where(lane_c == head, bat, 0.0), axis=-1, keepdims=True)
        acol = jnp.sum(jnp.where(lane_c == head + GDN_HEADS, bat, 0.0), axis=-1, keepdims=True)
        xsp = acol + dtb
        softplus = jnp.maximum(xsp, 0.0) + jnp.log1p(jnp.exp(-jnp.abs(xsp)))
        return jax.nn.sigmoid(bcol), jnp.broadcast_to(-a_exp * softplus, (c, LANES))

    def intra(step, carry):
        par = range(GDN_LOCKSTEP)
        ns = [step * GDN_LOCKSTEP + j for j in par]
        r0s = [pl.multiple_of(n * c, c) for n in ns]
        qn = [l2n(conv(xq, cwq_ref, r0)) * (dk ** -0.5) for r0 in r0s]
        kn = [l2n(conv(xk, cwk_ref, r0)) for r0 in r0s]
        vc = [conv(xv, cwv_ref, r0) for r0 in r0s]
        bg = [log_decay(r0) for r0 in r0s]
        beta = [t[0] for t in bg]
        g_hi = [t[1].astype(BF16) for t in bg]
        g_lo = [(t[1] - h.astype(F32)).astype(BF16) for t, h in zip(bg, g_hi)]
        gc = [_dot(tril, h) + _dot(tril, lo_) for h, lo_ in zip(g_hi, g_lo)]
        decay = []
        for gci in gc:
            gc_row = jnp.transpose(gci)[0:1, :]
            diff = jnp.concatenate([gci] * (c // LANES), axis=1) - gc_row
            decay.append(jnp.where(causal, jnp.exp(jnp.where(causal, diff, 0.0)), 0.0))
        kb = [k * b for k, b in zip(kn, beta)]
        knb = [k.astype(BF16) for k in kn]
        lm = [jnp.where(strict, _dot_nt(kbi.astype(BF16), ki) * dc, 0.0) for kbi, ki, dc in zip(kb, knb, decay)]
        attn = [jnp.where(causal, _dot_nt(q.astype(BF16), ki) * dc, 0.0) for q, ki, dc in zip(qn, knb, decay)]
        tinv = _unit_lower_inverse(lm, xr)
        egc = [jnp.exp(g) for g in gc]
        sol = [_dot(ti.astype(BF16), jnp.concatenate([v * b, kbi * e], axis=1).astype(BF16))
               for ti, v, b, kbi, e in zip(tinv, vc, beta, kb, egc)]
        for j in par:
            rows = pl.ds(r0s[j], c)
            u_s[rows, :] = sol[j][:, :LANES]
            w_s[rows, :] = sol[j][:, LANES:].astype(BF16)
            a_s[rows, :] = attn[j].astype(BF16)
            qd_s[rows, :] = (qn[j] * egc[j]).astype(BF16)
            g_last = gc[j][c - 1:c, :]
            kd_s[rows, :] = (kn[j] * jnp.exp(g_last - gc[j])).astype(BF16)
            cd_s[pl.ds(pl.multiple_of(ns[j] * 8, 8), 8), :] = jnp.broadcast_to(jnp.exp(g_last), (8, LANES))
        return carry

    lax.fori_loop(0, n_chunks // GDN_LOCKSTEP, intra, 0)

    def inter(n, state):
        r0 = pl.multiple_of(n * c, c)
        rows = pl.ds(r0, c)
        sb = state.astype(BF16)
        v_new = u_s[rows, :] - _dot(w_s[rows, :], sb)
        vb = v_new.astype(BF16)
        o = _dot(qd_s[rows, :], sb) + _dot(a_s[rows, :], vb)
        cd = cd_s[pl.ds(pl.multiple_of(n * 8, 8), 1), :]
        new_state = state * cd + _dot_tn(kd_s[rows, :], vb)
        on = o * lax.rsqrt(jnp.mean(o * o, axis=-1, keepdims=True) + EPS) * nw_ref[...]
        z = z_ref[0, rows, :].astype(F32)
        o_ref[0, rows, :] = (on * _silu(z)).astype(BF16)
        return new_state

    lax.fori_loop(0, n_chunks, inter, jnp.zeros((dk, LANES), F32))


def _gdn(proj, ba, conv_w, a_log, dt_bias, gdn_norm, batch, seq):
    c = GDN_CHUNK
    per = seq
    cw = conv_w.reshape(GDN_CONV, 3 * GDN_HEADS, LANES).transpose(1, 0, 2)
    pad = lambda v: jnp.pad(v, (0, LANES - v.shape[0])).reshape(1, LANES)

    def slab(base):
        return pl.BlockSpec((1, per, LANES), lambda b, h, base=base: (base + h, b, 0))

    def cwspec(base):
        return pl.BlockSpec((1, GDN_CONV, LANES), lambda b, h, base=base: (base + h, 0, 0))

    vec = pl.BlockSpec((1, LANES), lambda b, h: (0, 0))
    return pl.pallas_call(
        functools.partial(_gdn_kernel, seq=seq, chunk=c),
        grid=(batch, GDN_HEADS),
        in_specs=[slab(SLAB_QA), slab(SLAB_KA), slab(SLAB_VA), slab(SLAB_ZA),
                  pl.BlockSpec((per, LANES), lambda b, h: (b, 0)),
                  cwspec(0), cwspec(GDN_HEADS), cwspec(2 * GDN_HEADS), vec, vec, vec],
        out_specs=pl.BlockSpec((1, per, LANES), lambda b, h: (h, b, 0)),
        out_shape=jax.ShapeDtypeStruct((GDN_HEADS, batch * seq, LANES), BF16),
        scratch_shapes=[
            pltpu.VMEM((seq + 8, LANES), F32), pltpu.VMEM((seq + 8, LANES), F32),
            pltpu.VMEM((seq + 8, LANES), F32),
            pltpu.VMEM((seq, LANES), F32),
            pltpu.VMEM((seq, LANES), BF16),
            pltpu.VMEM((seq, c), BF16),
            pltpu.VMEM((seq, LANES), BF16),
            pltpu.VMEM((seq, LANES), BF16),
            pltpu.VMEM((seq // c * 8, LANES), F32),
        ],
        compiler_params=_cparams(("arbitrary", "arbitrary")),
        name="gdn_mixer",
    )(proj, proj, proj, proj, ba, cw, cw, cw, pad(a_log), pad(dt_bias), gdn_norm.reshape(1, LANES))


def _diff_kernel(q_ref, k_ref, v_ref, qw_ref, kw_ref, sw_ref, lam_ref, o_ref,
                 q1_s, q2_s, kn_s, ve_s, sc1, sc2, acc1, acc2, m1, m2, *, seq, tq, tk, lam_init):
    dh = DIFF_QK_DIM
    n_blk = seq // tq
    per_q = tq // tk
    lane = lax.broadcasted_iota(jnp.int32, (1, LANES), 1)
    lo = lane < dh

    lam = (jnp.exp(jnp.sum(lam_ref[0:1, :] * lam_ref[1:2, :], axis=-1, keepdims=True))
           - jnp.exp(jnp.sum(lam_ref[2:3, :] * lam_ref[3:4, :], axis=-1, keepdims=True)) + lam_init)

    def halfnorm(t, w):
        sq = t * t
        s_lo = jnp.sum(jnp.where(lo, sq, 0.0), axis=-1, keepdims=True)
        s_hi = jnp.sum(jnp.where(lo, 0.0, sq), axis=-1, keepdims=True)
        r = jnp.where(lo, lax.rsqrt(s_lo / dh + EPS), lax.rsqrt(s_hi / dh + EPS))
        return t * r * w

    ones_blk = jnp.ones((tk, LANES), BF16)

    def prep(i, carry):
        rows = pl.ds(pl.multiple_of(i * tk, tk), tk)
        kn_s[rows, :] = halfnorm(k_ref[0, rows, :].astype(F32), kw_ref[...]).astype(BF16)
        qn = halfnorm(q_ref[0, rows, :].astype(F32), qw_ref[...]) * (dh ** -0.5)
        q1_s[rows, :] = jnp.where(lo, qn, 0.0).astype(BF16)
        q2_s[rows, :] = jnp.where(lo, 0.0, qn).astype(BF16)
        ve_s[rows, 0:LANES] = v_ref[0, rows, :]
        ve_s[rows, LANES:2 * LANES] = ones_blk
        return carry

    lax.fori_loop(0, seq // tk, prep, 0)

    row = lax.broadcasted_iota(jnp.int32, (tq, tk), 0)
    col = lax.broadcasted_iota(jnp.int32, (tq, tk), 1)

    n_parts = tk // LANES

    def q_block(qi, carry):
        qrows = pl.ds(pl.multiple_of(qi * tq, tq), tq)
        maps = ((q1_s[qrows, :], sc1, m1, acc1), (q2_s[qrows, :], sc2, m2, acc2))
        for _, _, m_r, a_r in maps:
            m_r[...] = jnp.full((tq, LANES), -jnp.inf, F32)
            a_r[...] = jnp.zeros((tq, 2 * LANES), F32)

        def score_step(kj, masked):
            kb = kn_s[pl.ds(pl.multiple_of(kj * tk, tk), tk), :]
            for qq, s_c, m_r, _ in maps:
                s = _dot_nt(qq, kb)
                if masked:
                    s = jnp.where(row + qi * tq >= col + kj * tk, s, -jnp.inf)
                s_c[kj] = s
                smax = m_r[...]
                for part in range(n_parts):
                    smax = jnp.maximum(smax, s[:, part * LANES:(part + 1) * LANES])
                m_r[...] = smax

        def off_diag(kq, c2):
            for part in range(per_q):
                score_step(kq * per_q + part, False)
            return c2

        lax.fori_loop(0, qi, off_diag, 0)
        for dblk in range(per_q):
            score_step(qi * per_q + dblk, True)
        for _, _, m_r, _ in maps:
            m_r[...] = jnp.broadcast_to(jnp.max(m_r[...], axis=-1, keepdims=True), (tq, LANES))

        def value_step(kq, c2):
            for _, s_c, m_r, a_r in maps:
                mrep = m_r[...]
                upd = a_r[...]
                for sub in range(per_q):
                    kj = kq * per_q + sub
                    s = s_c[kj]
                    p = jnp.concatenate(
                        [jnp.exp(s[:, part * LANES:(part + 1) * LANES] - mrep) for part in range(n_parts)], axis=1)
                    upd = upd + _dot(p.astype(BF16), ve_s[pl.ds(pl.multiple_of(kj * tk, tk), tk), :])
                a_r[...] = upd
            return c2

        lax.fori_loop(0, qi + 1, value_step, 0)
        o = (acc1[:, 0:LANES] / acc1[:, LANES:2 * LANES]
             - lam * (acc2[:, 0:LANES] / acc2[:, LANES:2 * LANES]))
        o = o * lax.rsqrt(jnp.mean(o * o, axis=-1, keepdims=True) + EPS) * sw_ref[...] * (1.0 - lam_init)
        o_ref[0, qrows, :] = o.astype(BF16)
        return carry

    lax.fori_loop(0, n_blk, q_block, 0)


def _diff(proj, q_norm, k_norm, subln, lam_vecs, lam_init, batch, seq):
    tq = 512
    tk = 256
    per = seq

    def slab(base):
        return pl.BlockSpec((1, per, LANES), lambda b, h, base=base: (base + h, b, 0))

    vec = pl.BlockSpec((1, LANES), lambda b, h: (0, 0))
    tile2 = lambda w: jnp.concatenate([w, w]).reshape(1, LANES)
    return pl.pallas_call(
        functools.partial(_diff_kernel, seq=seq, tq=tq, tk=tk, lam_init=lam_init),
        grid=(batch, DIFF_HEADS),
        in_specs=[slab(SLAB_QB), slab(SLAB_KB), slab(SLAB_VB), vec, vec, vec,
                  pl.BlockSpec((4, DIFF_QK_DIM), lambda b, h: (0, 0))],
        out_specs=pl.BlockSpec((1, per, LANES), lambda b, h: (h, b, 0)),
        out_shape=jax.ShapeDtypeStruct((DIFF_HEADS, batch * seq, LANES), BF16),
        scratch_shapes=[
            pltpu.VMEM((seq, LANES), BF16), pltpu.VMEM((seq, LANES), BF16), pltpu.VMEM((seq, LANES), BF16),
            pltpu.VMEM((seq, 2 * LANES), BF16),
            pltpu.VMEM((seq // tk, tq, tk), F32), pltpu.VMEM((seq // tk, tq, tk), F32),
            pltpu.VMEM((tq, 2 * LANES), F32), pltpu.VMEM((tq, 2 * LANES), F32),
            pltpu.VMEM((tq, LANES), F32), pltpu.VMEM((tq, LANES), F32),
        ],
        compiler_params=_cparams(("arbitrary", "arbitrary")),
        name="diff_mixer",
    )(proj, proj, proj, tile2(q_norm), tile2(k_norm), subln.reshape(1, LANES), lam_vecs)


def _dil_kernel(q_ref, k_ref, v_ref, qw_ref, kw_ref, o_ref, qn_s, kn_s, vf_s, sc_s, mb_s, acc_s, l_s, m_s, *, seq):
    blk = DIL_BLOCK
    dh = LANES
    pc = 256
    n_pc = seq // pc
    unroll = 4

    def rms(t, w):
        return t * lax.rsqrt(jnp.mean(t * t, axis=-1, keepdims=True) + EPS) * w

    ones_kv = jnp.ones((2 * blk, LANES), BF16)

    def prep(i, carry):
        rows = pl.ds(pl.multiple_of(i * pc, pc), pc)
        qn_s[rows, :] = rms(q_ref[0, rows, :].astype(F32), qw_ref[...]) * (dh ** -0.5)
        kn_s[rows, :] = rms(k_ref[0, rows, :].astype(F32), kw_ref[...])
        vf_s[rows, :] = v_ref[0, rows, :].astype(F32)
        return carry

    lax.fori_loop(0, n_pc, prep, 0)

    qi = lax.broadcasted_iota(jnp.int32, (blk, 2 * blk), 0)
    kj = lax.broadcasted_iota(jnp.int32, (blk, 2 * blk), 1)

    for set_idx, (window, dil) in enumerate(DILATED_PAIRS):
        steps = window // dil
        n_sub = seq // dil
        nb = n_sub // blk
        assert steps == blk and nb >= 2 and (dil * nb) % unroll == 0

        def slices(idx, dil=dil, nb=nb):
            r = idx // nb
            n = idx - r * nb
            kbase = jnp.maximum(n - 1, 0) * blk
            if dil > 1:
                return n, kbase, pl.ds(r + dil * blk * n, blk, stride=dil), pl.ds(r + dil * kbase, 2 * blk, stride=dil)
            return n, kbase, pl.ds(pl.multiple_of(blk * n, blk), blk), pl.ds(pl.multiple_of(kbase, blk), 2 * blk)

        def score_block(idx, carry):
            n, kbase, qsl, ksl = slices(idx)
            s = _dot_nt(qn_s[qsl, :].astype(BF16), kn_s[ksl, :].astype(BF16))
            dist = (n * blk + qi) - (kbase + kj)
            s = jnp.where((dist >= 0) & (dist <= steps), s, -jnp.inf)
            sc_s[idx] = s
            m = jnp.max(jnp.maximum(s[:, 0:blk], s[:, blk:2 * blk]), axis=-1, keepdims=True)
            mb_s[idx] = jnp.broadcast_to(m, (blk, LANES))
            return carry

        lax.fori_loop(0, dil * nb, score_block, 0, unroll=unroll)

        def value_block(idx, carry, first=(set_idx == 0)):
            _, _, qsl, ksl = slices(idx)
            s = sc_s[idx]
            m = mb_s[idx]
            p = jnp.concatenate([jnp.exp(s[:, 0:blk] - m), jnp.exp(s[:, blk:2 * blk] - m)], axis=1)
            ve = jnp.concatenate([vf_s[ksl, :].astype(BF16), ones_kv], axis=1)
            ol = _dot(p.astype(BF16), ve)
            o = ol[:, 0:LANES]
            l = ol[:, LANES:2 * LANES]
            if first:
                acc_s[qsl, :] = o
                l_s[qsl, :] = l
                m_s[qsl, :] = m
            else:
                m_old = m_s[qsl, :]
                m_new = jnp.maximum(m_old, m)
                a_old = jnp.exp(m_old - m_new)
                a_cur = jnp.exp(m - m_new)
                acc_s[qsl, :] = a_old * acc_s[qsl, :] + a_cur * o
                l_s[qsl, :] = a_old * l_s[qsl, :] + a_cur * l
                m_s[qsl, :] = m_new
            return carry

        lax.fori_loop(0, dil * nb, value_block, 0, unroll=unroll)

    def fin(i, carry):
        rows = pl.ds(pl.multiple_of(i * pc, pc), pc)
        o_ref[0, rows, :] = (acc_s[rows, :] / l_s[rows, :]).astype(BF16)
        return carry

    lax.fori_loop(0, n_pc, fin, 0)


def _dil(proj, q_norm, k_norm, batch, seq):
    per = seq

    def slab(base):
        return pl.BlockSpec((1, per, LANES), lambda b, h, base=base: (base + h, b, 0))

    vec = pl.BlockSpec((1, LANES), lambda b, h: (0, 0))
    return pl.pallas_call(
        functools.partial(_dil_kernel, seq=seq),
        grid=(batch, DIL_HEADS),
        in_specs=[slab(SLAB_QC), slab(SLAB_KC), slab(SLAB_VC), vec, vec],
        out_specs=pl.BlockSpec((1, per, LANES), lambda b, h: (h, b, 0)),
        out_shape=jax.ShapeDtypeStruct((DIL_HEADS, batch * seq, LANES), BF16),
        scratch_shapes=[
            pltpu.VMEM((seq, LANES), F32), pltpu.VMEM((seq, LANES), F32),
            pltpu.VMEM((seq, LANES), F32),
            pltpu.VMEM((seq // DIL_BLOCK, DIL_BLOCK, 2 * DIL_BLOCK), F32),
            pltpu.VMEM((seq // DIL_BLOCK, DIL_BLOCK, LANES), F32),
            pltpu.VMEM((seq, LANES), F32), pltpu.VMEM((seq, LANES), F32),
            pltpu.VMEM((seq, LANES), F32),
        ],
        compiler_params=_cparams(("arbitrary", "arbitrary")),
        name="dil_mixer",
    )(proj, proj, proj, q_norm.reshape(1, LANES), k_norm.reshape(1, LANES))


def _route_tile(sc, bias, run):
    ne, tm = sc.shape
    epg = EXPERTS_PER_GROUP
    sel = sc + bias
    jrow = lax.broadcasted_iota(jnp.int32, (epg, tm), 0)
    gs, i1s, i2s = [], [], []
    for g in range(N_GROUPS):
        sg = sel[g * epg:(g + 1) * epg, :]
        m1 = jnp.max(sg, axis=0, keepdims=True)
        i1 = jnp.min(jnp.where(sg == m1, jrow, epg), axis=0, keepdims=True)
        rest = jnp.where(jrow == i1, -jnp.inf, sg)
        m2 = jnp.max(rest, axis=0, keepdims=True)
        i2 = jnp.min(jnp.where(rest == m2, jrow, epg), axis=0, keepdims=True)
        gs.append(m1 + m2)
        i1s.append(i1)
        i2s.append(i2)
    gmax = functools.reduce(jnp.maximum, gs)
    gidx = jnp.full((1, tm), N_GROUPS - 1, jnp.int32)
    for g in reversed(range(N_GROUPS - 1)):
        gidx = jnp.where(gs[g] == gmax, g, gidx)
    loc1 = jnp.zeros((1, tm), jnp.int32)
    loc2 = jnp.zeros((1, tm), jnp.int32)
    for g in range(N_GROUPS):
        loc1 = jnp.where(gidx == g, i1s[g], loc1)
        loc2 = jnp.where(gidx == g, i2s[g], loc2)
    e1 = gidx * epg + loc1
    e2 = gidx * epg + loc2
    erow = lax.broadcasted_iota(jnp.int32, (ne, tm), 0)
    oh1 = erow == e1
    oh2 = erow == e2
    s1 = jnp.sum(jnp.where(oh1, sc, 0.0), axis=0, keepdims=True)
    s2 = jnp.sum(jnp.where(oh2, sc, 0.0), axis=0, keepdims=True)
    den = s1 + s2
    tr = lax.broadcasted_iota(jnp.int32, (tm, tm), 0)
    tc = lax.broadcasted_iota(jnp.int32, (tm, tm), 1)
    before = jnp.where(tr < tc, 1.0, 0.0).astype(BF16)
    f1 = jnp.where(oh1, 1.0, 0.0)
    f2 = jnp.where(oh2, 1.0, 0.0)
    pre1 = _dot(f1.astype(BF16), before)
    pre2 = _dot(f2.astype(BF16), before)
    tot1 = jnp.sum(f1, axis=1, keepdims=True)
    tot2 = jnp.sum(f2, axis=1, keepdims=True)
    r1 = jnp.sum(jnp.where(oh1, pre1 + run, 0.0), axis=0, keepdims=True)
    r2 = jnp.sum(jnp.where(oh2, pre2 + (run + tot1), 0.0), axis=0, keepdims=True)
    return e1, e2, s1 / den, s2 / den, r1, r2, run + tot1 + tot2


def _outproj_kernel(oa_ref, ob_ref, oc_ref, w_ref, x_ref, g_ref, nw_ref, sc_ref, sh_ref, wrt_ref, rb_ref,
                    xo_ref, hf_ref, ri_ref, rw_ref, cnt_ref, mix_s, run_s):
    @pl.when(pl.program_id(0) == 0)
    def _():
        run_s[...] = jnp.zeros(run_s.shape, F32)

    off = 0
    for ref, heads in ((oa_ref, GDN_HEADS), (ob_ref, DIFF_HEADS), (oc_ref, DIL_HEADS)):
        for k in range(heads):
            mix_s[:, off:off + LANES] = ref[k]
            off += LANES
    xn = x_ref[...] + g_ref[0] * _dot(mix_s[...], w_ref[...])
    xo_ref[...] = xn
    hb = _modulated_norm(xn, nw_ref[...], sc_ref[0], sh_ref[0]).astype(BF16)
    hf_ref[...] = _pack_bf16_pairs(hb)
    scores_t = jax.nn.sigmoid(_dot_nt(wrt_ref[...], hb))
    e1, e2, w1, w2, r1, r2, run = _route_tile(scores_t, rb_ref[...], run_s[...])
    run_s[...] = run
    cnt_ref[...] = jnp.broadcast_to(run, cnt_ref.shape)
    tm = e1.shape[1]
    ri_ref[...] = jnp.concatenate([e1, e2, r1.astype(jnp.int32), r2.astype(jnp.int32),
                                   jnp.zeros((4, tm), jnp.int32)], axis=0)
    rw_ref[...] = jnp.transpose(jnp.concatenate([w1, w2, jnp.zeros((6, tm), F32)], axis=0))


def _outproj(o_a, o_b, o_c, w_out, x2, g_a, nw, sc, sh, w_router_t, router_bias, seq):
    t, d = x2.shape
    tm = 256
    per_b = seq // tm
    ne = w_router_t.shape[0]
    row = pl.BlockSpec((tm, d), lambda i: (i, 0))
    bvec = pl.BlockSpec((1, 1, d), lambda i: (i // per_b, 0, 0))
    return pl.pallas_call(
        _outproj_kernel,
        grid=(t // tm,),
        in_specs=[
            pl.BlockSpec((GDN_HEADS, tm, LANES), lambda i: (0, i, 0)),
            pl.BlockSpec((DIFF_HEADS, tm, LANES), lambda i: (0, i, 0)),
            pl.BlockSpec((DIL_HEADS, tm, LANES), lambda i: (0, i, 0)),
            pl.BlockSpec((d, d), lambda i: (0, 0)),
            row, bvec,
            pl.BlockSpec((1, d), lambda i: (0, 0)),
            bvec, bvec,
            pl.BlockSpec((ne, d), lambda i: (0, 0)),
            pl.BlockSpec((ne, 1), lambda i: (0, 0)),
        ],
        out_specs=[row,
                   pl.BlockSpec((tm, d // 2), lambda i: (i, 0)),
                   pl.BlockSpec((8, tm), lambda i: (0, i)),
                   pl.BlockSpec((tm, 8), lambda i: (i, 0)),
                   pl.BlockSpec((ne, LANES), lambda i: (0, 0))],
        out_shape=[jax.ShapeDtypeStruct((t, d), F32), jax.ShapeDtypeStruct((t, d // 2), jnp.int32),
                   jax.ShapeDtypeStruct((8, t), jnp.int32), jax.ShapeDtypeStruct((t, 8), F32),
                   jax.ShapeDtypeStruct((ne, LANES), F32)],
        scratch_shapes=[pltpu.VMEM((tm, d), BF16), pltpu.VMEM((ne, 1), F32)],
        compiler_params=_cparams(("arbitrary",)),
        name="outproj",
    )(o_a, o_b, o_c, w_out, x2, g_a, nw, sc, sh, w_router_t, router_bias.reshape(ne, 1).astype(F32))


def _moe_kernel(be_ref, nv_ref, tok_ref, tok_next_ref, dst_prev_ref, dst_ref, hf_hbm, wg_ref, wu_ref, wd_ref,
                y_hbm, xbuf, ybuf, wg_s, wu_s, wd_s, gsem, ssem):
    i = pl.program_id(0)
    nv = nv_ref[0]
    slot = i % 2
    rows = MOE_ROWS

    def gather_copy(tok_r, r, s):
        return pltpu.make_async_copy(hf_hbm.at[pl.ds(tok_r[0, 0, r], 1)], xbuf.at[s, pl.ds(r, 1)], gsem.at[s])

    def scatter_copy(dst_r, r, s):
        return pltpu.make_async_copy(ybuf.at[s, pl.ds(r, 1)], y_hbm.at[pl.ds(dst_r[0, 0, r], 1)], ssem.at[s])

    @pl.when(i == 0)
    def _():
        ybuf[...] = jnp.zeros(ybuf.shape, jnp.int32)
        for r in range(rows):
            gather_copy(tok_ref, r, 0).start()

    @pl.when((i == 0) | (be_ref[i] != be_ref[jnp.maximum(i - 1, 0)]))
    def _():
        wg_s[...] = wg_ref[0].astype(BF16)
        wu_s[...] = wu_ref[0].astype(BF16)
        wd_s[...] = wd_ref[0].astype(BF16)

    @pl.when((i >= 1) & (i < nv))
    def _():
        for r in range(rows):
            scatter_copy(dst_ref, r, slot).wait()

    @pl.when(i < nv)
    def _():
        for r in range(rows):
            gather_copy(tok_ref, r, slot).wait()
        x = _unpack_bf16_pairs(xbuf[slot]).astype(BF16)
        for r in range(rows):
            gather_copy(tok_next_ref, r, 1 - slot).start()
            scatter_copy(dst_prev_ref, r, 1 - slot).start()
        a = _dot(x, wg_s[...])
        u = _dot(x, wu_s[...])
        ybuf[slot] = _pack_bf16_pairs(_dot((_silu(a) * u).astype(BF16), wd_s[...]).astype(BF16))

    @pl.when(i == nv - 1)
    def _():
        for r in range(rows):
            gather_copy(tok_next_ref, r, 1 - slot).wait()
            scatter_copy(dst_prev_ref, r, 1 - slot).wait()
        for r in range(rows):
            scatter_copy(dst_ref, r, slot).start()
        for r in range(rows):
            scatter_copy(dst_ref, r, slot).wait()


def _moe(hf, block_e, n_valid, row_tok, row_dst, w_gate, w_up, w_down, layer):
    t = hf.shape[0]
    d = w_gate.shape[2]
    f = w_gate.shape[3]
    rows = MOE_ROWS
    nb = block_e.shape[0]
    tok3 = row_tok.reshape(nb, 1, rows)
    spare = (TOP_K * t + jnp.arange(rows, dtype=jnp.int32)).reshape(1, 1, rows)
    dst3 = jnp.concatenate([spare, row_dst.reshape(nb, 1, rows)], axis=0)
    smem_blk = lambda fn: pl.BlockSpec((1, 1, rows), fn, memory_space=pltpu.SMEM)
    grid_spec = pltpu.PrefetchScalarGridSpec(
        num_scalar_prefetch=2,
        grid=(nb,),
        in_specs=[
            smem_blk(lambda i, be, nv: (i, 0, 0)),
            smem_blk(lambda i, be, nv: (jnp.minimum(i + 1, nb - 1), 0, 0)),
            smem_blk(lambda i, be, nv: (i, 0, 0)),
            smem_blk(lambda i, be, nv: (i + 1, 0, 0)),
            pl.BlockSpec(memory_space=pl.ANY),
            pl.BlockSpec((None, 1, d, f), lambda i, be, nv: (layer, be[i], 0, 0)),
            pl.BlockSpec((None, 1, d, f), lambda i, be, nv: (layer, be[i], 0, 0)),
            pl.BlockSpec((None, 1, f, d), lambda i, be, nv: (layer, be[i], 0, 0)),
        ],
        out_specs=pl.BlockSpec(memory_space=pl.ANY),
        scratch_shapes=[
            pltpu.VMEM((2, rows, d // 2), jnp.int32),
            pltpu.VMEM((2, rows, d // 2), jnp.int32),
            pltpu.VMEM((d, f), BF16), pltpu.VMEM((d, f), BF16), pltpu.VMEM((f, d), BF16),
            pltpu.SemaphoreType.DMA((2,)),
            pltpu.SemaphoreType.DMA((2,)),
        ],
    )
    return pl.pallas_call(
        _moe_kernel,
        grid_spec=grid_spec,
        out_shape=jax.ShapeDtypeStruct((TOP_K * t + rows, d // 2), jnp.int32),
        compiler_params=_cparams(("arbitrary",), has_side_effects=True, disable_bounds_checks=True),
        name="moe_experts",
    )(block_e, n_valid, tok3, tok3, dst3, dst3, hf, w_gate, w_up, w_down)


def _block_layout(route_i, counts_f):
    t = route_i.shape[1]
    rows = MOE_ROWS
    n_assign = TOP_K * t
    counts = counts_f[:, 0].astype(jnp.int32)
    blocks_per_e = (counts + rows - 1) // rows
    blk_end = jnp.cumsum(blocks_per_e)
    blk_start = blk_end - blocks_per_e
    flat_e = route_i[0:TOP_K].reshape(n_assign)
    dest = blk_start[flat_e] * rows + route_i[TOP_K:2 * TOP_K].reshape(n_assign)
    nb = n_assign // rows + N_EXPERTS
    n_rows = nb * rows
    assign = jnp.arange(n_assign, dtype=jnp.int32)
    spare = n_assign + jnp.arange(n_rows, dtype=jnp.int32) % rows
    row_dst = spare.at[dest].set(assign)
    row_tok = jnp.where(row_dst < n_assign, row_dst % t, 0)
    n_valid = blk_end[-1].astype(jnp.int32)
    bidx = jnp.arange(nb, dtype=jnp.int32)
    block_e = jnp.sum((blk_end[None, :] <= bidx[:, None]).astype(jnp.int32), axis=1)
    block_e = jnp.minimum(block_e, N_EXPERTS - 1)
    last_e = block_e[jnp.maximum(n_valid - 1, 0)]
    block_e = jnp.where(bidx < n_valid, block_e, last_e)
    return block_e, n_valid.reshape(1), row_tok, row_dst


def _combine_kernel(x_ref, y0_ref, y1_ref, rw_ref, g_ref, o_ref):
    w = rw_ref[...]
    y0 = _unpack_bf16_pairs(y0_ref[...])
    y1 = _unpack_bf16_pairs(y1_ref[...])
    o_ref[...] = x_ref[...] + g_ref[0] * (y0 * w[:, 0:1] + y1 * w[:, 1:2])


def _combine(x2, y2, route_w, g_f, seq):
    t, d = x2.shape
    tm = 256
    per_b = seq // tm
    n_blk = t // tm
    return pl.pallas_call(
        _combine_kernel,
        grid=(n_blk,),
        in_specs=[pl.BlockSpec((tm, d), lambda i: (i, 0)),
                  pl.BlockSpec((tm, d // 2), lambda i: (i, 0)),
                  pl.BlockSpec((tm, d // 2), lambda i: (n_blk + i, 0)),
                  pl.BlockSpec((tm, 8), lambda i: (i, 0)),
                  pl.BlockSpec((1, 1, d), lambda i: (i // per_b, 0, 0))],
        out_specs=pl.BlockSpec((tm, d), lambda i: (i, 0)),
        out_shape=jax.ShapeDtypeStruct((t, d), F32),
        compiler_params=_cparams(("arbitrary",)),
        name="moe_combine",
    )(x2, y2, y2, route_w, g_f)


def kernel(x, c, norm_mix, norm_ffn, w_ada, b_ada, w_in, conv_w, a_log, dt_bias, gdn_norm, diff_q_norm, diff_k_norm, lam_q1, lam_k1, lam_q2, lam_k2, diff_subln, dil_q_norm, dil_k_norm, w_out, w_router, router_bias, w_gate, w_up, w_down):
    batch, seq, d = x.shape
    depth = w_ada.shape[0]
    t = batch * seq
    x2 = x.reshape(t, d)
    mod = _ada(c, w_ada, b_ada).reshape(depth, batch, N_MOD, 1, d)
    ba_lo = 3 * GDN_HEADS * LANES + GDN_HEADS * LANES
    ba_hi = ba_lo + 2 * GDN_HEADS
    w_router_t = w_router.T.astype(BF16)
    for l in range(depth):
        sh_a, sc_a, g_a, sh_f, sc_f, g_f = (mod[l, :, k] for k in range(N_MOD))
        w_main = jnp.concatenate([w_in[l, :, :ba_lo], w_in[l, :, ba_hi:]], axis=1).astype(BF16)
        w_ba = jnp.pad(w_in[l, :, ba_lo:ba_hi], ((0, 0), (0, LANES - 2 * GDN_HEADS))).astype(BF16)
        proj, ba = _inproj(x2, norm_mix[l].reshape(1, d), sc_a, sh_a, w_main, w_ba, seq)
        o_a = _gdn(proj, ba, conv_w[l], a_log[l], dt_bias[l], gdn_norm[l], batch, seq)
        lam_init = 0.8 - 0.6 * math.exp(-0.3 * l)
        lam_vecs = jnp.stack([lam_q1[l], lam_k1[l], lam_q2[l], lam_k2[l]]).astype(F32)
        o_b = _diff(proj, diff_q_norm[l], diff_k_norm[l], diff_subln[l], lam_vecs, lam_init, batch, seq)
        o_c = _dil(proj, dil_q_norm[l], dil_k_norm[l], batch, seq)
        x2, hf, route_i, route_w, counts = _outproj(o_a, o_b, o_c, w_out[l].astype(BF16), x2, g_a,
                                                    norm_ffn[l].reshape(1, d), sc_f, sh_f, w_router_t, router_bias, seq)
        block_e, n_valid, row_tok, row_dst = _block_layout(route_i, counts)
        y2 = _moe(hf, block_e, n_valid, row_tok, row_dst, w_gate, w_up, w_down, l)
        x2 = _combine(x2, y2, route_w, g_f, seq)
    return x2.reshape(batch, seq, d)
```

```python
import functools
import math

import jax
import jax.numpy as jnp
from jax import lax
from jax.experimental import pallas as pl
from jax.experimental.pallas import tpu as pltpu

F32 = jnp.float32
BF16 = jnp.bfloat16

LANES = 128
EPS = 1e-6
N_MOD = 6

GDN_HEADS = 6
GDN_CONV = 4
GDN_CHUNK = 256
GDN_LOCKSTEP = 4
DIFF_HEADS = 4
DIFF_QK_DIM = 64
DIL_HEADS = 6
DILATED_PAIRS = ((128, 1), (512, 4), (2048, 16))
DIL_BLOCK = 128

N_EXPERTS = 64
N_GROUPS = 8
EXPERTS_PER_GROUP = N_EXPERTS // N_GROUPS
TOP_K = 2
MOE_ROWS = 256

SLAB_QA, SLAB_KA, SLAB_VA, SLAB_ZA = 0, 6, 12, 18
SLAB_QB, SLAB_KB, SLAB_VB = 24, 28, 32
SLAB_QC, SLAB_KC, SLAB_VC = 36, 42, 48
N_SLABS = 54

VMEM_LIMIT = 56 * 1024 * 1024


def _cparams(sem, vmem=VMEM_LIMIT, **kw):
    return pltpu.CompilerParams(dimension_semantics=sem, vmem_limit_bytes=vmem, **kw)


def _silu(v):
    return v * jax.nn.sigmoid(v)


def _dot(a, b):
    return jnp.dot(a, b, preferred_element_type=F32)


def _dot_nt(a, b):
    return lax.dot_general(a, b, (((1,), (1,)), ((), ())), preferred_element_type=F32)


def _dot_tn(a, b):
    return lax.dot_general(a, b, (((0,), (0,)), ((), ())), preferred_element_type=F32)


def _pack_bf16_pairs(v):
    half = v.shape[1] // 2
    lo = lax.bitcast_convert_type(v[:, :half].astype(F32), jnp.int32)
    hi = lax.bitcast_convert_type(v[:, half:].astype(F32), jnp.int32)
    return lax.shift_right_logical(lo, 16) | (hi & -65536)


def _unpack_bf16_pairs(w):
    return jnp.concatenate([lax.bitcast_convert_type(lax.shift_left(w, 16), F32),
                            lax.bitcast_convert_type(w & -65536, F32)], axis=1)


def _ada_kernel(c_ref, w_ref, b_ref, o_ref):
    cact = _silu(c_ref[...]).astype(BF16)
    o_ref[0] = _dot(cact, w_ref[0].astype(BF16)) + b_ref[0]


def _ada(c, w_ada, b_ada):
    depth, d, n = w_ada.shape
    b = c.shape[0]
    tn = 1024
    return pl.pallas_call(
        _ada_kernel,
        grid=(depth, n // tn),
        in_specs=[
            pl.BlockSpec((b, d), lambda l, j: (0, 0)),
            pl.BlockSpec((1, d, tn), lambda l, j: (l, 0, j)),
            pl.BlockSpec((1, 1, tn), lambda l, j: (l, 0, j)),
        ],
        out_specs=pl.BlockSpec((1, b, tn), lambda l, j: (l, 0, j)),
        out_shape=jax.ShapeDtypeStruct((depth, b, n), F32),
        compiler_params=_cparams(("arbitrary", "arbitrary")),
        name="ada_mod",
    )(c, w_ada, b_ada.reshape(depth, 1, n))


def _modulated_norm(x, nw, sc, sh):
    ms = jnp.mean(x * x, axis=-1, keepdims=True)
    return (x * lax.rsqrt(ms + EPS) * nw) * (1.0 + sc) + sh


def _inproj_kernel(x_ref, nw_ref, sc_ref, sh_ref, w_ref, wba_ref, o_ref, ba_ref, h_scr, *, n_sub):
    @pl.when(pl.program_id(1) == 0)
    def _():
        h = _modulated_norm(x_ref[...], nw_ref[...], sc_ref[0], sh_ref[0]).astype(BF16)
        h_scr[...] = h
        ba_ref[...] = _dot(h, wba_ref[...])

    acc = _dot(h_scr[...], w_ref[...])
    for k in range(n_sub):
        o_ref[k] = acc[:, k * LANES:(k + 1) * LANES].astype(BF16)


def _inproj(x2, nw, sc, sh, w_main, w_ba, seq):
    t, d = x2.shape
    n = w_main.shape[1]
    tm = min(1024, seq)
    tn = 768
    n_sub = tn // LANES
    per_b = seq // tm
    return pl.pallas_call(
        functools.partial(_inproj_kernel, n_sub=n_sub),
        grid=(t // tm, n // tn),
        in_specs=[
            pl.BlockSpec((tm, d), lambda i, j: (i, 0)),
            pl.BlockSpec((1, d), lambda i, j: (0, 0)),
            pl.BlockSpec((1, 1, d), lambda i, j: (i // per_b, 0, 0)),
            pl.BlockSpec((1, 1, d), lambda i, j: (i // per_b, 0, 0)),
            pl.BlockSpec((d, tn), lambda i, j: (0, j)),
            pl.BlockSpec((d, LANES), lambda i, j: (0, 0)),
        ],
        out_specs=[
            pl.BlockSpec((n_sub, tm, LANES), lambda i, j: (j, i, 0)),
            pl.BlockSpec((tm, LANES), lambda i, j: (i, 0)),
        ],
        out_shape=[
            jax.ShapeDtypeStruct((n // LANES, t, LANES), BF16),
            jax.ShapeDtypeStruct((t, LANES), F32),
        ],
        scratch_shapes=[pltpu.VMEM((tm, d), BF16)],
        compiler_params=_cparams(("arbitrary", "arbitrary")),
        name="inproj",
    )(x2, nw, sc, sh, w_main, w_ba)


def _unit_lower_inverse(lms, xr):
    c = lms[0].shape[0]

    def mm(a, b):
        return [_dot(ai.astype(BF16), bi.astype(BF16)) for ai, bi in zip(a, b)]

    d1 = [jnp.where(xr < 16, lm, 0.0) for lm in lms]
    d2 = mm(d1, d1)
    d4 = mm(d2, d2)
    d8 = mm(d4, d4)
    p = [b - a - t for a, b, t in zip(d1, d2, mm(d1, d2))]
    p = [pi + di + t for pi, di, t in zip(p, d4, mm(p, d4))]
    p = [pi + di + t for pi, di, t in zip(p, d8, mm(p, d8))]
    eye = jnp.where(xr == 0, 1.0, 0.0)
    x = [eye + pi for pi in p]
    blk = 32
    while blk <= c:
        e = [jnp.where((xr < blk) & (xr >= blk // 2), lm, 0.0) for lm in lms]
        x = [xi - t for xi, t in zip(x, mm(x, mm(e, x)))]
        blk *= 2
    return x


def _gdn_kernel(q_ref, k_ref, v_ref, z_ref, ba_ref, cwq_ref, cwk_ref, cwv_ref, al_ref, dtb_ref, nw_ref,
                o_ref, xq, xk, xv, u_s, w_s, a_s, qd_s, kd_s, cd_s, *, seq, chunk):
    c = chunk
    n_chunks = seq // c
    head = pl.program_id(1)
    dk = LANES

    lane1 = lax.broadcasted_iota(jnp.int32, (1, LANES), 1)
    a_exp = jnp.exp(jnp.sum(jnp.where(lane1 == head, al_ref[...], 0.0), axis=-1, keepdims=True))
    dtb = jnp.sum(jnp.where(lane1 == head, dtb_ref[...], 0.0), axis=-1, keepdims=True)

    zeros8 = jnp.zeros((8, LANES), F32)
    for src, dst in ((q_ref, xq), (k_ref, xk), (v_ref, xv)):
        dst[0:8, :] = zeros8

        def stage(i, carry, src=src, dst=dst):
            r0 = pl.multiple_of(i * c, c)
            dst[pl.ds(r0 + 8, c), :] = src[0, pl.ds(r0, c), :].astype(F32)
            return carry

        lax.fori_loop(0, n_chunks, stage, 0)

    row = lax.broadcasted_iota(jnp.int32, (c, c), 0)
    col = lax.broadcasted_iota(jnp.int32, (c, c), 1)
    causal = row >= col
    strict = row > col
    xr = row ^ col
    tril = jnp.where(causal, 1.0, 0.0).astype(BF16)
    lane_c = lax.broadcasted_iota(jnp.int32, (c, LANES), 1)

    def conv(xs, cw_ref, r0):
        acc = cw_ref[0, GDN_CONV - 1:GDN_CONV, :] * xs[pl.ds(r0 + 8, c), :]
        for back in range(1, GDN_CONV):
            tap = GDN_CONV - 1 - back
            acc = acc + cw_ref[0, tap:tap + 1, :] * xs[pl.ds(r0 + 8 - back, c), :]
        return _silu(acc)

    def l2n(t):
        return t * lax.rsqrt(jnp.sum(t * t, axis=-1, keepdims=True) + EPS)

    def log_decay(r0):
        bat = ba_ref[pl.ds(r0, c), :]
        bcol = jnp.sum(jnp.where(lane_c == head, bat, 0.0), axis=-1, keepdims=True)
        acol = jnp.sum(jnp.where(lane_c == head + GDN_HEADS, bat, 0.0), axis=-1, keepdims=True)
        xsp = acol + dtb
        softplus = jnp.maximum(xsp, 0.0) + jnp.log1p(jnp.exp(-jnp.abs(xsp)))
        return jax.nn.sigmoid(bcol), jnp.broadcast_to(-a_exp * softplus, (c, LANES))

    def intra(step, carry):
        par = range(GDN_LOCKSTEP)
        ns = [step * GDN_LOCKSTEP + j for j in par]
        r0s = [pl.multiple_of(n * c, c) for n in ns]
        qn = [l2n(conv(xq, cwq_ref, r0)) * (dk ** -0.5) for r0 in r0s]
        kn = [l2n(conv(xk, cwk_ref, r0)) for r0 in r0s]
        vc = [conv(xv, cwv_ref, r0) for r0 in r0s]
        bg = [log_decay(r0) for r0 in r0s]
        beta = [t[0] for t in bg]
        g_hi = [t[1].astype(BF16) for t in bg]
        g_lo = [(t[1] - h.astype(F32)).astype(BF16) for t, h in zip(bg, g_hi)]
        gc = [_dot(tril, h) + _dot(tril, lo_) for h, lo_ in zip(g_hi, g_lo)]
        decay = []
        for gci in gc:
            gc_row = jnp.transpose(gci)[0:1, :]
            diff = jnp.concatenate([gci] * (c // LANES), axis=1) - gc_row
            decay.append(jnp.where(causal, jnp.exp(jnp.where(causal, diff, 0.0)), 0.0))
        kb = [k * b for k, b in zip(kn, beta)]
        knb = [k.astype(BF16) for k in kn]
        lm = [jnp.where(strict, _dot_nt(kbi.astype(BF16), ki) * dc, 0.0) for kbi, ki, dc in zip(kb, knb, decay)]
        attn = [jnp.where(causal, _dot_nt(q.astype(BF16), ki) * dc, 0.0) for q, ki, dc in zip(qn, knb, decay)]
        tinv = _unit_lower_inverse(lm, xr)
        egc = [jnp.exp(g) for g in gc]
        sol = [_dot(ti.astype(BF16), jnp.concatenate([v * b, kbi * e], axis=1).astype(BF16))
               for ti, v, b, kbi, e in zip(tinv, vc, beta, kb, egc)]
        for j in par:
            rows = pl.ds(r0s[j], c)
            u_s[rows, :] = sol[j][:, :LANES]
            w_s[rows, :] = sol[j][:, LANES:].astype(BF16)
            a_s[rows, :] = attn[j].astype(BF16)
            qd_s[rows, :] = (qn[j] * egc[j]).astype(BF16)
            g_last = gc[j][c - 1:c, :]
            kd_s[rows, :] = (kn[j] * jnp.exp(g_last - gc[j])).astype(BF16)
            cd_s[pl.ds(pl.multiple_of(ns[j] * 8, 8), 8), :] = jnp.broadcast_to(jnp.exp(g_last), (8, LANES))
        return carry

    lax.fori_loop(0, n_chunks // GDN_LOCKSTEP, intra, 0)

    def inter(n, state):
        r0 = pl.multiple_of(n * c, c)
        rows = pl.ds(r0, c)
        sb = state.astype(BF16)
        v_new = u_s[rows, :] - _dot(w_s[rows, :], sb)
        vb = v_new.astype(BF16)
        o = _dot(qd_s[rows, :], sb) + _dot(a_s[rows, :], vb)
        cd = cd_s[pl.ds(pl.multiple_of(n * 8, 8), 1), :]
        new_state = state * cd + _dot_tn(kd_s[rows, :], vb)
        on = o * lax.rsqrt(jnp.mean(o * o, axis=-1, keepdims=True) + EPS) * nw_ref[...]
        z = z_ref[0, rows, :].astype(F32)
        o_ref[0, rows, :] = (on * _silu(z)).astype(BF16)
        return new_state

    lax.fori_loop(0, n_chunks, inter, jnp.zeros((dk, LANES), F32), unroll=2)


def _gdn(proj, ba, conv_w, a_log, dt_bias, gdn_norm, batch, seq):
    c = GDN_CHUNK
    per = seq
    cw = conv_w.reshape(GDN_CONV, 3 * GDN_HEADS, LANES).transpose(1, 0, 2)
    pad = lambda v: jnp.pad(v, (0, LANES - v.shape[0])).reshape(1, LANES)

    def slab(base):
        return pl.BlockSpec((1, per, LANES), lambda b, h, base=base: (base + h, b, 0))

    def cwspec(base):
        return pl.BlockSpec((1, GDN_CONV, LANES), lambda b, h, base=base: (base + h, 0, 0))

    vec = pl.BlockSpec((1, LANES), lambda b, h: (0, 0))
    return pl.pallas_call(
        functools.partial(_gdn_kernel, seq=seq, chunk=c),
        grid=(batch, GDN_HEADS),
        in_specs=[slab(SLAB_QA), slab(SLAB_KA), slab(SLAB_VA), slab(SLAB_ZA),
                  pl.BlockSpec((per, LANES), lambda b, h: (b, 0)),
                  cwspec(0), cwspec(GDN_HEADS), cwspec(2 * GDN_HEADS), vec, vec, vec],
        out_specs=pl.BlockSpec((1, per, LANES), lambda b, h: (h, b, 0)),
        out_shape=jax.ShapeDtypeStruct((GDN_HEADS, batch * seq, LANES), BF16),
        scratch_shapes=[
            pltpu.VMEM((seq + 8, LANES), F32), pltpu.VMEM((seq + 8, LANES), F32),
            pltpu.VMEM((seq + 8, LANES), F32),
            pltpu.VMEM((seq, LANES), F32),
            pltpu.VMEM((seq, LANES), BF16),
            pltpu.VMEM((seq, c), BF16),
            pltpu.VMEM((seq, LANES), BF16),
            pltpu.VMEM((seq, LANES), BF16),
            pltpu.VMEM((seq // c * 8, LANES), F32),
        ],
        compiler_params=_cparams(("arbitrary", "arbitrary")),
        name="gdn_mixer",
    )(proj, proj, proj, proj, ba, cw, cw, cw, pad(a_log), pad(dt_bias), gdn_norm.reshape(1, LANES))


def _diff_kernel(q_ref, k_ref, v_ref, qw_ref, kw_ref, sw_ref, lam_ref, o_ref,
                 q1_s, q2_s, kn_s, ve_s, sc1, sc2, acc1, acc2, m1, m2, *, seq, tq, tk, lam_init):
    dh = DIFF_QK_DIM
    n_blk = seq // tq
    per_q = tq // tk
    lane = lax.broadcasted_iota(jnp.int32, (1, LANES), 1)
    lo = lane < dh

    lam = (jnp.exp(jnp.sum(lam_ref[0:1, :] * lam_ref[1:2, :], axis=-1, keepdims=True))
           - jnp.exp(jnp.sum(lam_ref[2:3, :] * lam_ref[3:4, :], axis=-1, keepdims=True)) + lam_init)

    def halfnorm(t, w):
        sq = t * t
        s_lo = jnp.sum(jnp.where(lo, sq, 0.0), axis=-1, keepdims=True)
        s_hi = jnp.sum(jnp.where(lo, 0.0, sq), axis=-1, keepdims=True)
        r = jnp.where(lo, lax.rsqrt(s_lo / dh + EPS), lax.rsqrt(s_hi / dh + EPS))
        return t * r * w

    ones_blk = jnp.ones((tk, LANES), BF16)

    def prep(i, carry):
        rows = pl.ds(pl.multiple_of(i * tk, tk), tk)
        kn_s[rows, :] = halfnorm(k_ref[0, rows, :].astype(F32), kw_ref[...]).astype(BF16)
        qn = halfnorm(q_ref[0, rows, :].astype(F32), qw_ref[...]) * (dh ** -0.5)
        q1_s[rows, :] = jnp.where(lo, qn, 0.0).astype(BF16)
        q2_s[rows, :] = jnp.where(lo, 0.0, qn).astype(BF16)
        ve_s[rows, 0:LANES] = v_ref[0, rows, :]
        ve_s[rows, LANES:2 * LANES] = ones_blk
        return carry

    lax.fori_loop(0, seq // tk, prep, 0)

    row = lax.broadcasted_iota(jnp.int32, (tq, tk), 0)
    col = lax.broadcasted_iota(jnp.int32, (tq, tk), 1)

    n_parts = tk // LANES

    def q_block(qi, carry):
        qrows = pl.ds(pl.multiple_of(qi * tq, tq), tq)
        maps = ((q1_s[qrows, :], sc1, m1, acc1), (q2_s[qrows, :], sc2, m2, acc2))
        for _, _, m_r, a_r in maps:
            m_r[...] = jnp.full((tq, LANES), -jnp.inf, F32)
            a_r[...] = jnp.zeros((tq, 2 * LANES), F32)

        def score_step(kj, masked):
            kb = kn_s[pl.ds(pl.multiple_of(kj * tk, tk), tk), :]
            for qq, s_c, m_r, _ in maps:
                s = _dot_nt(qq, kb)
                if masked:
                    s = jnp.where(row + qi * tq >= col + kj * tk, s, -jnp.inf)
                s_c[kj] = s
                smax = m_r[...]
                for part in range(n_parts):
                    smax = jnp.maximum(smax, s[:, part * LANES:(part + 1) * LANES])
                m_r[...] = smax

        def off_diag(kq, c2):
            for part in range(per_q):
                score_step(kq * per_q + part, False)
            return c2

        lax.fori_loop(0, qi, off_diag, 0)
        for dblk in range(per_q):
            score_step(qi * per_q + dblk, True)
        for _, _, m_r, _ in maps:
            m_r[...] = jnp.broadcast_to(jnp.max(m_r[...], axis=-1, keepdims=True), (tq, LANES))

        def value_step(kq, c2):
            for _, s_c, m_r, a_r in maps:
                mrep = m_r[...]
                upd = a_r[...]
                for sub in range(per_q):
                    kj = kq * per_q + sub
                    s = s_c[kj]
                    p = jnp.concatenate(
                        [jnp.exp(s[:, part * LANES:(part + 1) * LANES] - mrep) for part in range(n_parts)], axis=1)
                    upd = upd + _dot(p.astype(BF16), ve_s[pl.ds(pl.multiple_of(kj * tk, tk), tk), :])
                a_r[...] = upd
            return c2

        lax.fori_loop(0, qi + 1, value_step, 0)
        o = (acc1[:, 0:LANES] / acc1[:, LANES:2 * LANES]
             - lam * (acc2[:, 0:LANES] / acc2[:, LANES:2 * LANES]))
        o = o * lax.rsqrt(jnp.mean(o * o, axis=-1, keepdims=True) + EPS) * sw_ref[...] * (1.0 - lam_init)
        o_ref[0, qrows, :] = o.astype(BF16)
        return carry

    lax.fori_loop(0, n_blk, q_block, 0)


def _diff(proj, q_norm, k_norm, subln, lam_vecs, lam_init, batch, seq):
    tq = 512
    tk = 256
    per = seq

    def slab(base):
        return pl.BlockSpec((1, per, LANES), lambda b, h, base=base: (base + h, b, 0))

    vec = pl.BlockSpec((1, LANES), lambda b, h: (0, 0))
    tile2 = lambda w: jnp.concatenate([w, w]).reshape(1, LANES)
    return pl.pallas_call(
        functools.partial(_diff_kernel, seq=seq, tq=tq, tk=tk, lam_init=lam_init),
        grid=(batch, DIFF_HEADS),
        in_specs=[slab(SLAB_QB), slab(SLAB_KB), slab(SLAB_VB), vec, vec, vec,
                  pl.BlockSpec((4, DIFF_QK_DIM), lambda b, h: (0, 0))],
        out_specs=pl.BlockSpec((1, per, LANES), lambda b, h: (h, b, 0)),
        out_shape=jax.ShapeDtypeStruct((DIFF_HEADS, batch * seq, LANES), BF16),
        scratch_shapes=[
            pltpu.VMEM((seq, LANES), BF16), pltpu.VMEM((seq, LANES), BF16), pltpu.VMEM((seq, LANES), BF16),
            pltpu.VMEM((seq, 2 * LANES), BF16),
            pltpu.VMEM((seq // tk, tq, tk), F32), pltpu.VMEM((seq // tk, tq, tk), F32),
            pltpu.VMEM((tq, 2 * LANES), F32), pltpu.VMEM((tq, 2 * LANES), F32),
            pltpu.VMEM((tq, LANES), F32), pltpu.VMEM((tq, LANES), F32),
        ],
        compiler_params=_cparams(("arbitrary", "arbitrary")),
        name="diff_mixer",
    )(proj, proj, proj, tile2(q_norm), tile2(k_norm), subln.reshape(1, LANES), lam_vecs)


def _dil_kernel(q_ref, k_ref, v_ref, qw_ref, kw_ref, o_ref, qn_s, kn_s, vf_s, sc_s, mb_s, acc_s, l_s, m_s, *, seq):
    blk = DIL_BLOCK
    dh = LANES
    pc = 256
    n_pc = seq // pc
    unroll = 4

    def rms(t, w):
        return t * lax.rsqrt(jnp.mean(t * t, axis=-1, keepdims=True) + EPS) * w

    ones_kv = jnp.ones((2 * blk, LANES), BF16)

    def prep(i, carry):
        rows = pl.ds(pl.multiple_of(i * pc, pc), pc)
        qn_s[rows, :] = rms(q_ref[0, rows, :].astype(F32), qw_ref[...]) * (dh ** -0.5)
        kn_s[rows, :] = rms(k_ref[0, rows, :].astype(F32), kw_ref[...])
        vf_s[rows, :] = v_ref[0, rows, :].astype(F32)
        return carry

    lax.fori_loop(0, n_pc, prep, 0)

    qi = lax.broadcasted_iota(jnp.int32, (blk, 2 * blk), 0)
    kj = lax.broadcasted_iota(jnp.int32, (blk, 2 * blk), 1)

    for set_idx, (window, dil) in enumerate(DILATED_PAIRS):
        steps = window // dil
        n_sub = seq // dil
        nb = n_sub // blk
        assert steps == blk and nb >= 2 and (dil * nb) % unroll == 0

        def slices(idx, dil=dil, nb=nb):
            r = idx // nb
            n = idx - r * nb
            kbase = jnp.maximum(n - 1, 0) * blk
            if dil > 1:
                return n, kbase, pl.ds(r + dil * blk * n, blk, stride=dil), pl.ds(r + dil * kbase, 2 * blk, stride=dil)
            return n, kbase, pl.ds(pl.multiple_of(blk * n, blk), blk), pl.ds(pl.multiple_of(kbase, blk), 2 * blk)

        def score_block(idx, carry):
            n, kbase, qsl, ksl = slices(idx)
            s = _dot_nt(qn_s[qsl, :].astype(BF16), kn_s[ksl, :].astype(BF16))
            dist = (n * blk + qi) - (kbase + kj)
            s = jnp.where((dist >= 0) & (dist <= steps), s, -jnp.inf)
            sc_s[idx] = s
            m = jnp.max(jnp.maximum(s[:, 0:blk], s[:, blk:2 * blk]), axis=-1, keepdims=True)
            mb_s[idx] = jnp.broadcast_to(m, (blk, LANES))
            return carry

        lax.fori_loop(0, dil * nb, score_block, 0, unroll=unroll)

        def value_block(idx, carry, first=(set_idx == 0)):
            _, _, qsl, ksl = slices(idx)
            s = sc_s[idx]
            m = mb_s[idx]
            p = jnp.concatenate([jnp.exp(s[:, 0:blk] - m), jnp.exp(s[:, blk:2 * blk] - m)], axis=1)
            ve = jnp.concatenate([vf_s[ksl, :].astype(BF16), ones_kv], axis=1)
            ol = _dot(p.astype(BF16), ve)
            o = ol[:, 0:LANES]
            l = ol[:, LANES:2 * LANES]
            if first:
                acc_s[qsl, :] = o
                l_s[qsl, :] = l
                m_s[qsl, :] = m
            else:
                m_old = m_s[qsl, :]
                m_new = jnp.maximum(m_old, m)
                a_old = jnp.exp(m_old - m_new)
                a_cur = jnp.exp(m - m_new)
                acc_s[qsl, :] = a_old * acc_s[qsl, :] + a_cur * o
                l_s[qsl, :] = a_old * l_s[qsl, :] + a_cur * l
                m_s[qsl, :] = m_new
            return carry

        lax.fori_loop(0, dil * nb, value_block, 0, unroll=unroll)

    def fin(i, carry):
        rows = pl.ds(pl.multiple_of(i * pc, pc), pc)
        o_ref[0, rows, :] = (acc_s[rows, :] / l_s[rows, :]).astype(BF16)
        return carry

    lax.fori_loop(0, n_pc, fin, 0)


def _dil(proj, q_norm, k_norm, batch, seq):
    per = seq

    def slab(base):
        return pl.BlockSpec((1, per, LANES), lambda b, h, base=base: (base + h, b, 0))

    vec = pl.BlockSpec((1, LANES), lambda b, h: (0, 0))
    return pl.pallas_call(
        functools.partial(_dil_kernel, seq=seq),
        grid=(batch, DIL_HEADS),
        in_specs=[slab(SLAB_QC), slab(SLAB_KC), slab(SLAB_VC), vec, vec],
        out_specs=pl.BlockSpec((1, per, LANES), lambda b, h: (h, b, 0)),
        out_shape=jax.ShapeDtypeStruct((DIL_HEADS, batch * seq, LANES), BF16),
        scratch_shapes=[
            pltpu.VMEM((seq, LANES), F32), pltpu.VMEM((seq, LANES), F32),
            pltpu.VMEM((seq, LANES), F32),
            pltpu.VMEM((seq // DIL_BLOCK, DIL_BLOCK, 2 * DIL_BLOCK), F32),
            pltpu.VMEM((seq // DIL_BLOCK, DIL_BLOCK, LANES), F32),
            pltpu.VMEM((seq, LANES), F32), pltpu.VMEM((seq, LANES), F32),
            pltpu.VMEM((seq, LANES), F32),
        ],
        compiler_params=_cparams(("arbitrary", "arbitrary")),
        name="dil_mixer",
    )(proj, proj, proj, q_norm.reshape(1, LANES), k_norm.reshape(1, LANES))


def _route_tile(sc, bias, run):
    ne, tm = sc.shape
    epg = EXPERTS_PER_GROUP
    sel = sc + bias
    jrow = lax.broadcasted_iota(jnp.int32, (epg, tm), 0)
    gs, i1s, i2s = [], [], []
    for g in range(N_GROUPS):
        sg = sel[g * epg:(g + 1) * epg, :]
        m1 = jnp.max(sg, axis=0, keepdims=True)
        i1 = jnp.min(jnp.where(sg == m1, jrow, epg), axis=0, keepdims=True)
        rest = jnp.where(jrow == i1, -jnp.inf, sg)
        m2 = jnp.max(rest, axis=0, keepdims=True)
        i2 = jnp.min(jnp.where(rest == m2, jrow, epg), axis=0, keepdims=True)
        gs.append(m1 + m2)
        i1s.append(i1)
        i2s.append(i2)
    gmax = functools.reduce(jnp.maximum, gs)
    gidx = jnp.full((1, tm), N_GROUPS - 1, jnp.int32)
    for g in reversed(range(N_GROUPS - 1)):
        gidx = jnp.where(gs[g] == gmax, g, gidx)
    loc1 = jnp.zeros((1, tm), jnp.int32)
    loc2 = jnp.zeros((1, tm), jnp.int32)
    for g in range(N_GROUPS):
        loc1 = jnp.where(gidx == g, i1s[g], loc1)
        loc2 = jnp.where(gidx == g, i2s[g], loc2)
    e1 = gidx * epg + loc1
    e2 = gidx * epg + loc2
    erow = lax.broadcasted_iota(jnp.int32, (ne, tm), 0)
    oh1 = erow == e1
    oh2 = erow == e2
    s1 = jnp.sum(jnp.where(oh1, sc, 0.0), axis=0, keepdims=True)
    s2 = jnp.sum(jnp.where(oh2, sc, 0.0), axis=0, keepdims=True)
    den = s1 + s2
    tr = lax.broadcasted_iota(jnp.int32, (tm, tm), 0)
    tc = lax.broadcasted_iota(jnp.int32, (tm, tm), 1)
    before = jnp.where(tr < tc, 1.0, 0.0).astype(BF16)
    f1 = jnp.where(oh1, 1.0, 0.0)
    f2 = jnp.where(oh2, 1.0, 0.0)
    pre1 = _dot(f1.astype(BF16), before)
    pre2 = _dot(f2.astype(BF16), before)
    tot1 = jnp.sum(f1, axis=1, keepdims=True)
    tot2 = jnp.sum(f2, axis=1, keepdims=True)
    r1 = jnp.sum(jnp.where(oh1, pre1 + run, 0.0), axis=0, keepdims=True)
    r2 = jnp.sum(jnp.where(oh2, pre2 + (run + tot1), 0.0), axis=0, keepdims=True)
    return e1, e2, s1 / den, s2 / den, r1, r2, run + tot1 + tot2


def _outproj_kernel(oa_ref, ob_ref, oc_ref, w_ref, x_ref, g_ref, nw_ref, sc_ref, sh_ref, wrt_ref, rb_ref,
                    xo_ref, hf_ref, ri_ref, rw_ref, cnt_ref, mix_s, run_s):
    @pl.when(pl.program_id(0) == 0)
    def _():
        run_s[...] = jnp.zeros(run_s.shape, F32)

    off = 0
    for ref, heads in ((oa_ref, GDN_HEADS), (ob_ref, DIFF_HEADS), (oc_ref, DIL_HEADS)):
        for k in range(heads):
            mix_s[:, off:off + LANES] = ref[k]
            off += LANES
    xn = x_ref[...] + g_ref[0] * _dot(mix_s[...], w_ref[...])
    xo_ref[...] = xn
    hb = _modulated_norm(xn, nw_ref[...], sc_ref[0], sh_ref[0]).astype(BF16)
    hf_ref[...] = _pack_bf16_pairs(hb)
    scores_t = jax.nn.sigmoid(_dot_nt(wrt_ref[...], hb))
    e1, e2, w1, w2, r1, r2, run = _route_tile(scores_t, rb_ref[...], run_s[...])
    run_s[...] = run
    cnt_ref[...] = jnp.broadcast_to(run, cnt_ref.shape)
    tm = e1.shape[1]
    ri_ref[...] = jnp.concatenate([e1, e2, r1.astype(jnp.int32), r2.astype(jnp.int32),
                                   jnp.zeros((4, tm), jnp.int32)], axis=0)
    rw_ref[...] = jnp.transpose(jnp.concatenate([w1, w2, jnp.zeros((6, tm), F32)], axis=0))


def _outproj(o_a, o_b, o_c, w_out, x2, g_a, nw, sc, sh, w_router_t, router_bias, seq):
    t, d = x2.shape
    tm = 256
    per_b = seq // tm
    ne = w_router_t.shape[0]
    row = pl.BlockSpec((tm, d), lambda i: (i, 0))
    bvec = pl.BlockSpec((1, 1, d), lambda i: (i // per_b, 0, 0))
    return pl.pallas_call(
        _outproj_kernel,
        grid=(t // tm,),
        in_specs=[
            pl.BlockSpec((GDN_HEADS, tm, LANES), lambda i: (0, i, 0)),
            pl.BlockSpec((DIFF_HEADS, tm, LANES), lambda i: (0, i, 0)),
            pl.BlockSpec((DIL_HEADS, tm, LANES), lambda i: (0, i, 0)),
            pl.BlockSpec((d, d), lambda i: (0, 0)),
            row, bvec,
            pl.BlockSpec((1, d), lambda i: (0, 0)),
            bvec, bvec,
            pl.BlockSpec((ne, d), lambda i: (0, 0)),
            pl.BlockSpec((ne, 1), lambda i: (0, 0)),
        ],
        out_specs=[row,
                   pl.BlockSpec((tm, d // 2), lambda i: (i, 0)),
                   pl.BlockSpec((8, tm), lambda i: (0, i)),
                   pl.BlockSpec((tm, 8), lambda i: (i, 0)),
                   pl.BlockSpec((ne, LANES), lambda i: (0, 0))],
        out_shape=[jax.ShapeDtypeStruct((t, d), F32), jax.ShapeDtypeStruct((t, d // 2), jnp.int32),
                   jax.ShapeDtypeStruct((8, t), jnp.int32), jax.ShapeDtypeStruct((t, 8), F32),
                   jax.ShapeDtypeStruct((ne, LANES), F32)],
        scratch_shapes=[pltpu.VMEM((tm, d), BF16), pltpu.VMEM((ne, 1), F32)],
        compiler_params=_cparams(("arbitrary",)),
        name="outproj",
    )(o_a, o_b, o_c, w_out, x2, g_a, nw, sc, sh, w_router_t, router_bias.reshape(ne, 1).astype(F32))


def _moe_kernel(be_ref, nv_ref, tok_ref, tok_next_ref, dst_prev_ref, dst_ref, hf_hbm, wg_ref, wu_ref, wd_ref,
                y_hbm, xbuf, ybuf, wg_s, wu_s, wd_s, gsem, ssem):
    i = pl.program_id(0)
    nv = nv_ref[0]
    slot = i % 2
    rows = MOE_ROWS

    def gather_copy(tok_r, r, s):
        return pltpu.make_async_copy(hf_hbm.at[pl.ds(tok_r[0, 0, r], 1)], xbuf.at[s, pl.ds(r, 1)], gsem.at[s])

    def scatter_copy(dst_r, r, s):
        return pltpu.make_async_copy(ybuf.at[s, pl.ds(r, 1)], y_hbm.at[pl.ds(dst_r[0, 0, r], 1)], ssem.at[s])

    @pl.when(i == 0)
    def _():
        ybuf[...] = jnp.zeros(ybuf.shape, jnp.int32)
        for r in range(rows):
            gather_copy(tok_ref, r, 0).start(priority=r % 2)

    @pl.when((i == 0) | (be_ref[i] != be_ref[jnp.maximum(i - 1, 0)]))
    def _():
        wg_s[...] = wg_ref[0].astype(BF16)
        wu_s[...] = wu_ref[0].astype(BF16)
        wd_s[...] = wd_ref[0].astype(BF16)

    @pl.when((i >= 1) & (i < nv))
    def _():
        for r in range(rows):
            scatter_copy(dst_ref, r, slot).wait()

    @pl.when(i < nv)
    def _():
        for r in range(rows):
            gather_copy(tok_ref, r, slot).wait()
        x = _unpack_bf16_pairs(xbuf[slot]).astype(BF16)
        for r in range(rows):
            gather_copy(tok_next_ref, r, 1 - slot).start(priority=r % 2)
            scatter_copy(dst_prev_ref, r, 1 - slot).start(priority=r % 2)
        a = _dot(x, wg_s[...])
        u = _dot(x, wu_s[...])
        ybuf[slot] = _pack_bf16_pairs(_dot((_silu(a) * u).astype(BF16), wd_s[...]).astype(BF16))

    @pl.when(i == nv - 1)
    def _():
        for r in range(rows):
            gather_copy(tok_next_ref, r, 1 - slot).wait()
            scatter_copy(dst_prev_ref, r, 1 - slot).wait()
        for r in range(rows):
            scatter_copy(dst_ref, r, slot).start(priority=r % 2)
        for r in range(rows):
            scatter_copy(dst_ref, r, slot).wait()


def _moe(hf, block_e, n_valid, row_tok, row_dst, w_gate, w_up, w_down, layer):
    t = hf.shape[0]
    d = w_gate.shape[2]
    f = w_gate.shape[3]
    rows = MOE_ROWS
    nb = block_e.shape[0]
    tok3 = row_tok.reshape(nb, 1, rows)
    spare = (TOP_K * t + jnp.arange(rows, dtype=jnp.int32)).reshape(1, 1, rows)
    dst3 = jnp.concatenate([spare, row_dst.reshape(nb, 1, rows)], axis=0)
    smem_blk = lambda fn: pl.BlockSpec((1, 1, rows), fn, memory_space=pltpu.SMEM)
    grid_spec = pltpu.PrefetchScalarGridSpec(
        num_scalar_prefetch=2,
        grid=(nb,),
        in_specs=[
            smem_blk(lambda i, be, nv: (i, 0, 0)),
            smem_blk(lambda i, be, nv: (jnp.minimum(i + 1, nb - 1), 0, 0)),
            smem_blk(lambda i, be, nv: (i, 0, 0)),
            smem_blk(lambda i, be, nv: (i + 1, 0, 0)),
            pl.BlockSpec(memory_space=pl.ANY),
            pl.BlockSpec((None, 1, d, f), lambda i, be, nv: (layer, be[i], 0, 0)),
            pl.BlockSpec((None, 1, d, f), lambda i, be, nv: (layer, be[i], 0, 0)),
            pl.BlockSpec((None, 1, f, d), lambda i, be, nv: (layer, be[i], 0, 0)),
        ],
        out_specs=pl.BlockSpec(memory_space=pl.ANY),
        scratch_shapes=[
            pltpu.VMEM((2, rows, d // 2), jnp.int32),
            pltpu.VMEM((2, rows, d // 2), jnp.int32),
            pltpu.VMEM((d, f), BF16), pltpu.VMEM((d, f), BF16), pltpu.VMEM((f, d), BF16),
            pltpu.SemaphoreType.DMA((2,)),
            pltpu.SemaphoreType.DMA((2,)),
        ],
    )
    return pl.pallas_call(
        _moe_kernel,
        grid_spec=grid_spec,
        out_shape=jax.ShapeDtypeStruct((TOP_K * t + rows, d // 2), jnp.int32),
        compiler_params=_cparams(("arbitrary",), has_side_effects=True, disable_bounds_checks=True),
        name="moe_experts",
    )(block_e, n_valid, tok3, tok3, dst3, dst3, hf, w_gate, w_up, w_down)


def _block_layout(route_i, counts_f):
    t = route_i.shape[1]
    rows = MOE_ROWS
    n_assign = TOP_K * t
    counts = counts_f[:, 0].astype(jnp.int32)
    blocks_per_e = (counts + rows - 1) // rows
    blk_end = jnp.cumsum(blocks_per_e)
    blk_start = blk_end - blocks_per_e
    flat_e = route_i[0:TOP_K].reshape(n_assign)
    dest = blk_start[flat_e] * rows + route_i[TOP_K:2 * TOP_K].reshape(n_assign)
    nb = n_assign // rows + N_EXPERTS
    n_rows = nb * rows
    assign = jnp.arange(n_assign, dtype=jnp.int32)
    spare = n_assign + jnp.arange(n_rows, dtype=jnp.int32) % rows
    row_dst = spare.at[dest].set(assign, unique_indices=True, mode='promise_in_bounds')
    row_tok = jnp.where(row_dst < n_assign, row_dst % t, 0)
    n_valid = blk_end[-1].astype(jnp.int32)
    bidx = jnp.arange(nb, dtype=jnp.int32)
    block_e = jnp.sum((blk_end[None, :] <= bidx[:, None]).astype(jnp.int32), axis=1)
    block_e = jnp.minimum(block_e, N_EXPERTS - 1)
    last_e = block_e[jnp.maximum(n_valid - 1, 0)]
    block_e = jnp.where(bidx < n_valid, block_e, last_e)
    return block_e, n_valid.reshape(1), row_tok, row_dst


def _combine_kernel(x_ref, y0_ref, y1_ref, rw_ref, g_ref, o_ref):
    w = rw_ref[...]
    y0 = _unpack_bf16_pairs(y0_ref[...])
    y1 = _unpack_bf16_pairs(y1_ref[...])
    o_ref[...] = x_ref[...] + g_ref[0] * (y0 * w[:, 0:1] + y1 * w[:, 1:2])


def _combine(x2, y2, route_w, g_f, seq):
    t, d = x2.shape
    tm = 256
    per_b = seq // tm
    n_blk = t // tm
    return pl.pallas_call(
        _combine_kernel,
        grid=(n_blk,),
        in_specs=[pl.BlockSpec((tm, d), lambda i: (i, 0)),
                  pl.BlockSpec((tm, d // 2), lambda i: (i, 0)),
                  pl.BlockSpec((tm, d // 2), lambda i: (n_blk + i, 0)),
                  pl.BlockSpec((tm, 8), lambda i: (i, 0)),
                  pl.BlockSpec((1, 1, d), lambda i: (i // per_b, 0, 0))],
        out_specs=pl.BlockSpec((tm, d), lambda i: (i, 0)),
        out_shape=jax.ShapeDtypeStruct((t, d), F32),
        compiler_params=_cparams(("arbitrary",)),
        name="moe_combine",
    )(x2, y2, y2, route_w, g_f)


def kernel(x, c, norm_mix, norm_ffn, w_ada, b_ada, w_in, conv_w, a_log, dt_bias, gdn_norm, diff_q_norm, diff_k_norm, lam_q1, lam_k1, lam_q2, lam_k2, diff_subln, dil_q_norm, dil_k_norm, w_out, w_router, router_bias, w_gate, w_up, w_down):
    batch, seq, d = x.shape
    depth = w_ada.shape[0]
    t = batch * seq
    x2 = x.reshape(t, d)
    mod = _ada(c, w_ada, b_ada).reshape(depth, batch, N_MOD, 1, d)
    ba_lo = 3 * GDN_HEADS * LANES + GDN_HEADS * LANES
    ba_hi = ba_lo + 2 * GDN_HEADS
    w_router_t = w_router.T.astype(BF16)
    for l in range(depth):
        sh_a, sc_a, g_a, sh_f, sc_f, g_f = (mod[l, :, k] for k in range(N_MOD))
        w_main = jnp.concatenate([w_in[l, :, :ba_lo], w_in[l, :, ba_hi:]], axis=1).astype(BF16)
        w_ba = jnp.pad(w_in[l, :, ba_lo:ba_hi], ((0, 0), (0, LANES - 2 * GDN_HEADS))).astype(BF16)
        proj, ba = _inproj(x2, norm_mix[l].reshape(1, d), sc_a, sh_a, w_main, w_ba, seq)
        o_a = _gdn(proj, ba, conv_w[l], a_log[l], dt_bias[l], gdn_norm[l], batch, seq)
        lam_init = 0.8 - 0.6 * math.exp(-0.3 * l)
        lam_vecs = jnp.stack([lam_q1[l], lam_k1[l], lam_q2[l], lam_k2[l]]).astype(F32)
        o_b = _diff(proj, diff_q_norm[l], diff_k_norm[l], diff_subln[l], lam_vecs, lam_init, batch, seq)
        o_c = _dil(proj, dil_q_norm[l], dil_k_norm[l], batch, seq)
        x2, hf, route_i, route_w, counts = _outproj(o_a, o_b, o_c, w_out[l].astype(BF16), x2, g_a,
                                                    norm_ffn[l].reshape(1, d), sc_f, sh_f, w_router_t, router_bias, seq)
        block_e, n_valid, row_tok, row_dst = _block_layout(route_i, counts)
        y2 = _moe(hf, block_e, n_valid, row_tok, row_dst, w_gate, w_up, w_down, l)
        x2 = _combine(x2, y2, route_w, g_f, seq)
    return x2.reshape(batch, seq, d)
```

```python
import functools
import math

import jax
import jax.numpy as jnp
from jax import lax
from jax.experimental import pallas as pl
from jax.experimental.pallas import tpu as pltpu

F32 = jnp.float32
BF16 = jnp.bfloat16

LANES = 128
EPS = 1e-6
N_MOD = 6

GDN_HEADS = 6
GDN_CONV = 4
GDN_CHUNK = 256
GDN_LOCKSTEP = 4
DIFF_HEADS = 4
DIFF_QK_DIM = 64
DIL_HEADS = 6
DILATED_PAIRS = ((128, 1), (512, 4), (2048, 16))
DIL_BLOCK = 128

N_EXPERTS = 64
N_GROUPS = 8
EXPERTS_PER_GROUP = N_EXPERTS // N_GROUPS
TOP_K = 2
MOE_ROWS = 256

SLAB_QA, SLAB_KA, SLAB_VA, SLAB_ZA = 0, 6, 12, 18
SLAB_QB, SLAB_KB, SLAB_VB = 24, 28, 32
SLAB_QC, SLAB_KC, SLAB_VC = 36, 42, 48
N_SLABS = 54

VMEM_LIMIT = 56 * 1024 * 1024


def _cparams(sem, vmem=VMEM_LIMIT, **kw):
    return pltpu.CompilerParams(dimension_semantics=sem, vmem_limit_bytes=vmem, **kw)


def _silu(v):
    return v * jax.nn.sigmoid(v)


def _dot(a, b):
    return jnp.dot(a, b, preferred_element_type=F32)


def _dot_nt(a, b):
    return lax.dot_general(a, b, (((1,), (1,)), ((), ())), preferred_element_type=F32)


def _dot_tn(a, b):
    return lax.dot_general(a, b, (((0,), (0,)), ((), ())), preferred_element_type=F32)


def _pack_bf16_pairs(v):
    half = v.shape[1] // 2
    lo = lax.bitcast_convert_type(v[:, :half].astype(F32), jnp.int32)
    hi = lax.bitcast_convert_type(v[:, half:].astype(F32), jnp.int32)
    return lax.shift_right_logical(lo, 16) | (hi & -65536)


def _unpack_bf16_pairs(w):
    return jnp.concatenate([lax.bitcast_convert_type(lax.shift_left(w, 16), F32),
                            lax.bitcast_convert_type(w & -65536, F32)], axis=1)


def _ada_kernel(c_ref, w_ref, b_ref, o_ref):
    cact = _silu(c_ref[...]).astype(BF16)
    o_ref[0] = _dot(cact, w_ref[0].astype(BF16)) + b_ref[0]


def _ada(c, w_ada, b_ada):
    depth, d, n = w_ada.shape
    b = c.shape[0]
    tn = 1024
    return pl.pallas_call(
        _ada_kernel,
        grid=(depth, n // tn),
        in_specs=[
            pl.BlockSpec((b, d), lambda l, j: (0, 0)),
            pl.BlockSpec((1, d, tn), lambda l, j: (l, 0, j)),
            pl.BlockSpec((1, 1, tn), lambda l, j: (l, 0, j)),
        ],
        out_specs=pl.BlockSpec((1, b, tn), lambda l, j: (l, 0, j)),
        out_shape=jax.ShapeDtypeStruct((depth, b, n), F32),
        compiler_params=_cparams(("arbitrary", "arbitrary")),
        name="ada_mod",
    )(c, w_ada, b_ada.reshape(depth, 1, n))


def _modulated_norm(x, nw, sc, sh):
    ms = jnp.mean(x * x, axis=-1, keepdims=True)
    return (x * lax.rsqrt(ms + EPS) * nw) * (1.0 + sc) + sh


def _inproj_kernel(x_ref, nw_ref, sc_ref, sh_ref, w_ref, wba_ref, o_ref, ba_ref, h_scr, *, n_sub):
    @pl.when(pl.program_id(1) == 0)
    def _():
        h = _modulated_norm(x_ref[...], nw_ref[...], sc_ref[0], sh_ref[0]).astype(BF16)
        h_scr[...] = h
        ba_ref[...] = _dot(h, wba_ref[...])

    acc = _dot(h_scr[...], w_ref[...])
    for k in range(n_sub):
        o_ref[k] = acc[:, k * LANES:(k + 1) * LANES].astype(BF16)


def _inproj(x2, nw, sc, sh, w_main, w_ba, seq):
    t, d = x2.shape
    n = w_main.shape[1]
    tm = min(1024, seq)
    tn = 768
    n_sub = tn // LANES
    per_b = seq // tm
    return pl.pallas_call(
        functools.partial(_inproj_kernel, n_sub=n_sub),
        grid=(t // tm, n // tn),
        in_specs=[
            pl.BlockSpec((tm, d), lambda i, j: (i, 0)),
            pl.BlockSpec((1, d), lambda i, j: (0, 0)),
            pl.BlockSpec((1, 1, d), lambda i, j: (i // per_b, 0, 0)),
            pl.BlockSpec((1, 1, d), lambda i, j: (i // per_b, 0, 0)),
            pl.BlockSpec((d, tn), lambda i, j: (0, j)),
            pl.BlockSpec((d, LANES), lambda i, j: (0, 0)),
        ],
        out_specs=[
            pl.BlockSpec((n_sub, tm, LANES), lambda i, j: (j, i, 0)),
            pl.BlockSpec((tm, LANES), lambda i, j: (i, 0)),
        ],
        out_shape=[
            jax.ShapeDtypeStruct((n // LANES, t, LANES), BF16),
            jax.ShapeDtypeStruct((t, LANES), F32),
        ],
        scratch_shapes=[pltpu.VMEM((tm, d), BF16)],
        compiler_params=_cparams(("arbitrary", "arbitrary")),
        name="inproj",
    )(x2, nw, sc, sh, w_main, w_ba)


def _unit_lower_inverse(lms, xr):
    c = lms[0].shape[0]

    def mm(a, b):
        return [_dot(ai.astype(BF16), bi.astype(BF16)) for ai, bi in zip(a, b)]

    d1 = [jnp.where(xr < 16, lm, 0.0) for lm in lms]
    d2 = mm(d1, d1)
    d4 = mm(d2, d2)
    d8 = mm(d4, d4)
    p = [b - a - t for a, b, t in zip(d1, d2, mm(d1, d2))]
    p = [pi + di + t for pi, di, t in zip(p, d4, mm(p, d4))]
    p = [pi + di + t for pi, di, t in zip(p, d8, mm(p, d8))]
    eye = jnp.where(xr == 0, 1.0, 0.0)
    x = [eye + pi for pi in p]
    blk = 32
    while blk <= c:
        e = [jnp.where((xr < blk) & (xr >= blk // 2), lm, 0.0) for lm in lms]
        x = [xi - t for xi, t in zip(x, mm(x, mm(e, x)))]
        blk *= 2
    return x


def _gdn_kernel(q_ref, k_ref, v_ref, z_ref, ba_ref, cwq_ref, cwk_ref, cwv_ref, al_ref, dtb_ref, nw_ref,
                o_ref, xq, xk, xv, u_s, w_s, a_s, qd_s, kd_s, cd_s, *, seq, chunk):
    c = chunk
    n_chunks = seq // c
    head = pl.program_id(1)
    dk = LANES

    lane1 = lax.broadcasted_iota(jnp.int32, (1, LANES), 1)
    a_exp = jnp.exp(jnp.sum(jnp.where(lane1 == head, al_ref[...], 0.0), axis=-1, keepdims=True))
    dtb = jnp.sum(jnp.where(lane1 == head, dtb_ref[...], 0.0), axis=-1, keepdims=True)

    zeros8 = jnp.zeros((8, LANES), F32)
    for src, dst in ((q_ref, xq), (k_ref, xk), (v_ref, xv)):
        dst[0:8, :] = zeros8

        def stage(i, carry, src=src, dst=dst):
            r0 = pl.multiple_of(i * c, c)
            dst[pl.ds(r0 + 8, c), :] = src[0, pl.ds(r0, c), :].astype(F32)
            return carry

        lax.fori_loop(0, n_chunks, stage, 0)

    row = lax.broadcasted_iota(jnp.int32, (c, c), 0)
    col = lax.broadcasted_iota(jnp.int32, (c, c), 1)
    causal = row >= col
    strict = row > col
    xr = row ^ col
    tril = jnp.where(causal, 1.0, 0.0).astype(BF16)
    lane_c = lax.broadcasted_iota(jnp.int32, (c, LANES), 1)

    def conv(xs, cw_ref, r0):
        acc = cw_ref[0, GDN_CONV - 1:GDN_CONV, :] * xs[pl.ds(r0 + 8, c), :]
        for back in range(1, GDN_CONV):
            tap = GDN_CONV - 1 - back
            acc = acc + cw_ref[0, tap:tap + 1, :] * xs[pl.ds(r0 + 8 - back, c), :]
        return _silu(acc)

    def l2n(t):
        return t * lax.rsqrt(jnp.sum(t * t, axis=-1, keepdims=True) + EPS)

    def log_decay(r0):
        bat = ba_ref[pl.ds(r0, c), :]
        bcol = jnp.sum(jnp.where(lane_c == head, bat, 0.0), axis=-1, keepdims=True)
        acol = jnp.sum(jnp.where(lane_c == head + GDN_HEADS, bat, 0.0), axis=-1, keepdims=True)
        xsp = acol + dtb
        softplus = jnp.maximum(xsp, 0.0) + jnp.log1p(jnp.exp(-jnp.abs(xsp)))
        return jax.nn.sigmoid(bcol), jnp.broadcast_to(-a_exp * softplus, (c, LANES))

    def intra(step, carry):
        par = range(GDN_LOCKSTEP)
        ns = [step * GDN_LOCKSTEP + j for j in par]
        r0s = [pl.multiple_of(n * c, c) for n in ns]
        qn = [l2n(conv(xq, cwq_ref, r0)) * (dk ** -0.5) for r0 in r0s]
        kn = [l2n(conv(xk, cwk_ref, r0)) for r0 in r0s]
        vc = [conv(xv, cwv_ref, r0) for r0 in r0s]
        bg = [log_decay(r0) for r0 in r0s]
        beta = [t[0] for t in bg]
        g_hi = [t[1].astype(BF16) for t in bg]
        g_lo = [(t[1] - h.astype(F32)).astype(BF16) for t, h in zip(bg, g_hi)]
        gc = [_dot(tril, h) + _dot(tril, lo_) for h, lo_ in zip(g_hi, g_lo)]
        decay = []
        for gci in gc:
            gc_row = jnp.transpose(gci)[0:1, :]
            diff = jnp.concatenate([gci] * (c // LANES), axis=1) - gc_row
            decay.append(jnp.where(causal, jnp.exp(jnp.where(causal, diff, 0.0)), 0.0))
        kb = [k * b for k, b in zip(kn, beta)]
        knb = [k.astype(BF16) for k in kn]
        lm = [jnp.where(strict, _dot_nt(kbi.astype(BF16), ki) * dc, 0.0) for kbi, ki, dc in zip(kb, knb, decay)]
        attn = [jnp.where(causal, _dot_nt(q.astype(BF16), ki) * dc, 0.0) for q, ki, dc in zip(qn, knb, decay)]
        tinv = _unit_lower_inverse(lm, xr)
        egc = [jnp.exp(g) for g in gc]
        sol = [_dot(ti.astype(BF16), jnp.concatenate([v * b, kbi * e], axis=1).astype(BF16))
               for ti, v, b, kbi, e in zip(tinv, vc, beta, kb, egc)]
        for j in par:
            rows = pl.ds(r0s[j], c)
            u_s[rows, :] = sol[j][:, :LANES]
            w_s[rows, :] = sol[j][:, LANES:].astype(BF16)
            a_s[rows, :] = attn[j].astype(BF16)
            qd_s[rows, :] = (qn[j] * egc[j]).astype(BF16)
            g_last = gc[j][c - 1:c, :]
            kd_s[rows, :] = (kn[j] * jnp.exp(g_last - gc[j])).astype(BF16)
            cd_s[pl.ds(pl.multiple_of(ns[j] * 8, 8), 8), :] = jnp.broadcast_to(jnp.exp(g_last), (8, LANES))
        return carry

    lax.fori_loop(0, n_chunks // GDN_LOCKSTEP, intra, 0)

    def inter(n, state):
        r0 = pl.multiple_of(n * c, c)
        rows = pl.ds(r0, c)
        sb = state.astype(BF16)
        v_new = u_s[rows, :] - _dot(w_s[rows, :], sb)
        vb = v_new.astype(BF16)
        o = _dot(qd_s[rows, :], sb) + _dot(a_s[rows, :], vb)
        cd = cd_s[pl.ds(pl.multiple_of(n * 8, 8), 1), :]
        new_state = state * cd + _dot_tn(kd_s[rows, :], vb)
        on = o * lax.rsqrt(jnp.mean(o * o, axis=-1, keepdims=True) + EPS) * nw_ref[...]
        z = z_ref[0, rows, :].astype(F32)
        o_ref[0, rows, :] = (on * _silu(z)).astype(BF16)
        return new_state

    lax.fori_loop(0, n_chunks, inter, jnp.zeros((dk, LANES), F32), unroll=2)


def _gdn(proj, ba, conv_w, a_log, dt_bias, gdn_norm, batch, seq):
    c = GDN_CHUNK
    per = seq
    cw = conv_w.reshape(GDN_CONV, 3 * GDN_HEADS, LANES).transpose(1, 0, 2)
    pad = lambda v: jnp.pad(v, (0, LANES - v.shape[0])).reshape(1, LANES)

    def slab(base):
        return pl.BlockSpec((1, per, LANES), lambda b, h, base=base: (base + h, b, 0))

    def cwspec(base):
        return pl.BlockSpec((1, GDN_CONV, LANES), lambda b, h, base=base: (base + h, 0, 0))

    vec = pl.BlockSpec((1, LANES), lambda b, h: (0, 0))
    return pl.pallas_call(
        functools.partial(_gdn_kernel, seq=seq, chunk=c),
        grid=(batch, GDN_HEADS),
        in_specs=[slab(SLAB_QA), slab(SLAB_KA), slab(SLAB_VA), slab(SLAB_ZA),
                  pl.BlockSpec((per, LANES), lambda b, h: (b, 0)),
                  cwspec(0), cwspec(GDN_HEADS), cwspec(2 * GDN_HEADS), vec, vec, vec],
        out_specs=pl.BlockSpec((1, per, LANES), lambda b, h: (h, b, 0)),
        out_shape=jax.ShapeDtypeStruct((GDN_HEADS, batch * seq, LANES), BF16),
        scratch_shapes=[
            pltpu.VMEM((seq + 8, LANES), F32), pltpu.VMEM((seq + 8, LANES), F32),
            pltpu.VMEM((seq + 8, LANES), F32),
            pltpu.VMEM((seq, LANES), F32),
            pltpu.VMEM((seq, LANES), BF16),
            pltpu.VMEM((seq, c), BF16),
            pltpu.VMEM((seq, LANES), BF16),
            pltpu.VMEM((seq, LANES), BF16),
            pltpu.VMEM((seq // c * 8, LANES), F32),
        ],
        compiler_params=_cparams(("arbitrary", "arbitrary")),
        name="gdn_mixer",
    )(proj, proj, proj, proj, ba, cw, cw, cw, pad(a_log), pad(dt_bias), gdn_norm.reshape(1, LANES))


def _diff_kernel(q_ref, k_ref, v_ref, qw_ref, kw_ref, sw_ref, lam_ref, o_ref,
                 q1_s, q2_s, kn_s, ve_s, sc1, sc2, acc1, acc2, m1, m2, *, seq, tq, tk, lam_init):
    dh = DIFF_QK_DIM
    n_blk = seq // tq
    per_q = tq // tk
    lane = lax.broadcasted_iota(jnp.int32, (1, LANES), 1)
    lo = lane < dh

    lam = (jnp.exp(jnp.sum(lam_ref[0:1, :] * lam_ref[1:2, :], axis=-1, keepdims=True))
           - jnp.exp(jnp.sum(lam_ref[2:3, :] * lam_ref[3:4, :], axis=-1, keepdims=True)) + lam_init)

    def halfnorm(t, w):
        sq = t * t
        s_lo = jnp.sum(jnp.where(lo, sq, 0.0), axis=-1, keepdims=True)
        s_hi = jnp.sum(jnp.where(lo, 0.0, sq), axis=-1, keepdims=True)
        r = jnp.where(lo, lax.rsqrt(s_lo / dh + EPS), lax.rsqrt(s_hi / dh + EPS))
        return t * r * w

    ones_blk = jnp.ones((tk, LANES), BF16)

    def prep(i, carry):
        rows = pl.ds(pl.multiple_of(i * tk, tk), tk)
        kn_s[rows, :] = halfnorm(k_ref[0, rows, :].astype(F32), kw_ref[...]).astype(BF16)
        qn = halfnorm(q_ref[0, rows, :].astype(F32), qw_ref[...]) * (dh ** -0.5)
        q1_s[rows, :] = jnp.where(lo, qn, 0.0).astype(BF16)
        q2_s[rows, :] = jnp.where(lo, 0.0, qn).astype(BF16)
        ve_s[rows, 0:LANES] = v_ref[0, rows, :]
        ve_s[rows, LANES:2 * LANES] = ones_blk
        return carry

    lax.fori_loop(0, seq // tk, prep, 0)

    row = lax.broadcasted_iota(jnp.int32, (tq, tk), 0)
    col = lax.broadcasted_iota(jnp.int32, (tq, tk), 1)

    n_parts = tk // LANES

    def q_block(qi, carry):
        qrows = pl.ds(pl.multiple_of(qi * tq, tq), tq)
        maps = ((q1_s[qrows, :], sc1, m1, acc1), (q2_s[qrows, :], sc2, m2, acc2))
        for _, _, m_r, a_r in maps:
            m_r[...] = jnp.full((tq, LANES), -jnp.inf, F32)
            a_r[...] = jnp.zeros((tq, 2 * LANES), F32)

        def score_step(kj, masked):
            kb = kn_s[pl.ds(pl.multiple_of(kj * tk, tk), tk), :]
            for qq, s_c, m_r, _ in maps:
                s = _dot_nt(qq, kb)
                if masked:
                    s = jnp.where(row + qi * tq >= col + kj * tk, s, -jnp.inf)
                s_c[kj] = s
                smax = m_r[...]
                for part in range(n_parts):
                    smax = jnp.maximum(smax, s[:, part * LANES:(part + 1) * LANES])
                m_r[...] = smax

        def off_diag(kq, c2):
            for part in range(per_q):
                score_step(kq * per_q + part, False)
            return c2

        lax.fori_loop(0, qi, off_diag, 0)
        for dblk in range(per_q):
            score_step(qi * per_q + dblk, True)
        for _, _, m_r, _ in maps:
            m_r[...] = jnp.broadcast_to(jnp.max(m_r[...], axis=-1, keepdims=True), (tq, LANES))

        def value_step(kq, c2):
            for _, s_c, m_r, a_r in maps:
                mrep = m_r[...]
                upd = a_r[...]
                for sub in range(per_q):
                    kj = kq * per_q + sub
                    s = s_c[kj]
                    p = jnp.concatenate(
                        [jnp.exp(s[:, part * LANES:(part + 1) * LANES] - mrep) for part in range(n_parts)], axis=1)
                    upd = upd + _dot(p.astype(BF16), ve_s[pl.ds(pl.multiple_of(kj * tk, tk), tk), :])
                a_r[...] = upd
            return c2

        lax.fori_loop(0, qi + 1, value_step, 0)
        o = (acc1[:, 0:LANES] / acc1[:, LANES:2 * LANES]
             - lam * (acc2[:, 0:LANES] / acc2[:, LANES:2 * LANES]))
        o = o * lax.rsqrt(jnp.mean(o * o, axis=-1, keepdims=True) + EPS) * sw_ref[...] * (1.0 - lam_init)
        o_ref[0, qrows, :] = o.astype(BF16)
        return carry

    lax.fori_loop(0, n_blk, q_block, 0)


def _diff(proj, q_norm, k_norm, subln, lam_vecs, lam_init, batch, seq):
    tq = 512
    tk = 256
    per = seq

    def slab(base):
        return pl.BlockSpec((1, per, LANES), lambda b, h, base=base: (base + h, b, 0))

    vec = pl.BlockSpec((1, LANES), lambda b, h: (0, 0))
    tile2 = lambda w: jnp.concatenate([w, w]).reshape(1, LANES)
    return pl.pallas_call(
        functools.partial(_diff_kernel, seq=seq, tq=tq, tk=tk, lam_init=lam_init),
        grid=(batch, DIFF_HEADS),
        in_specs=[slab(SLAB_QB), slab(SLAB_KB), slab(SLAB_VB), vec, vec, vec,
                  pl.BlockSpec((4, DIFF_QK_DIM), lambda b, h: (0, 0))],
        out_specs=pl.BlockSpec((1, per, LANES), lambda b, h: (h, b, 0)),
        out_shape=jax.ShapeDtypeStruct((DIFF_HEADS, batch * seq, LANES), BF16),
        scratch_shapes=[
            pltpu.VMEM((seq, LANES), BF16), pltpu.VMEM((seq, LANES), BF16), pltpu.VMEM((seq, LANES), BF16),
            pltpu.VMEM((seq, 2 * LANES), BF16),
            pltpu.VMEM((seq // tk, tq, tk), F32), pltpu.VMEM((seq // tk, tq, tk), F32),
            pltpu.VMEM((tq, 2 * LANES), F32), pltpu.VMEM((tq, 2 * LANES), F32),
            pltpu.VMEM((tq, LANES), F32), pltpu.VMEM((tq, LANES), F32),
        ],
        compiler_params=_cparams(("arbitrary", "arbitrary")),
        name="diff_mixer",
    )(proj, proj, proj, tile2(q_norm), tile2(k_norm), subln.reshape(1, LANES), lam_vecs)


def _dil_kernel(q_ref, k_ref, v_ref, qw_ref, kw_ref, o_ref, qn_s, kn_s, vf_s, sc_s, mb_s, acc_s, l_s, m_s, *, seq):
    blk = DIL_BLOCK
    dh = LANES
    pc = 256
    n_pc = seq // pc
    unroll = 4

    def rms(t, w):
        return t * lax.rsqrt(jnp.mean(t * t, axis=-1, keepdims=True) + EPS) * w

    ones_kv = jnp.ones((2 * blk, LANES), BF16)

    def prep(i, carry):
        rows = pl.ds(pl.multiple_of(i * pc, pc), pc)
        qn_s[rows, :] = rms(q_ref[0, rows, :].astype(F32), qw_ref[...]) * (dh ** -0.5)
        kn_s[rows, :] = rms(k_ref[0, rows, :].astype(F32), kw_ref[...])
        vf_s[rows, :] = v_ref[0, rows, :].astype(F32)
        return carry

    lax.fori_loop(0, n_pc, prep, 0)

    qi = lax.broadcasted_iota(jnp.int32, (blk, 2 * blk), 0)
    kj = lax.broadcasted_iota(jnp.int32, (blk, 2 * blk), 1)

    for set_idx, (window, dil) in enumerate(DILATED_PAIRS):
        steps = window // dil
        n_sub = seq // dil
        nb = n_sub // blk
        assert steps == blk and nb >= 2 and (dil * nb) % unroll == 0

        def slices(idx, dil=dil, nb=nb):
            r = idx // nb
            n = idx - r * nb
            kbase = jnp.maximum(n - 1, 0) * blk
            if dil > 1:
                return n, kbase, pl.ds(r + dil * blk * n, blk, stride=dil), pl.ds(r + dil * kbase, 2 * blk, stride=dil)
            return n, kbase, pl.ds(pl.multiple_of(blk * n, blk), blk), pl.ds(pl.multiple_of(kbase, blk), 2 * blk)

        def score_block(idx, carry):
            n, kbase, qsl, ksl = slices(idx)
            s = _dot_nt(qn_s[qsl, :].astype(BF16), kn_s[ksl, :].astype(BF16))
            dist = (n * blk + qi) - (kbase + kj)
            s = jnp.where((dist >= 0) & (dist <= steps), s, -jnp.inf)
            sc_s[idx] = s
            m = jnp.max(jnp.maximum(s[:, 0:blk], s[:, blk:2 * blk]), axis=-1, keepdims=True)
            mb_s[idx] = jnp.broadcast_to(m, (blk, LANES))
            return carry

        lax.fori_loop(0, dil * nb, score_block, 0, unroll=unroll)

        def value_block(idx, carry, first=(set_idx == 0)):
            _, _, qsl, ksl = slices(idx)
            s = sc_s[idx]
            m = mb_s[idx]
            p = jnp.concatenate([jnp.exp(s[:, 0:blk] - m), jnp.exp(s[:, blk:2 * blk] - m)], axis=1)
            ve = jnp.concatenate([vf_s[ksl, :].astype(BF16), ones_kv], axis=1)
            ol = _dot(p.astype(BF16), ve)
            o = ol[:, 0:LANES]
            l = ol[:, LANES:2 * LANES]
            if first:
                acc_s[qsl, :] = o
                l_s[qsl, :] = l
                m_s[qsl, :] = m
            else:
                m_old = m_s[qsl, :]
                m_new = jnp.maximum(m_old, m)
                a_old = jnp.exp(m_old - m_new)
                a_cur = jnp.exp(m - m_new)
                acc_s[qsl, :] = a_old * acc_s[qsl, :] + a_cur * o
                l_s[qsl, :] = a_old * l_s[qsl, :] + a_cur * l
                m_s[qsl, :] = m_new
            return carry

        lax.fori_loop(0, dil * nb, value_block, 0, unroll=unroll)

    def fin(i, carry):
        rows = pl.ds(pl.multiple_of(i * pc, pc), pc)
        o_ref[0, rows, :] = (acc_s[rows, :] / l_s[rows, :]).astype(BF16)
        return carry

    lax.fori_loop(0, n_pc, fin, 0)


def _dil(proj, q_norm, k_norm, batch, seq):
    per = seq

    def slab(base):
        return pl.BlockSpec((1, per, LANES), lambda b, h, base=base: (base + h, b, 0))

    vec = pl.BlockSpec((1, LANES), lambda b, h: (0, 0))
    return pl.pallas_call(
        functools.partial(_dil_kernel, seq=seq),
        grid=(batch, DIL_HEADS),
        in_specs=[slab(SLAB_QC), slab(SLAB_KC), slab(SLAB_VC), vec, vec],
        out_specs=pl.BlockSpec((1, per, LANES), lambda b, h: (h, b, 0)),
        out_shape=jax.ShapeDtypeStruct((DIL_HEADS, batch * seq, LANES), BF16),
        scratch_shapes=[
            pltpu.VMEM((seq, LANES), F32), pltpu.VMEM((seq, LANES), F32),
            pltpu.VMEM((seq, LANES), F32),
            pltpu.VMEM((seq // DIL_BLOCK, DIL_BLOCK, 2 * DIL_BLOCK), F32),
            pltpu.VMEM((seq // DIL_BLOCK, DIL_BLOCK, LANES), F32),
            pltpu.VMEM((seq, LANES), F32), pltpu.VMEM((seq, LANES), F32),
            pltpu.VMEM((seq, LANES), F32),
        ],
        compiler_params=_cparams(("arbitrary", "arbitrary")),
        name="dil_mixer",
    )(proj, proj, proj, q_norm.reshape(1, LANES), k_norm.reshape(1, LANES))


def _route_tile(sc, bias, run):
    ne, tm = sc.shape
    epg = EXPERTS_PER_GROUP
    sel = sc + bias
    jrow = lax.broadcasted_iota(jnp.int32, (epg, tm), 0)
    gs, i1s, i2s = [], [], []
    for g in range(N_GROUPS):
        sg = sel[g * epg:(g + 1) * epg, :]
        m1 = jnp.max(sg, axis=0, keepdims=True)
        i1 = jnp.min(jnp.where(sg == m1, jrow, epg), axis=0, keepdims=True)
        rest = jnp.where(jrow == i1, -jnp.inf, sg)
        m2 = jnp.max(rest, axis=0, keepdims=True)
        i2 = jnp.min(jnp.where(rest == m2, jrow, epg), axis=0, keepdims=True)
        gs.append(m1 + m2)
        i1s.append(i1)
        i2s.append(i2)
    gmax = functools.reduce(jnp.maximum, gs)
    gidx = jnp.full((1, tm), N_GROUPS - 1, jnp.int32)
    for g in reversed(range(N_GROUPS - 1)):
        gidx = jnp.where(gs[g] == gmax, g, gidx)
    loc1 = jnp.zeros((1, tm), jnp.int32)
    loc2 = jnp.zeros((1, tm), jnp.int32)
    for g in range(N_GROUPS):
        loc1 = jnp.where(gidx == g, i1s[g], loc1)
        loc2 = jnp.where(gidx == g, i2s[g], loc2)
    e1 = gidx * epg + loc1
    e2 = gidx * epg + loc2
    erow = lax.broadcasted_iota(jnp.int32, (ne, tm), 0)
    oh1 = erow == e1
    oh2 = erow == e2
    s1 = jnp.sum(jnp.where(oh1, sc, 0.0), axis=0, keepdims=True)
    s2 = jnp.sum(jnp.where(oh2, sc, 0.0), axis=0, keepdims=True)
    den = s1 + s2
    tr = lax.broadcasted_iota(jnp.int32, (tm, tm), 0)
    tc = lax.broadcasted_iota(jnp.int32, (tm, tm), 1)
    before = jnp.where(tr < tc, 1.0, 0.0).astype(BF16)
    f1 = jnp.where(oh1, 1.0, 0.0)
    f2 = jnp.where(oh2, 1.0, 0.0)
    pre1 = _dot(f1.astype(BF16), before)
    pre2 = _dot(f2.astype(BF16), before)
    tot1 = jnp.sum(f1, axis=1, keepdims=True)
    tot2 = jnp.sum(f2, axis=1, keepdims=True)
    r1 = jnp.sum(jnp.where(oh1, pre1 + run, 0.0), axis=0, keepdims=True)
    r2 = jnp.sum(jnp.where(oh2, pre2 + (run + tot1), 0.0), axis=0, keepdims=True)
    return e1, e2, s1 / den, s2 / den, r1, r2, run + tot1 + tot2


def _outproj_kernel(oa_ref, ob_ref, oc_ref, w_ref, x_ref, g_ref, nw_ref, sc_ref, sh_ref, wrt_ref, rb_ref,
                    xo_ref, hf_ref, ri_ref, rw_ref, cnt_ref, mix_s, run_s):
    @pl.when(pl.program_id(0) == 0)
    def _():
        run_s[...] = jnp.zeros(run_s.shape, F32)

    off = 0
    for ref, heads in ((oa_ref, GDN_HEADS), (ob_ref, DIFF_HEADS), (oc_ref, DIL_HEADS)):
        for k in range(heads):
            mix_s[:, off:off + LANES] = ref[k]
            off += LANES
    xn = x_ref[...] + g_ref[0] * _dot(mix_s[...], w_ref[...])
    xo_ref[...] = xn
    hb = _modulated_norm(xn, nw_ref[...], sc_ref[0], sh_ref[0]).astype(BF16)
    hf_ref[...] = _pack_bf16_pairs(hb)
    scores_t = jax.nn.sigmoid(_dot_nt(wrt_ref[...], hb))
    e1, e2, w1, w2, r1, r2, run = _route_tile(scores_t, rb_ref[...], run_s[...])
    run_s[...] = run
    cnt_ref[...] = jnp.broadcast_to(run, cnt_ref.shape)
    tm = e1.shape[1]
    ri_ref[...] = jnp.concatenate([e1, e2, r1.astype(jnp.int32), r2.astype(jnp.int32),
                                   jnp.zeros((4, tm), jnp.int32)], axis=0)
    rw_ref[...] = jnp.transpose(jnp.concatenate([w1, w2, jnp.zeros((6, tm), F32)], axis=0))


def _outproj(o_a, o_b, o_c, w_out, x2, g_a, nw, sc, sh, w_router_t, router_bias, seq):
    t, d = x2.shape
    tm = 256
    per_b = seq // tm
    ne = w_router_t.shape[0]
    row = pl.BlockSpec((tm, d), lambda i: (i, 0))
    bvec = pl.BlockSpec((1, 1, d), lambda i: (i // per_b, 0, 0))
    return pl.pallas_call(
        _outproj_kernel,
        grid=(t // tm,),
        in_specs=[
            pl.BlockSpec((GDN_HEADS, tm, LANES), lambda i: (0, i, 0)),
            pl.BlockSpec((DIFF_HEADS, tm, LANES), lambda i: (0, i, 0)),
            pl.BlockSpec((DIL_HEADS, tm, LANES), lambda i: (0, i, 0)),
            pl.BlockSpec((d, d), lambda i: (0, 0)),
            row, bvec,
            pl.BlockSpec((1, d), lambda i: (0, 0)),
            bvec, bvec,
            pl.BlockSpec((ne, d), lambda i: (0, 0)),
            pl.BlockSpec((ne, 1), lambda i: (0, 0)),
        ],
        out_specs=[row,
                   pl.BlockSpec((tm, d // 2), lambda i: (i, 0)),
                   pl.BlockSpec((8, tm), lambda i: (0, i)),
                   pl.BlockSpec((tm, 8), lambda i: (i, 0)),
                   pl.BlockSpec((ne, LANES), lambda i: (0, 0))],
        out_shape=[jax.ShapeDtypeStruct((t, d), F32), jax.ShapeDtypeStruct((t, d // 2), jnp.int32),
                   jax.ShapeDtypeStruct((8, t), jnp.int32), jax.ShapeDtypeStruct((t, 8), F32),
                   jax.ShapeDtypeStruct((ne, LANES), F32)],
        scratch_shapes=[pltpu.VMEM((tm, d), BF16), pltpu.VMEM((ne, 1), F32)],
        compiler_params=_cparams(("arbitrary",)),
        name="outproj",
    )(o_a, o_b, o_c, w_out, x2, g_a, nw, sc, sh, w_router_t, router_bias.reshape(ne, 1).astype(F32))


def _moe_kernel(be_ref, nv_ref, first_ref, wpar_ref, wnext_ref, tok_ref, tok_next_ref, dst_spare_ref,
                dst_prev_ref, dst_ref, hf_hbm, wg_hbm, wu_hbm, wd_hbm, y_hbm,
                xbuf, ybuf, wg_f, wu_f, wd_f, wg_s, wu_s, wd_s, gsem, ssem, wsem, *, layer):
    i = pl.program_id(0)
    nv = nv_ref[0]
    slot = i % 2
    rows = MOE_ROWS

    def gather_copy(tok_r, r, s):
        return pltpu.make_async_copy(hf_hbm.at[pl.ds(tok_r[0, 0, r], 1)], xbuf.at[s, pl.ds(r, 1)], gsem.at[s])

    def scatter_copy(dst_r, r, s):
        return pltpu.make_async_copy(ybuf.at[s, pl.ds(r, 1)], y_hbm.at[pl.ds(dst_r[0, 0, r], 1)], ssem.at[s])

    def weight_copies(e, par):
        return (pltpu.make_async_copy(wg_hbm.at[layer, e], wg_f.at[par], wsem.at[par]),
                pltpu.make_async_copy(wu_hbm.at[layer, e], wu_f.at[par], wsem.at[par]),
                pltpu.make_async_copy(wd_hbm.at[layer, e], wd_f.at[par], wsem.at[par]))

    @pl.when(i == 0)
    def _():
        ybuf[...] = jnp.zeros(ybuf.shape, jnp.int32)
        for r in range(rows):
            gather_copy(tok_ref, r, 0).start(priority=r % 2)
        for r in range(rows):
            scatter_copy(dst_spare_ref, r, 0).start(priority=r % 2)
        for c in weight_copies(be_ref[0], wpar_ref[0]):
            c.start()

    @pl.when(first_ref[i] == 1)
    def _():
        par = wpar_ref[i]
        for c in weight_copies(be_ref[i], par):
            c.wait()

        @pl.when(wnext_ref[i] >= 0)
        def _():
            for c in weight_copies(wnext_ref[i], 1 - par):
                c.start()

        wg_s[...] = wg_f[par].astype(BF16)
        wu_s[...] = wu_f[par].astype(BF16)
        wd_s[...] = wd_f[par].astype(BF16)

    @pl.when(i < nv)
    def _():
        for r in range(rows):
            gather_copy(tok_ref, r, slot).wait()
        x = _unpack_bf16_pairs(xbuf[slot]).astype(BF16)
        for r in range(rows):
            gather_copy(tok_next_ref, r, 1 - slot).start(priority=r % 2)
        for r in range(rows):
            scatter_copy(dst_prev_ref, r, 1 - slot).start(priority=r % 2)
        a = _dot(x, wg_s[...])
        u = _dot(x, wu_s[...])
        y = _pack_bf16_pairs(_dot((_silu(a) * u).astype(BF16), wd_s[...]).astype(BF16))

        for r in range(rows):
            scatter_copy(dst_ref, r, slot).wait()
        ybuf[slot] = y

    @pl.when(i == nv - 1)
    def _():
        for r in range(rows):
            gather_copy(tok_next_ref, r, 1 - slot).wait()
            scatter_copy(dst_prev_ref, r, 1 - slot).wait()
        for r in range(rows):
            scatter_copy(dst_ref, r, slot).start(priority=r % 2)
        for r in range(rows):
            scatter_copy(dst_ref, r, slot).wait()


def _moe(hf, layout, w_gate, w_up, w_down, layer):
    block_e, n_valid, first, wpar, wnext, row_tok, row_dst = layout
    t = hf.shape[0]
    d = w_gate.shape[2]
    f = w_gate.shape[3]
    rows = MOE_ROWS
    nb = block_e.shape[0]
    tok3 = row_tok.reshape(nb, 1, rows)
    spare = (TOP_K * t + jnp.arange(2 * rows, dtype=jnp.int32)).reshape(2, 1, rows)
    dst3 = jnp.concatenate([spare, row_dst.reshape(nb, 1, rows)], axis=0)
    smem_blk = lambda fn: pl.BlockSpec((1, 1, rows), fn, memory_space=pltpu.SMEM)
    hbm = pl.BlockSpec(memory_space=pl.ANY)
    grid_spec = pltpu.PrefetchScalarGridSpec(
        num_scalar_prefetch=5,
        grid=(nb,),
        in_specs=[
            smem_blk(lambda i, *_: (i, 0, 0)),
            smem_blk(lambda i, *_: (jnp.minimum(i + 1, nb - 1), 0, 0)),
            smem_blk(lambda i, *_: (0, 0, 0)),
            smem_blk(lambda i, *_: (i + 1, 0, 0)),
            smem_blk(lambda i, *_: (i + 2, 0, 0)),
            hbm, hbm, hbm, hbm,
        ],
        out_specs=hbm,
        scratch_shapes=[
            pltpu.VMEM((2, rows, d // 2), jnp.int32),
            pltpu.VMEM((2, rows, d // 2), jnp.int32),
            pltpu.VMEM((2, d, f), F32), pltpu.VMEM((2, d, f), F32), pltpu.VMEM((2, f, d), F32),
            pltpu.VMEM((d, f), BF16), pltpu.VMEM((d, f), BF16), pltpu.VMEM((f, d), BF16),
            pltpu.SemaphoreType.DMA((2,)),
            pltpu.SemaphoreType.DMA((2,)),
            pltpu.SemaphoreType.DMA((2,)),
        ],
    )
    return pl.pallas_call(
        functools.partial(_moe_kernel, layer=layer),
        grid_spec=grid_spec,
        out_shape=jax.ShapeDtypeStruct((TOP_K * t + 2 * rows, d // 2), jnp.int32),
        compiler_params=_cparams(("arbitrary",), has_side_effects=True, disable_bounds_checks=True),
        name="moe_experts",
    )(block_e, n_valid, first, wpar, wnext, tok3, tok3, dst3, dst3, dst3, hf, w_gate, w_up, w_down)


def _block_layout(route_i, counts_f):
    t = route_i.shape[1]
    rows = MOE_ROWS
    n_assign = TOP_K * t
    counts = counts_f[:, 0].astype(jnp.int32)
    blocks_per_e = (counts + rows - 1) // rows
    blk_end = jnp.cumsum(blocks_per_e)
    blk_start = blk_end - blocks_per_e
    flat_e = route_i[0:TOP_K].reshape(n_assign)
    dest = blk_start[flat_e] * rows + route_i[TOP_K:2 * TOP_K].reshape(n_assign)
    nb = n_assign // rows + N_EXPERTS
    n_rows = nb * rows
    assign = jnp.arange(n_assign, dtype=jnp.int32)
    ridx = jnp.arange(n_rows, dtype=jnp.int32)
    spare = n_assign + ((ridx // rows) % 2) * rows + ridx % rows
    row_dst = spare.at[dest].set(assign, unique_indices=True, mode='promise_in_bounds')
    row_tok = jnp.where(row_dst < n_assign, row_dst % t, 0)
    n_valid = blk_end[-1].astype(jnp.int32)
    bidx = jnp.arange(nb, dtype=jnp.int32)
    block_e = jnp.sum((blk_end[None, :] <= bidx[:, None]).astype(jnp.int32), axis=1)
    block_e = jnp.minimum(block_e, N_EXPERTS - 1)
    last_e = block_e[jnp.maximum(n_valid - 1, 0)]
    block_e = jnp.where(bidx < n_valid, block_e, last_e)
    prev_e = jnp.concatenate([jnp.full((1,), -1, jnp.int32), block_e[:-1]])
    first = ((bidx < n_valid) & (block_e != prev_e)).astype(jnp.int32)
    wpar = (jnp.cumsum(first) + 1) % 2
    first_at = jnp.where(first == 1, bidx, nb)
    later = jnp.concatenate([lax.cummin(first_at[::-1])[::-1][1:], jnp.full((1,), nb, jnp.int32)])
    wnext = jnp.where(later < nb, block_e[jnp.minimum(later, nb - 1)], -1).astype(jnp.int32)
    return block_e, n_valid.reshape(1), first, wpar.astype(jnp.int32), wnext, row_tok, row_dst


def _combine_kernel(x_ref, y0_ref, y1_ref, rw_ref, g_ref, o_ref):
    w = rw_ref[...]
    y0 = _unpack_bf16_pairs(y0_ref[...])
    y1 = _unpack_bf16_pairs(y1_ref[...])
    o_ref[...] = x_ref[...] + g_ref[0] * (y0 * w[:, 0:1] + y1 * w[:, 1:2])


def _combine(x2, y2, route_w, g_f, seq):
    t, d = x2.shape
    tm = 256
    per_b = seq // tm
    n_blk = t // tm
    return pl.pallas_call(
        _combine_kernel,
        grid=(n_blk,),
        in_specs=[pl.BlockSpec((tm, d), lambda i: (i, 0)),
                  pl.BlockSpec((tm, d // 2), lambda i: (i, 0)),
                  pl.BlockSpec((tm, d // 2), lambda i: (n_blk + i, 0)),
                  pl.BlockSpec((tm, 8), lambda i: (i, 0)),
                  pl.BlockSpec((1, 1, d), lambda i: (i // per_b, 0, 0))],
        out_specs=pl.BlockSpec((tm, d), lambda i: (i, 0)),
        out_shape=jax.ShapeDtypeStruct((t, d), F32),
        compiler_params=_cparams(("arbitrary",)),
        name="moe_combine",
    )(x2, y2, y2, route_w, g_f)


def kernel(x, c, norm_mix, norm_ffn, w_ada, b_ada, w_in, conv_w, a_log, dt_bias, gdn_norm, diff_q_norm, diff_k_norm, lam_q1, lam_k1, lam_q2, lam_k2, diff_subln, dil_q_norm, dil_k_norm, w_out, w_router, router_bias, w_gate, w_up, w_down):
    batch, seq, d = x.shape
    depth = w_ada.shape[0]
    t = batch * seq
    x2 = x.reshape(t, d)
    mod = _ada(c, w_ada, b_ada).reshape(depth, batch, N_MOD, 1, d)
    ba_lo = 3 * GDN_HEADS * LANES + GDN_HEADS * LANES
    ba_hi = ba_lo + 2 * GDN_HEADS
    w_router_t = w_router.T.astype(BF16)
    for l in range(depth):
        sh_a, sc_a, g_a, sh_f, sc_f, g_f = (mod[l, :, k] for k in range(N_MOD))
        w_main = jnp.concatenate([w_in[l, :, :ba_lo], w_in[l, :, ba_hi:]], axis=1).astype(BF16)
        w_ba = jnp.pad(w_in[l, :, ba_lo:ba_hi], ((0, 0), (0, LANES - 2 * GDN_HEADS))).astype(BF16)
        proj, ba = _inproj(x2, norm_mix[l].reshape(1, d), sc_a, sh_a, w_main, w_ba, seq)
        o_a = _gdn(proj, ba, conv_w[l], a_log[l], dt_bias[l], gdn_norm[l], batch, seq)
        lam_init = 0.8 - 0.6 * math.exp(-0.3 * l)
        lam_vecs = jnp.stack([lam_q1[l], lam_k1[l], lam_q2[l], lam_k2[l]]).astype(F32)
        o_b = _diff(proj, diff_q_norm[l], diff_k_norm[l], diff_subln[l], lam_vecs, lam_init, batch, seq)
        o_c = _dil(proj, dil_q_norm[l], dil_k_norm[l], batch, seq)
        x2, hf, route_i, route_w, counts = _outproj(o_a, o_b, o_c, w_out[l].astype(BF16), x2, g_a,
                                                    norm_ffn[l].reshape(1, d), sc_f, sh_f, w_router_t, router_bias, seq)
        y2 = _moe(hf, _block_layout(route_i, counts), w_gate, w_up, w_down, l)
        x2 = _combine(x2, y2, route_w, g_f, seq)
    return x2.reshape(batch, seq, d)
```

```python
import functools
import math

import jax
import jax.numpy as jnp
from jax import lax
from jax.experimental import pallas as pl
from jax.experimental.pallas import tpu as pltpu

F32 = jnp.float32
BF16 = jnp.bfloat16

LANES = 128
EPS = 1e-6
N_MOD = 6

GDN_HEADS = 6
GDN_CONV = 4
GDN_CHUNK = 256
GDN_LOCKSTEP = 4
DIFF_HEADS = 4
DIFF_QK_DIM = 64
DIL_HEADS = 6
DILATED_PAIRS = ((128, 1), (512, 4), (2048, 16))
DIL_BLOCK = 128
DIL_SPLIT = 4

N_EXPERTS = 64
N_GROUPS = 8
EXPERTS_PER_GROUP = N_EXPERTS // N_GROUPS
TOP_K = 2
MOE_ROWS = 256

SLAB_QA, SLAB_KA, SLAB_VA, SLAB_ZA = 0, 6, 12, 18
SLAB_QB, SLAB_KB, SLAB_VB = 24, 28, 32
SLAB_QC, SLAB_KC, SLAB_VC = 36, 42, 48
N_SLABS = 54

VMEM_LIMIT = 56 * 1024 * 1024


def _cparams(sem, vmem=VMEM_LIMIT, **kw):
    return pltpu.CompilerParams(dimension_semantics=sem, vmem_limit_bytes=vmem, **kw)


def _silu(v):
    return v * jax.nn.sigmoid(v)


def _dot(a, b):
    return jnp.dot(a, b, preferred_element_type=F32)


def _dot_nt(a, b):
    return lax.dot_general(a, b, (((1,), (1,)), ((), ())), preferred_element_type=F32)


def _dot_tn(a, b):
    return lax.dot_general(a, b, (((0,), (0,)), ((), ())), preferred_element_type=F32)


def _pack_bf16_pairs(v):
    half = v.shape[1] // 2
    lo = lax.bitcast_convert_type(v[:, :half].astype(F32), jnp.int32)
    hi = lax.bitcast_convert_type(v[:, half:].astype(F32), jnp.int32)
    return lax.shift_right_logical(lo, 16) | (hi & -65536)


def _unpack_bf16_pairs(w):
    return jnp.concatenate([lax.bitcast_convert_type(lax.shift_left(w, 16), F32),
                            lax.bitcast_convert_type(w & -65536, F32)], axis=1)


def _ada_kernel(c_ref, w_ref, b_ref, o_ref):
    cact = _silu(c_ref[...]).astype(BF16)
    o_ref[0] = _dot(cact, w_ref[0].astype(BF16)) + b_ref[0]


def _ada(c, w_ada, b_ada):
    depth, d, n = w_ada.shape
    b = c.shape[0]
    tn = 1024
    return pl.pallas_call(
        _ada_kernel,
        grid=(depth, n // tn),
        in_specs=[
            pl.BlockSpec((b, d), lambda l, j: (0, 0)),
            pl.BlockSpec((1, d, tn), lambda l, j: (l, 0, j)),
            pl.BlockSpec((1, 1, tn), lambda l, j: (l, 0, j)),
        ],
        out_specs=pl.BlockSpec((1, b, tn), lambda l, j: (l, 0, j)),
        out_shape=jax.ShapeDtypeStruct((depth, b, n), F32),
        compiler_params=_cparams(("arbitrary", "arbitrary")),
        name="ada_mod",
    )(c, w_ada, b_ada.reshape(depth, 1, n))


def _modulated_norm(x, nw, sc, sh):
    ms = jnp.mean(x * x, axis=-1, keepdims=True)
    return (x * lax.rsqrt(ms + EPS) * nw) * (1.0 + sc) + sh


def _inproj_kernel(x_ref, nw_ref, sc_ref, sh_ref, w_ref, wba_ref, o_ref, ba_ref, h_scr, *, n_sub):
    @pl.when(pl.program_id(1) == 0)
    def _():
        h = _modulated_norm(x_ref[...], nw_ref[...], sc_ref[0], sh_ref[0]).astype(BF16)
        h_scr[...] = h
        ba_ref[...] = _dot(h, wba_ref[...])

    acc = _dot(h_scr[...], w_ref[...])
    for k in range(n_sub):
        o_ref[k] = acc[:, k * LANES:(k + 1) * LANES].astype(BF16)


def _inproj(x2, nw, sc, sh, w_main, w_ba, seq):
    t, d = x2.shape
    n = w_main.shape[1]
    tm = min(1024, seq)
    tn = 768
    n_sub = tn // LANES
    per_b = seq // tm
    return pl.pallas_call(
        functools.partial(_inproj_kernel, n_sub=n_sub),
        grid=(t // tm, n // tn),
        in_specs=[
            pl.BlockSpec((tm, d), lambda i, j: (i, 0)),
            pl.BlockSpec((1, d), lambda i, j: (0, 0)),
            pl.BlockSpec((1, 1, d), lambda i, j: (i // per_b, 0, 0)),
            pl.BlockSpec((1, 1, d), lambda i, j: (i // per_b, 0, 0)),
            pl.BlockSpec((d, tn), lambda i, j: (0, j)),
            pl.BlockSpec((d, LANES), lambda i, j: (0, 0)),
        ],
        out_specs=[
            pl.BlockSpec((n_sub, tm, LANES), lambda i, j: (j, i, 0)),
            pl.BlockSpec((tm, LANES), lambda i, j: (i, 0)),
        ],
        out_shape=[
            jax.ShapeDtypeStruct((n // LANES, t, LANES), BF16),
            jax.ShapeDtypeStruct((t, LANES), F32),
        ],
        scratch_shapes=[pltpu.VMEM((tm, d), BF16)],
        compiler_params=_cparams(("arbitrary", "arbitrary")),
        name="inproj",
    )(x2, nw, sc, sh, w_main, w_ba)


def _unit_lower_inverse(lms, xr):
    c = lms[0].shape[0]

    def mm(a, b):
        return [_dot(ai.astype(BF16), bi.astype(BF16)) for ai, bi in zip(a, b)]

    d1 = [jnp.where(xr < 16, lm, 0.0) for lm in lms]
    d2 = mm(d1, d1)
    d4 = mm(d2, d2)
    d8 = mm(d4, d4)
    p = [b - a - t for a, b, t in zip(d1, d2, mm(d1, d2))]
    p = [pi + di + t for pi, di, t in zip(p, d4, mm(p, d4))]
    p = [pi + di + t for pi, di, t in zip(p, d8, mm(p, d8))]
    eye = jnp.where(xr == 0, 1.0, 0.0)
    x = [eye + pi for pi in p]
    blk = 32
    while blk <= c:
        e = [jnp.where((xr < blk) & (xr >= blk // 2), lm, 0.0) for lm in lms]
        x = [xi - t for xi, t in zip(x, mm(x, mm(e, x)))]
        blk *= 2
    return x


def _gdn_kernel(q_ref, k_ref, v_ref, z_ref, ba_ref, cwq_ref, cwk_ref, cwv_ref, al_ref, dtb_ref, nw_ref,
                o_ref, xq, xk, xv, u_s, w_s, a_s, qd_s, kd_s, cd_s, *, seq, chunk):
    c = chunk
    n_chunks = seq // c
    head = pl.program_id(1)
    dk = LANES

    lane1 = lax.broadcasted_iota(jnp.int32, (1, LANES), 1)
    a_exp = jnp.exp(jnp.sum(jnp.where(lane1 == head, al_ref[...], 0.0), axis=-1, keepdims=True))
    dtb = jnp.sum(jnp.where(lane1 == head, dtb_ref[...], 0.0), axis=-1, keepdims=True)

    zeros8 = jnp.zeros((8, LANES), F32)
    for src, dst in ((q_ref, xq), (k_ref, xk), (v_ref, xv)):
        dst[0:8, :] = zeros8

        def stage(i, carry, src=src, dst=dst):
            r0 = pl.multiple_of(i * c, c)
            dst[pl.ds(r0 + 8, c), :] = src[0, pl.ds(r0, c), :].astype(F32)
            return carry

        lax.fori_loop(0, n_chunks, stage, 0)

    row = lax.broadcasted_iota(jnp.int32, (c, c), 0)
    col = lax.broadcasted_iota(jnp.int32, (c, c), 1)
    causal = row >= col
    strict = row > col
    xr = row ^ col
    tril = jnp.where(causal, 1.0, 0.0).astype(BF16)
    lane_c = lax.broadcasted_iota(jnp.int32, (c, LANES), 1)

    def conv(xs, cw_ref, r0):
        acc = cw_ref[0, GDN_CONV - 1:GDN_CONV, :] * xs[pl.ds(r0 + 8, c), :]
        for back in range(1, GDN_CONV):
            tap = GDN_CONV - 1 - back
            acc = acc + cw_ref[0, tap:tap + 1, :] * xs[pl.ds(r0 + 8 - back, c), :]
        return _silu(acc)

    def l2n(t):
        return t * lax.rsqrt(jnp.sum(t * t, axis=-1, keepdims=True) + EPS)

    def log_decay(r0):
        bat = ba_ref[pl.ds(r0, c), :]
        bcol = jnp.sum(jnp.where(lane_c == head, bat, 0.0), axis=-1, keepdims=True)
        acol = jnp.sum(jnp.where(lane_c == head + GDN_HEADS, bat, 0.0), axis=-1, keepdims=True)
        xsp = acol + dtb
        softplus = jnp.maximum(xsp, 0.0) + jnp.log1p(jnp.exp(-jnp.abs(xsp)))
        return jax.nn.sigmoid(bcol), jnp.broadcast_to(-a_exp * softplus, (c, LANES))

    def intra(step, carry):
        par = range(GDN_LOCKSTEP)
        ns = [step * GDN_LOCKSTEP + j for j in par]
        r0s = [pl.multiple_of(n * c, c) for n in ns]
        qn = [l2n(conv(xq, cwq_ref, r0)) * (dk ** -0.5) for r0 in r0s]
        kn = [l2n(conv(xk, cwk_ref, r0)) for r0 in r0s]
        vc = [conv(xv, cwv_ref, r0) for r0 in r0s]
        bg = [log_decay(r0) for r0 in r0s]
        beta = [t[0] for t in bg]
        g_hi = [t[1].astype(BF16) for t in bg]
        g_lo = [(t[1] - h.astype(F32)).astype(BF16) for t, h in zip(bg, g_hi)]
        gc = [_dot(tril, h) + _dot(tril, lo_) for h, lo_ in zip(g_hi, g_lo)]
        decay = []
        for gci in gc:
            gc_row = jnp.transpose(gci)[0:1, :]
            diff = jnp.concatenate([gci] * (c // LANES), axis=1) - gc_row
            decay.append(jnp.where(causal, jnp.exp(jnp.where(causal, diff, 0.0)), 0.0))
        kb = [k * b for k, b in zip(kn, beta)]
        knb = [k.astype(BF16) for k in kn]
        lm = [jnp.where(strict, _dot_nt(kbi.astype(BF16), ki) * dc, 0.0) for kbi, ki, dc in zip(kb, knb, decay)]
        attn = [jnp.where(causal, _dot_nt(q.astype(BF16), ki) * dc, 0.0) for q, ki, dc in zip(qn, knb, decay)]
        tinv = _unit_lower_inverse(lm, xr)
        egc = [jnp.exp(g) for g in gc]
        sol = [_dot(ti.astype(BF16), jnp.concatenate([v * b, kbi * e], axis=1).astype(BF16))
               for ti, v, b, kbi, e in zip(tinv, vc, beta, kb, egc)]
        for j in par:
            rows = pl.ds(r0s[j], c)
            u_s[rows, :] = sol[j][:, :LANES]
            w_s[rows, :] = sol[j][:, LANES:].astype(BF16)
            a_s[rows, :] = attn[j].astype(BF16)
            qd_s[rows, :] = (qn[j] * egc[j]).astype(BF16)
            g_last = gc[j][c - 1:c, :]
            kd_s[rows, :] = (kn[j] * jnp.exp(g_last - gc[j])).astype(BF16)
            cd_s[pl.ds(pl.multiple_of(ns[j] * 8, 8), 8), :] = jnp.broadcast_to(jnp.exp(g_last), (8, LANES))
        return carry

    lax.fori_loop(0, n_chunks // GDN_LOCKSTEP, intra, 0)

    def inter(n, state):
        r0 = pl.multiple_of(n * c, c)
        rows = pl.ds(r0, c)
        sb = state.astype(BF16)
        v_new = u_s[rows, :] - _dot(w_s[rows, :], sb)
        vb = v_new.astype(BF16)
        o = _dot(qd_s[rows, :], sb) + _dot(a_s[rows, :], vb)
        cd = cd_s[pl.ds(pl.multiple_of(n * 8, 8), 1), :]
        new_state = state * cd + _dot_tn(kd_s[rows, :], vb)
        on = o * lax.rsqrt(jnp.mean(o * o, axis=-1, keepdims=True) + EPS) * nw_ref[...]
        z = z_ref[0, rows, :].astype(F32)
        o_ref[0, rows, :] = (on * _silu(z)).astype(BF16)
        return new_state

    lax.fori_loop(0, n_chunks, inter, jnp.zeros((dk, LANES), F32), unroll=2)


def _gdn(proj, ba, conv_w, a_log, dt_bias, gdn_norm, batch, seq):
    c = GDN_CHUNK
    per = seq
    cw = conv_w.reshape(GDN_CONV, 3 * GDN_HEADS, LANES).transpose(1, 0, 2)
    pad = lambda v: jnp.pad(v, (0, LANES - v.shape[0])).reshape(1, LANES)

    def slab(base):
        return pl.BlockSpec((1, per, LANES), lambda b, h, base=base: (base + h, b, 0))

    def cwspec(base):
        return pl.BlockSpec((1, GDN_CONV, LANES), lambda b, h, base=base: (base + h, 0, 0))

    vec = pl.BlockSpec((1, LANES), lambda b, h: (0, 0))
    return pl.pallas_call(
        functools.partial(_gdn_kernel, seq=seq, chunk=c),
        grid=(batch, GDN_HEADS),
        in_specs=[slab(SLAB_QA), slab(SLAB_KA), slab(SLAB_VA), slab(SLAB_ZA),
                  pl.BlockSpec((per, LANES), lambda b, h: (b, 0)),
                  cwspec(0), cwspec(GDN_HEADS), cwspec(2 * GDN_HEADS), vec, vec, vec],
        out_specs=pl.BlockSpec((1, per, LANES), lambda b, h: (h, b, 0)),
        out_shape=jax.ShapeDtypeStruct((GDN_HEADS, batch * seq, LANES), BF16),
        scratch_shapes=[
            pltpu.VMEM((seq + 8, LANES), F32), pltpu.VMEM((seq + 8, LANES), F32),
            pltpu.VMEM((seq + 8, LANES), F32),
            pltpu.VMEM((seq, LANES), F32),
            pltpu.VMEM((seq, LANES), BF16),
            pltpu.VMEM((seq, c), BF16),
            pltpu.VMEM((seq, LANES), BF16),
            pltpu.VMEM((seq, LANES), BF16),
            pltpu.VMEM((seq // c * 8, LANES), F32),
        ],
        compiler_params=_cparams(("arbitrary", "arbitrary")),
        name="gdn_mixer",
    )(proj, proj, proj, proj, ba, cw, cw, cw, pad(a_log), pad(dt_bias), gdn_norm.reshape(1, LANES))


def _diff_kernel(q_ref, k_ref, v_ref, qw_ref, kw_ref, sw_ref, lam_ref, o_ref,
                 q1_s, q2_s, kn_s, ve_s, sc1, sc2, acc1, acc2, m1, m2, *, seq, tq, tk, lam_init):
    dh = DIFF_QK_DIM
    n_blk = seq // tq
    per_q = tq // tk
    lane = lax.broadcasted_iota(jnp.int32, (1, LANES), 1)
    lo = lane < dh

    lam = (jnp.exp(jnp.sum(lam_ref[0:1, :] * lam_ref[1:2, :], axis=-1, keepdims=True))
           - jnp.exp(jnp.sum(lam_ref[2:3, :] * lam_ref[3:4, :], axis=-1, keepdims=True)) + lam_init)

    def halfnorm(t, w):
        sq = t * t
        s_lo = jnp.sum(jnp.where(lo, sq, 0.0), axis=-1, keepdims=True)
        s_hi = jnp.sum(jnp.where(lo, 0.0, sq), axis=-1, keepdims=True)
        r = jnp.where(lo, lax.rsqrt(s_lo / dh + EPS), lax.rsqrt(s_hi / dh + EPS))
        return t * r * w

    ones_blk = jnp.ones((tk, LANES), BF16)

    def prep(i, carry):
        rows = pl.ds(pl.multiple_of(i * tk, tk), tk)
        kn_s[rows, :] = halfnorm(k_ref[0, rows, :].astype(F32), kw_ref[...]).astype(BF16)
        qn = halfnorm(q_ref[0, rows, :].astype(F32), qw_ref[...]) * (dh ** -0.5)
        q1_s[rows, :] = jnp.where(lo, qn, 0.0).astype(BF16)
        q2_s[rows, :] = jnp.where(lo, 0.0, qn).astype(BF16)
        ve_s[rows, 0:LANES] = v_ref[0, rows, :]
        ve_s[rows, LANES:2 * LANES] = ones_blk
        return carry

    lax.fori_loop(0, seq // tk, prep, 0)

    row = lax.broadcasted_iota(jnp.int32, (tq, tk), 0)
    col = lax.broadcasted_iota(jnp.int32, (tq, tk), 1)

    n_parts = tk // LANES

    def q_block(qi, carry):
        qrows = pl.ds(pl.multiple_of(qi * tq, tq), tq)
        maps = ((q1_s[qrows, :], sc1, m1, acc1), (q2_s[qrows, :], sc2, m2, acc2))
        for _, _, m_r, a_r in maps:
            m_r[...] = jnp.full((tq, LANES), -jnp.inf, F32)
            a_r[...] = jnp.zeros((tq, 2 * LANES), F32)

        def score_step(kj, masked):
            kb = kn_s[pl.ds(pl.multiple_of(kj * tk, tk), tk), :]
            for qq, s_c, m_r, _ in maps:
                s = _dot_nt(qq, kb)
                if masked:
                    s = jnp.where(row + qi * tq >= col + kj * tk, s, -jnp.inf)
                s_c[kj] = s
                smax = m_r[...]
                for part in range(n_parts):
                    smax = jnp.maximum(smax, s[:, part * LANES:(part + 1) * LANES])
                m_r[...] = smax

        def off_diag(kq, c2):
            for part in range(per_q):
                score_step(kq * per_q + part, False)
            return c2

        lax.fori_loop(0, qi, off_diag, 0)
        for dblk in range(per_q):
            score_step(qi * per_q + dblk, True)
        for _, _, m_r, _ in maps:
            m_r[...] = jnp.broadcast_to(jnp.max(m_r[...], axis=-1, keepdims=True), (tq, LANES))

        def value_step(kq, c2):
            for _, s_c, m_r, a_r in maps:
                mrep = m_r[...]
                upd = a_r[...]
                for sub in range(per_q):
                    kj = kq * per_q + sub
                    s = s_c[kj]
                    p = jnp.concatenate(
                        [jnp.exp(s[:, part * LANES:(part + 1) * LANES] - mrep) for part in range(n_parts)], axis=1)
                    upd = upd + _dot(p.astype(BF16), ve_s[pl.ds(pl.multiple_of(kj * tk, tk), tk), :])
                a_r[...] = upd
            return c2

        lax.fori_loop(0, qi + 1, value_step, 0)
        o = (acc1[:, 0:LANES] / acc1[:, LANES:2 * LANES]
             - lam * (acc2[:, 0:LANES] / acc2[:, LANES:2 * LANES]))
        o = o * lax.rsqrt(jnp.mean(o * o, axis=-1, keepdims=True) + EPS) * sw_ref[...] * (1.0 - lam_init)
        o_ref[0, qrows, :] = o.astype(BF16)
        return carry

    lax.fori_loop(0, n_blk, q_block, 0)


def _diff(proj, q_norm, k_norm, subln, lam_vecs, lam_init, batch, seq):
    tq = 512
    tk = 256
    per = seq

    def slab(base):
        return pl.BlockSpec((1, per, LANES), lambda b, h, base=base: (base + h, b, 0))

    vec = pl.BlockSpec((1, LANES), lambda b, h: (0, 0))
    tile2 = lambda w: jnp.concatenate([w, w]).reshape(1, LANES)
    return pl.pallas_call(
        functools.partial(_diff_kernel, seq=seq, tq=tq, tk=tk, lam_init=lam_init),
        grid=(batch, DIFF_HEADS),
        in_specs=[slab(SLAB_QB), slab(SLAB_KB), slab(SLAB_VB), vec, vec, vec,
                  pl.BlockSpec((4, DIFF_QK_DIM), lambda b, h: (0, 0))],
        out_specs=pl.BlockSpec((1, per, LANES), lambda b, h: (h, b, 0)),
        out_shape=jax.ShapeDtypeStruct((DIFF_HEADS, batch * seq, LANES), BF16),
        scratch_shapes=[
            pltpu.VMEM((seq, LANES), BF16), pltpu.VMEM((seq, LANES), BF16), pltpu.VMEM((seq, LANES), BF16),
            pltpu.VMEM((seq, 2 * LANES), BF16),
            pltpu.VMEM((seq // tk, tq, tk), F32), pltpu.VMEM((seq // tk, tq, tk), F32),
            pltpu.VMEM((tq, 2 * LANES), F32), pltpu.VMEM((tq, 2 * LANES), F32),
            pltpu.VMEM((tq, LANES), F32), pltpu.VMEM((tq, LANES), F32),
        ],
        compiler_params=_cparams(("arbitrary", "arbitrary")),
        name="diff_mixer",
    )(proj, proj, proj, tile2(q_norm), tile2(k_norm), subln.reshape(1, LANES), lam_vecs)


def _dil_kernel(q_ref, k_ref, v_ref, qw_ref, kw_ref, o_ref, qn_s, kn_s, vf_s, q4_s, k4_s, v4_s, sc_s, mb_s,
                acc_s, l_s, m_s, acc4_s, l4_s, m4_s, *, seq):
    blk = DIL_BLOCK
    dh = LANES
    pc = 256
    n_pc = seq // pc
    unroll = 4

    def rms(t, w):
        return t * lax.rsqrt(jnp.mean(t * t, axis=-1, keepdims=True) + EPS) * w

    ones_kv = jnp.ones((2 * blk, LANES), BF16)

    def prep(i, carry):
        rows = pl.ds(pl.multiple_of(i * pc, pc), pc)
        qn_s[rows, :] = rms(q_ref[0, rows, :].astype(F32), qw_ref[...]) * (dh ** -0.5)
        kn_s[rows, :] = rms(k_ref[0, rows, :].astype(F32), kw_ref[...])
        vf_s[rows, :] = v_ref[0, rows, :].astype(F32)
        return carry

    lax.fori_loop(0, n_pc, prep, 0)

    sp = DIL_SPLIT
    cls_rows = seq // sp
    cps = cls_rows // pc

    def chunk_slices(i):
        c4 = i // cps
        c = i - c4 * cps
        return (pl.ds(c4 + sp * pc * c, pc, stride=sp),
                pl.ds(pl.multiple_of(c4 * cls_rows + pc * c, pc), pc))

    def split(i, carry):
        nat, spl = chunk_slices(i)
        q4_s[spl, :] = qn_s[nat, :]
        k4_s[spl, :] = kn_s[nat, :]
        v4_s[spl, :] = vf_s[nat, :]
        return carry

    lax.fori_loop(0, sp * cps, split, 0)

    qi = lax.broadcasted_iota(jnp.int32, (blk, 2 * blk), 0)
    kj = lax.broadcasted_iota(jnp.int32, (blk, 2 * blk), 1)
    seen_layouts = set()

    for window, dil in DILATED_PAIRS:
        steps = window // dil
        n_sub = seq // dil
        nb = n_sub // blk
        assert steps == blk and nb >= 2 and (dil * nb) % unroll == 0
        natural = dil == 1
        assert natural or dil % sp == 0
        first = natural not in seen_layouts
        seen_layouts.add(natural)
        qsrc, ksrc, vsrc, acc_r, l_r, m_r = ((qn_s, kn_s, vf_s, acc_s, l_s, m_s) if natural
                                             else (q4_s, k4_s, v4_s, acc4_s, l4_s, m4_s))

        def slices(idx, dil=dil, nb=nb, natural=natural):
            r = idx // nb
            n = idx - r * nb
            kbase = jnp.maximum(n - 1, 0) * blk
            if natural:
                return n, kbase, pl.ds(pl.multiple_of(blk * n, blk), blk), pl.ds(pl.multiple_of(kbase, blk), 2 * blk)
            d2 = dil // sp
            r_hi = r // sp
            off = (r - r_hi * sp) * cls_rows + r_hi
            if d2 == 1:
                return (n, kbase, pl.ds(pl.multiple_of(off + blk * n, blk), blk),
                        pl.ds(pl.multiple_of(off + kbase, blk), 2 * blk))
            return (n, kbase, pl.ds(off + d2 * blk * n, blk, stride=d2),
                    pl.ds(off + d2 * kbase, 2 * blk, stride=d2))

        def score_block(idx, carry, slices=slices, qsrc=qsrc, ksrc=ksrc):
            n, kbase, qsl, ksl = slices(idx)
            s = _dot_nt(qsrc[qsl, :].astype(BF16), ksrc[ksl, :].astype(BF16))
            dist = (n * blk + qi) - (kbase + kj)
            s = jnp.where((dist >= 0) & (dist <= steps), s, -jnp.inf)
            sc_s[idx] = s
            m = jnp.max(jnp.maximum(s[:, 0:blk], s[:, blk:2 * blk]), axis=-1, keepdims=True)
            mb_s[idx] = jnp.broadcast_to(m, (blk, LANES))
            return carry

        lax.fori_loop(0, dil * nb, score_block, 0, unroll=unroll)

        def value_block(idx, carry, slices=slices, first=first, vsrc=vsrc, acc_r=acc_r, l_r=l_r, m_r=m_r):
            _, _, qsl, ksl = slices(idx)
            s = sc_s[idx]
            m = mb_s[idx]
            p = jnp.concatenate([jnp.exp(s[:, 0:blk] - m), jnp.exp(s[:, blk:2 * blk] - m)], axis=1)
            ve = jnp.concatenate([vsrc[ksl, :].astype(BF16), ones_kv], axis=1)
            ol = _dot(p.astype(BF16), ve)
            o = ol[:, 0:LANES]
            l = ol[:, LANES:2 * LANES]
            if first:
                acc_r[qsl, :] = o
                l_r[qsl, :] = l
                m_r[qsl, :] = m
            else:
                m_old = m_r[qsl, :]
                m_new = jnp.maximum(m_old, m)
                a_old = jnp.exp(m_old - m_new)
                a_cur = jnp.exp(m - m_new)
                acc_r[qsl, :] = a_old * acc_r[qsl, :] + a_cur * o
                l_r[qsl, :] = a_old * l_r[qsl, :] + a_cur * l
                m_r[qsl, :] = m_new
            return carry

        lax.fori_loop(0, dil * nb, value_block, 0, unroll=unroll)

    assert seen_layouts == {True, False}

    def merge(i, carry):
        nat, spl = chunk_slices(i)
        m_nat = m_s[nat, :]
        m_spl = m4_s[spl, :]
        m_new = jnp.maximum(m_nat, m_spl)
        a_nat = jnp.exp(m_nat - m_new)
        a_spl = jnp.exp(m_spl - m_new)
        acc_s[nat, :] = ((a_nat * acc_s[nat, :] + a_spl * acc4_s[spl, :])
                         / (a_nat * l_s[nat, :] + a_spl * l4_s[spl, :]))
        return carry

    lax.fori_loop(0, sp * cps, merge, 0)

    def fin(i, carry):
        rows = pl.ds(pl.multiple_of(i * pc, pc), pc)
        o_ref[0, rows, :] = acc_s[rows, :].astype(BF16)
        return carry

    lax.fori_loop(0, n_pc, fin, 0)


def _dil(proj, q_norm, k_norm, batch, seq):
    per = seq

    def slab(base):
        return pl.BlockSpec((1, per, LANES), lambda b, h, base=base: (base + h, b, 0))

    vec = pl.BlockSpec((1, LANES), lambda b, h: (0, 0))
    row = lambda: pltpu.VMEM((seq, LANES), F32)
    return pl.pallas_call(
        functools.partial(_dil_kernel, seq=seq),
        grid=(batch, DIL_HEADS),
        in_specs=[slab(SLAB_QC), slab(SLAB_KC), slab(SLAB_VC), vec, vec],
        out_specs=pl.BlockSpec((1, per, LANES), lambda b, h: (h, b, 0)),
        out_shape=jax.ShapeDtypeStruct((DIL_HEADS, batch * seq, LANES), BF16),
        scratch_shapes=[
            row(), row(), row(),
            row(), row(), row(),
            pltpu.VMEM((seq // DIL_BLOCK, DIL_BLOCK, 2 * DIL_BLOCK), F32),
            pltpu.VMEM((seq // DIL_BLOCK, DIL_BLOCK, LANES), F32),
            row(), row(), row(),
            row(), row(), row(),
        ],
        compiler_params=_cparams(("arbitrary", "arbitrary")),
        name="dil_mixer",
    )(proj, proj, proj, q_norm.reshape(1, LANES), k_norm.reshape(1, LANES))


def _route_tile(sc, bias, run):
    ne, tm = sc.shape
    epg = EXPERTS_PER_GROUP
    sel = sc + bias
    jrow = lax.broadcasted_iota(jnp.int32, (epg, tm), 0)
    gs, i1s, i2s = [], [], []
    for g in range(N_GROUPS):
        sg = sel[g * epg:(g + 1) * epg, :]
        m1 = jnp.max(sg, axis=0, keepdims=True)
        i1 = jnp.min(jnp.where(sg == m1, jrow, epg), axis=0, keepdims=True)
        rest = jnp.where(jrow == i1, -jnp.inf, sg)
        m2 = jnp.max(rest, axis=0, keepdims=True)
        i2 = jnp.min(jnp.where(rest == m2, jrow, epg), axis=0, keepdims=True)
        gs.append(m1 + m2)
        i1s.append(i1)
        i2s.append(i2)
    gmax = functools.reduce(jnp.maximum, gs)
    gidx = jnp.full((1, tm), N_GROUPS - 1, jnp.int32)
    for g in reversed(range(N_GROUPS - 1)):
        gidx = jnp.where(gs[g] == gmax, g, gidx)
    loc1 = jnp.zeros((1, tm), jnp.int32)
    loc2 = jnp.zeros((1, tm), jnp.int32)
    for g in range(N_GROUPS):
        loc1 = jnp.where(gidx == g, i1s[g], loc1)
        loc2 = jnp.where(gidx == g, i2s[g], loc2)
    e1 = gidx * epg + loc1
    e2 = gidx * epg + loc2
    erow = lax.broadcasted_iota(jnp.int32, (ne, tm), 0)
    oh1 = erow == e1
    oh2 = erow == e2
    s1 = jnp.sum(jnp.where(oh1, sc, 0.0), axis=0, keepdims=True)
    s2 = jnp.sum(jnp.where(oh2, sc, 0.0), axis=0, keepdims=True)
    den = s1 + s2
    tr = lax.broadcasted_iota(jnp.int32, (tm, tm), 0)
    tc = lax.broadcasted_iota(jnp.int32, (tm, tm), 1)
    before = jnp.where(tr < tc, 1.0, 0.0).astype(BF16)
    f1 = jnp.where(oh1, 1.0, 0.0)
    f2 = jnp.where(oh2, 1.0, 0.0)
    pre1 = _dot(f1.astype(BF16), before)
    pre2 = _dot(f2.astype(BF16), before)
    tot1 = jnp.sum(f1, axis=1, keepdims=True)
    tot2 = jnp.sum(f2, axis=1, keepdims=True)
    r1 = jnp.sum(jnp.where(oh1, pre1 + run, 0.0), axis=0, keepdims=True)
    r2 = jnp.sum(jnp.where(oh2, pre2 + (run + tot1), 0.0), axis=0, keepdims=True)
    return e1, e2, s1 / den, s2 / den, r1, r2, run + tot1 + tot2


def _outproj_kernel(oa_ref, ob_ref, oc_ref, w_ref, x_ref, g_ref, nw_ref, sc_ref, sh_ref, wrt_ref, rb_ref,
                    xo_ref, hf_ref, ri_ref, rw_ref, cnt_ref, mix_s, run_s):
    @pl.when(pl.program_id(0) == 0)
    def _():
        run_s[...] = jnp.zeros(run_s.shape, F32)

    off = 0
    for ref, heads in ((oa_ref, GDN_HEADS), (ob_ref, DIFF_HEADS), (oc_ref, DIL_HEADS)):
        for k in range(heads):
            mix_s[:, off:off + LANES] = ref[k]
            off += LANES
    xn = x_ref[...] + g_ref[0] * _dot(mix_s[...], w_ref[...])
    xo_ref[...] = xn
    hb = _modulated_norm(xn, nw_ref[...], sc_ref[0], sh_ref[0]).astype(BF16)
    hf_ref[...] = _pack_bf16_pairs(hb)
    scores_t = jax.nn.sigmoid(_dot_nt(wrt_ref[...], hb))
    e1, e2, w1, w2, r1, r2, run = _route_tile(scores_t, rb_ref[...], run_s[...])
    run_s[...] = run
    cnt_ref[...] = jnp.broadcast_to(run, cnt_ref.shape)
    tm = e1.shape[1]
    ri_ref[...] = jnp.concatenate([e1, e2, r1.astype(jnp.int32), r2.astype(jnp.int32),
                                   jnp.zeros((4, tm), jnp.int32)], axis=0)
    rw_ref[...] = jnp.transpose(jnp.concatenate([w1, w2, jnp.zeros((6, tm), F32)], axis=0))


def _outproj(o_a, o_b, o_c, w_out, x2, g_a, nw, sc, sh, w_router_t, router_bias, seq):
    t, d = x2.shape
    tm = 256
    per_b = seq // tm
    ne = w_router_t.shape[0]
    row = pl.BlockSpec((tm, d), lambda i: (i, 0))
    bvec = pl.BlockSpec((1, 1, d), lambda i: (i // per_b, 0, 0))
    return pl.pallas_call(
        _outproj_kernel,
        grid=(t // tm,),
        in_specs=[
            pl.BlockSpec((GDN_HEADS, tm, LANES), lambda i: (0, i, 0)),
            pl.BlockSpec((DIFF_HEADS, tm, LANES), lambda i: (0, i, 0)),
            pl.BlockSpec((DIL_HEADS, tm, LANES), lambda i: (0, i, 0)),
            pl.BlockSpec((d, d), lambda i: (0, 0)),
            row, bvec,
            pl.BlockSpec((1, d), lambda i: (0, 0)),
            bvec, bvec,
            pl.BlockSpec((ne, d), lambda i: (0, 0)),
            pl.BlockSpec((ne, 1), lambda i: (0, 0)),
        ],
        out_specs=[row,
                   pl.BlockSpec((tm, d // 2), lambda i: (i, 0)),
                   pl.BlockSpec((8, tm), lambda i: (0, i)),
                   pl.BlockSpec((tm, 8), lambda i: (i, 0)),
                   pl.BlockSpec((ne, LANES), lambda i: (0, 0))],
        out_shape=[jax.ShapeDtypeStruct((t, d), F32), jax.ShapeDtypeStruct((t, d // 2), jnp.int32),
                   jax.ShapeDtypeStruct((8, t), jnp.int32), jax.ShapeDtypeStruct((t, 8), F32),
                   jax.ShapeDtypeStruct((ne, LANES), F32)],
        scratch_shapes=[pltpu.VMEM((tm, d), BF16), pltpu.VMEM((ne, 1), F32)],
        compiler_params=_cparams(("arbitrary",)),
        name="outproj",
    )(o_a, o_b, o_c, w_out, x2, g_a, nw, sc, sh, w_router_t, router_bias.reshape(ne, 1).astype(F32))


def _moe_kernel(be_ref, nv_ref, first_ref, wpar_ref, wnext_ref, tok_ref, tok_next_ref, dst_spare_ref,
                dst_prev_ref, dst_ref, hf_hbm, wg_hbm, wu_hbm, wd_hbm, y_hbm,
                xbuf, ybuf, wg_f, wu_f, wd_f, wg_s, wu_s, wd_s, gsem, ssem, wsem, *, layer):
    i = pl.program_id(0)
    nv = nv_ref[0]
    slot = i % 2
    rows = MOE_ROWS

    def gather_copy(tok_r, r, s):
        return pltpu.make_async_copy(hf_hbm.at[pl.ds(tok_r[0, 0, r], 1)], xbuf.at[s, pl.ds(r, 1)], gsem.at[s])

    def scatter_copy(dst_r, r, s):
        return pltpu.make_async_copy(ybuf.at[s, pl.ds(r, 1)], y_hbm.at[pl.ds(dst_r[0, 0, r], 1)], ssem.at[s])

    def weight_copies(e, par):
        return (pltpu.make_async_copy(wg_hbm.at[layer, e], wg_f.at[par], wsem.at[par]),
                pltpu.make_async_copy(wu_hbm.at[layer, e], wu_f.at[par], wsem.at[par]),
                pltpu.make_async_copy(wd_hbm.at[layer, e], wd_f.at[par], wsem.at[par]))

    @pl.when(i == 0)
    def _():
        ybuf[...] = jnp.zeros(ybuf.shape, jnp.int32)
        for r in range(rows):
            gather_copy(tok_ref, r, 0).start(priority=r % 2)
        for r in range(rows):
            scatter_copy(dst_spare_ref, r, 0).start(priority=r % 2)
        for c in weight_copies(be_ref[0], wpar_ref[0]):
            c.start()

    @pl.when(first_ref[i] == 1)
    def _():
        par = wpar_ref[i]
        for c in weight_copies(be_ref[i], par):
            c.wait()

        @pl.when(wnext_ref[i] >= 0)
        def _():
            for c in weight_copies(wnext_ref[i], 1 - par):
                c.start()

        wg_s[...] = wg_f[par].astype(BF16)
        wu_s[...] = wu_f[par].astype(BF16)
        wd_s[...] = wd_f[par].astype(BF16)

    @pl.when(i < nv)
    def _():
        for r in range(rows):
            gather_copy(tok_ref, r, slot).wait()
        x = _unpack_bf16_pairs(xbuf[slot]).astype(BF16)
        for r in range(rows):
            gather_copy(tok_next_ref, r, 1 - slot).start(priority=r % 2)
        for r in range(rows):
            scatter_copy(dst_prev_ref, r, 1 - slot).start(priority=r % 2)
        a = _dot(x, wg_s[...])
        u = _dot(x, wu_s[...])
        y = _pack_bf16_pairs(_dot((_silu(a) * u).astype(BF16), wd_s[...]).astype(BF16))

        for r in range(rows):
            scatter_copy(dst_ref, r, slot).wait()
        ybuf[slot] = y

    @pl.when(i == nv - 1)
    def _():
        for r in range(rows):
            gather_copy(tok_next_ref, r, 1 - slot).wait()
            scatter_copy(dst_prev_ref, r, 1 - slot).wait()
        for r in range(rows):
            scatter_copy(dst_ref, r, slot).start(priority=r % 2)
        for r in range(rows):
            scatter_copy(dst_ref, r, slot).wait()


def _moe(hf, layout, w_gate, w_up, w_down, layer):
    block_e, n_valid, first, wpar, wnext, row_tok, row_dst = layout
    t = hf.shape[0]
    d = w_gate.shape[2]
    f = w_gate.shape[3]
    rows = MOE_ROWS
    nb = block_e.shape[0]
    tok3 = row_tok.reshape(nb, 1, rows)
    spare = (TOP_K * t + jnp.arange(2 * rows, dtype=jnp.int32)).reshape(2, 1, rows)
    dst3 = jnp.concatenate([spare, row_dst.reshape(nb, 1, rows)], axis=0)
    smem_blk = lambda fn: pl.BlockSpec((1, 1, rows), fn, memory_space=pltpu.SMEM)
    hbm = pl.BlockSpec(memory_space=pl.ANY)
    grid_spec = pltpu.PrefetchScalarGridSpec(
        num_scalar_prefetch=5,
        grid=(nb,),
        in_specs=[
            smem_blk(lambda i, *_: (i, 0, 0)),
            smem_blk(lambda i, *_: (jnp.minimum(i + 1, nb - 1), 0, 0)),
            smem_blk(lambda i, *_: (0, 0, 0)),
            smem_blk(lambda i, *_: (i + 1, 0, 0)),
            smem_blk(lambda i, *_: (i + 2, 0, 0)),
            hbm, hbm, hbm, hbm,
        ],
        out_specs=hbm,
        scratch_shapes=[
            pltpu.VMEM((2, rows, d // 2), jnp.int32),
            pltpu.VMEM((2, rows, d // 2), jnp.int32),
            pltpu.VMEM((2, d, f), F32), pltpu.VMEM((2, d, f), F32), pltpu.VMEM((2, f, d), F32),
            pltpu.VMEM((d, f), BF16), pltpu.VMEM((d, f), BF16), pltpu.VMEM((f, d), BF16),
            pltpu.SemaphoreType.DMA((2,)),
            pltpu.SemaphoreType.DMA((2,)),
            pltpu.SemaphoreType.DMA((2,)),
        ],
    )
    return pl.pallas_call(
        functools.partial(_moe_kernel, layer=layer),
        grid_spec=grid_spec,
        out_shape=jax.ShapeDtypeStruct((TOP_K * t + 2 * rows, d // 2), jnp.int32),
        compiler_params=_cparams(("arbitrary",), has_side_effects=True, disable_bounds_checks=True),
        name="moe_experts",
    )(block_e, n_valid, first, wpar, wnext, tok3, tok3, dst3, dst3, dst3, hf, w_gate, w_up, w_down)


def _block_layout(route_i, counts_f):
    t = route_i.shape[1]
    rows = MOE_ROWS
    n_assign = TOP_K * t
    counts = counts_f[:, 0].astype(jnp.int32)
    blocks_per_e = (counts + rows - 1) // rows
    blk_end = jnp.cumsum(blocks_per_e)
    blk_start = blk_end - blocks_per_e
    flat_e = route_i[0:TOP_K].reshape(n_assign)
    dest = blk_start[flat_e] * rows + route_i[TOP_K:2 * TOP_K].reshape(n_assign)
    nb = n_assign // rows + N_EXPERTS
    n_rows = nb * rows
    assign = jnp.arange(n_assign, dtype=jnp.int32)
    ridx = jnp.arange(n_rows, dtype=jnp.int32)
    spare = n_assign + ((ridx // rows) % 2) * rows + ridx % rows
    row_dst = spare.at[dest].set(assign, unique_indices=True, mode='promise_in_bounds')
    row_tok = jnp.where(row_dst < n_assign, row_dst % t, 0)
    n_valid = blk_end[-1].astype(jnp.int32)
    bidx = jnp.arange(nb, dtype=jnp.int32)
    block_e = jnp.sum((blk_end[None, :] <= bidx[:, None]).astype(jnp.int32), axis=1)
    block_e = jnp.minimum(block_e, N_EXPERTS - 1)
    last_e = block_e[jnp.maximum(n_valid - 1, 0)]
    block_e = jnp.where(bidx < n_valid, block_e, last_e)
    prev_e = jnp.concatenate([jnp.full((1,), -1, jnp.int32), block_e[:-1]])
    first = ((bidx < n_valid) & (block_e != prev_e)).astype(jnp.int32)
    wpar = (jnp.cumsum(first) + 1) % 2
    first_at = jnp.where(first == 1, bidx, nb)
    later = jnp.concatenate([lax.cummin(first_at[::-1])[::-1][1:], jnp.full((1,), nb, jnp.int32)])
    wnext = jnp.where(later < nb, block_e[jnp.minimum(later, nb - 1)], -1).astype(jnp.int32)
    return block_e, n_valid.reshape(1), first, wpar.astype(jnp.int32), wnext, row_tok, row_dst


def _combine_kernel(x_ref, y0_ref, y1_ref, rw_ref, g_ref, o_ref):
    w = rw_ref[...]
    y0 = _unpack_bf16_pairs(y0_ref[...])
    y1 = _unpack_bf16_pairs(y1_ref[...])
    o_ref[...] = x_ref[...] + g_ref[0] * (y0 * w[:, 0:1] + y1 * w[:, 1:2])


def _combine(x2, y2, route_w, g_f, seq):
    t, d = x2.shape
    tm = 256
    per_b = seq // tm
    n_blk = t // tm
    return pl.pallas_call(
        _combine_kernel,
        grid=(n_blk,),
        in_specs=[pl.BlockSpec((tm, d), lambda i: (i, 0)),
                  pl.BlockSpec((tm, d // 2), lambda i: (i, 0)),
                  pl.BlockSpec((tm, d // 2), lambda i: (n_blk + i, 0)),
                  pl.BlockSpec((tm, 8), lambda i: (i, 0)),
                  pl.BlockSpec((1, 1, d), lambda i: (i // per_b, 0, 0))],
        out_specs=pl.BlockSpec((tm, d), lambda i: (i, 0)),
        out_shape=jax.ShapeDtypeStruct((t, d), F32),
        compiler_params=_cparams(("arbitrary",)),
        name="moe_combine",
    )(x2, y2, y2, route_w, g_f)


def kernel(x, c, norm_mix, norm_ffn, w_ada, b_ada, w_in, conv_w, a_log, dt_bias, gdn_norm, diff_q_norm, diff_k_norm, lam_q1, lam_k1, lam_q2, lam_k2, diff_subln, dil_q_norm, dil_k_norm, w_out, w_router, router_bias, w_gate, w_up, w_down):
    batch, seq, d = x.shape
    depth = w_ada.shape[0]
    t = batch * seq
    x2 = x.reshape(t, d)
    mod = _ada(c, w_ada, b_ada).reshape(depth, batch, N_MOD, 1, d)
    ba_lo = 3 * GDN_HEADS * LANES + GDN_HEADS * LANES
    ba_hi = ba_lo + 2 * GDN_HEADS
    w_router_t = w_router.T.astype(BF16)
    for l in range(depth):
        sh_a, sc_a, g_a, sh_f, sc_f, g_f = (mod[l, :, k] for k in range(N_MOD))
        w_main = jnp.concatenate([w_in[l, :, :ba_lo], w_in[l, :, ba_hi:]], axis=1).astype(BF16)
        w_ba = jnp.pad(w_in[l, :, ba_lo:ba_hi], ((0, 0), (0, LANES - 2 * GDN_HEADS))).astype(BF16)
        proj, ba = _inproj(x2, norm_mix[l].reshape(1, d), sc_a, sh_a, w_main, w_ba, seq)
        o_a = _gdn(proj, ba, conv_w[l], a_log[l], dt_bias[l], gdn_norm[l], batch, seq)
        lam_init = 0.8 - 0.6 * math.exp(-0.3 * l)
        lam_vecs = jnp.stack([lam_q1[l], lam_k1[l], lam_q2[l], lam_k2[l]]).astype(F32)
        o_b = _diff(proj, diff_q_norm[l], diff_k_norm[l], diff_subln[l], lam_vecs, lam_init, batch, seq)
        o_c = _dil(proj, dil_q_norm[l], dil_k_norm[l], batch, seq)
        x2, hf, route_i, route_w, counts = _outproj(o_a, o_b, o_c, w_out[l].astype(BF16), x2, g_a,
                                                    norm_ffn[l].reshape(1, d), sc_f, sh_f, w_router_t, router_bias, seq)
        y2 = _moe(hf, _block_layout(route_i, counts), w_gate, w_up, w_down, l)
        x2 = _combine(x2, y2, route_w, g_f, seq)
    return x2.reshape(batch, seq, d)
```

```python
import functools
import math

import jax
import jax.numpy as jnp
from jax import lax
from jax.experimental import pallas as pl
from jax.experimental.pallas import tpu as pltpu

F32 = jnp.float32
BF16 = jnp.bfloat16

LANES = 128
EPS = 1e-6
N_MOD = 6

GDN_HEADS = 6
GDN_CONV = 4
GDN_CHUNK = 256
GDN_LOCKSTEP = 4
DIFF_HEADS = 4
DIFF_QK_DIM = 64
DIL_HEADS = 6
DILATED_PAIRS = ((128, 1), (512, 4), (2048, 16))
DIL_BLOCK = 128
DIL_SPLIT = 4

N_EXPERTS = 64
N_GROUPS = 8
EXPERTS_PER_GROUP = N_EXPERTS // N_GROUPS
TOP_K = 2
MOE_ROWS = 256

SLAB_QA, SLAB_KA, SLAB_VA, SLAB_ZA = 0, 6, 12, 18
SLAB_QB, SLAB_KB, SLAB_VB = 24, 28, 32
SLAB_QC, SLAB_KC, SLAB_VC = 36, 42, 48
N_SLABS = 54

VMEM_LIMIT = 56 * 1024 * 1024


def _cparams(sem, vmem=VMEM_LIMIT, **kw):
    return pltpu.CompilerParams(dimension_semantics=sem, vmem_limit_bytes=vmem, **kw)


def _silu(v):
    return v * jax.nn.sigmoid(v)


def _dot(a, b):
    return jnp.dot(a, b, preferred_element_type=F32)


def _dot_nt(a, b):
    return lax.dot_general(a, b, (((1,), (1,)), ((), ())), preferred_element_type=F32)


def _dot_tn(a, b):
    return lax.dot_general(a, b, (((0,), (0,)), ((), ())), preferred_element_type=F32)


def _pack_bf16_pairs(v):
    half = v.shape[1] // 2
    lo = lax.bitcast_convert_type(v[:, :half].astype(F32), jnp.int32)
    hi = lax.bitcast_convert_type(v[:, half:].astype(F32), jnp.int32)
    return lax.shift_right_logical(lo, 16) | (hi & -65536)


def _unpack_bf16_pairs(w):
    return jnp.concatenate([lax.bitcast_convert_type(lax.shift_left(w, 16), F32),
                            lax.bitcast_convert_type(w & -65536, F32)], axis=1)


def _ada_kernel(c_ref, w_ref, b_ref, o_ref):
    cact = _silu(c_ref[...]).astype(BF16)
    o_ref[0] = _dot(cact, w_ref[0].astype(BF16)) + b_ref[0]


def _ada(c, w_ada, b_ada):
    depth, d, n = w_ada.shape
    b = c.shape[0]
    tn = 1024
    return pl.pallas_call(
        _ada_kernel,
        grid=(depth, n // tn),
        in_specs=[
            pl.BlockSpec((b, d), lambda l, j: (0, 0)),
            pl.BlockSpec((1, d, tn), lambda l, j: (l, 0, j)),
            pl.BlockSpec((1, 1, tn), lambda l, j: (l, 0, j)),
        ],
        out_specs=pl.BlockSpec((1, b, tn), lambda l, j: (l, 0, j)),
        out_shape=jax.ShapeDtypeStruct((depth, b, n), F32),
        compiler_params=_cparams(("arbitrary", "arbitrary")),
        name="ada_mod",
    )(c, w_ada, b_ada.reshape(depth, 1, n))


def _modulated_norm(x, nw, sc, sh):
    ms = jnp.mean(x * x, axis=-1, keepdims=True)
    return (x * lax.rsqrt(ms + EPS) * nw) * (1.0 + sc) + sh


def _inproj_kernel(x_ref, nw_ref, sc_ref, sh_ref, w_ref, wba_ref, o_ref, ba_ref, h_scr, *, n_sub):
    @pl.when(pl.program_id(1) == 0)
    def _():
        h = _modulated_norm(x_ref[...], nw_ref[...], sc_ref[0], sh_ref[0]).astype(BF16)
        h_scr[...] = h
        ba_ref[...] = _dot(h, wba_ref[...])

    acc = _dot(h_scr[...], w_ref[...])
    for k in range(n_sub):
        o_ref[k] = acc[:, k * LANES:(k + 1) * LANES].astype(BF16)


def _inproj(x2, nw, sc, sh, w_main, w_ba, seq):
    t, d = x2.shape
    n = w_main.shape[1]
    tm = min(1024, seq)
    tn = 768
    n_sub = tn // LANES
    per_b = seq // tm
    return pl.pallas_call(
        functools.partial(_inproj_kernel, n_sub=n_sub),
        grid=(t // tm, n // tn),
        in_specs=[
            pl.BlockSpec((tm, d), lambda i, j: (i, 0)),
            pl.BlockSpec((1, d), lambda i, j: (0, 0)),
            pl.BlockSpec((1, 1, d), lambda i, j: (i // per_b, 0, 0)),
            pl.BlockSpec((1, 1, d), lambda i, j: (i // per_b, 0, 0)),
            pl.BlockSpec((d, tn), lambda i, j: (0, j)),
            pl.BlockSpec((d, LANES), lambda i, j: (0, 0)),
        ],
        out_specs=[
            pl.BlockSpec((n_sub, tm, LANES), lambda i, j: (j, i, 0)),
            pl.BlockSpec((tm, LANES), lambda i, j: (i, 0)),
        ],
        out_shape=[
            jax.ShapeDtypeStruct((n // LANES, t, LANES), BF16),
            jax.ShapeDtypeStruct((t, LANES), F32),
        ],
        scratch_shapes=[pltpu.VMEM((tm, d), BF16)],
        compiler_params=_cparams(("arbitrary", "arbitrary")),
        name="inproj",
    )(x2, nw, sc, sh, w_main, w_ba)


def _unit_lower_inverse(lms, xr):
    c = lms[0].shape[0]

    def mm(a, b):
        return [_dot(ai.astype(BF16), bi.astype(BF16)) for ai, bi in zip(a, b)]

    d1 = [jnp.where(xr < 16, lm, 0.0) for lm in lms]
    d2 = mm(d1, d1)
    d4 = mm(d2, d2)
    d8 = mm(d4, d4)
    p = [b - a - t for a, b, t in zip(d1, d2, mm(d1, d2))]
    p = [pi + di + t for pi, di, t in zip(p, d4, mm(p, d4))]
    p = [pi + di + t for pi, di, t in zip(p, d8, mm(p, d8))]
    eye = jnp.where(xr == 0, 1.0, 0.0)
    x = [eye + pi for pi in p]
    blk = 32
    while blk <= c:
        e = [jnp.where((xr < blk) & (xr >= blk // 2), lm, 0.0) for lm in lms]
        x = [xi - t for xi, t in zip(x, mm(x, mm(e, x)))]
        blk *= 2
    return x


def _gdn_kernel(q_ref, k_ref, v_ref, z_ref, ba_ref, cwq_ref, cwk_ref, cwv_ref, al_ref, dtb_ref, nw_ref,
                o_ref, xq, xk, xv, u_s, w_s, a_s, qd_s, kd_s, cd_s, *, seq, chunk):
    c = chunk
    n_chunks = seq // c
    head = pl.program_id(1)
    dk = LANES

    lane1 = lax.broadcasted_iota(jnp.int32, (1, LANES), 1)
    a_exp = jnp.exp(jnp.sum(jnp.where(lane1 == head, al_ref[...], 0.0), axis=-1, keepdims=True))
    dtb = jnp.sum(jnp.where(lane1 == head, dtb_ref[...], 0.0), axis=-1, keepdims=True)

    zeros8 = jnp.zeros((8, LANES), F32)
    for src, dst in ((q_ref, xq), (k_ref, xk), (v_ref, xv)):
        dst[0:8, :] = zeros8

        def stage(i, carry, src=src, dst=dst):
            r0 = pl.multiple_of(i * c, c)
            dst[pl.ds(r0 + 8, c), :] = src[0, pl.ds(r0, c), :].astype(F32)
            return carry

        lax.fori_loop(0, n_chunks, stage, 0)

    row = lax.broadcasted_iota(jnp.int32, (c, c), 0)
    col = lax.broadcasted_iota(jnp.int32, (c, c), 1)
    causal = row >= col
    strict = row > col
    xr = row ^ col
    tril = jnp.where(causal, 1.0, 0.0).astype(BF16)
    lane_c = lax.broadcasted_iota(jnp.int32, (c, LANES), 1)

    def conv(xs, cw_ref, r0):
        acc = cw_ref[0, GDN_CONV - 1:GDN_CONV, :] * xs[pl.ds(r0 + 8, c), :]
        for back in range(1, GDN_CONV):
            tap = GDN_CONV - 1 - back
            acc = acc + cw_ref[0, tap:tap + 1, :] * xs[pl.ds(r0 + 8 - back, c), :]
        return _silu(acc)

    def l2n(t):
        return t * lax.rsqrt(jnp.sum(t * t, axis=-1, keepdims=True) + EPS)

    def log_decay(r0):
        bat = ba_ref[pl.ds(r0, c), :]
        bcol = jnp.sum(jnp.where(lane_c == head, bat, 0.0), axis=-1, keepdims=True)
        acol = jnp.sum(jnp.where(lane_c == head + GDN_HEADS, bat, 0.0), axis=-1, keepdims=True)
        xsp = acol + dtb
        softplus = jnp.maximum(xsp, 0.0) + jnp.log1p(jnp.exp(-jnp.abs(xsp)))
        return jax.nn.sigmoid(bcol), jnp.broadcast_to(-a_exp * softplus, (c, LANES))

    def intra(step, carry):
        par = range(GDN_LOCKSTEP)
        ns = [step * GDN_LOCKSTEP + j for j in par]
        r0s = [pl.multiple_of(n * c, c) for n in ns]
        qn = [l2n(conv(xq, cwq_ref, r0)) * (dk ** -0.5) for r0 in r0s]
        kn = [l2n(conv(xk, cwk_ref, r0)) for r0 in r0s]
        vc = [conv(xv, cwv_ref, r0) for r0 in r0s]
        bg = [log_decay(r0) for r0 in r0s]
        beta = [t[0] for t in bg]
        g_hi = [t[1].astype(BF16) for t in bg]
        g_lo = [(t[1] - h.astype(F32)).astype(BF16) for t, h in zip(bg, g_hi)]
        gc = [_dot(tril, h) + _dot(tril, lo_) for h, lo_ in zip(g_hi, g_lo)]
        decay = []
        for gci in gc:
            gc_row = jnp.transpose(gci)[0:1, :]
            diff = jnp.concatenate([gci] * (c // LANES), axis=1) - gc_row
            decay.append(jnp.where(causal, jnp.exp(jnp.where(causal, diff, 0.0)), 0.0))
        kb = [k * b for k, b in zip(kn, beta)]
        knb = [k.astype(BF16) for k in kn]
        lm = [jnp.where(strict, _dot_nt(kbi.astype(BF16), ki) * dc, 0.0) for kbi, ki, dc in zip(kb, knb, decay)]
        attn = [jnp.where(causal, _dot_nt(q.astype(BF16), ki) * dc, 0.0) for q, ki, dc in zip(qn, knb, decay)]
        tinv = _unit_lower_inverse(lm, xr)
        egc = [jnp.exp(g) for g in gc]
        sol = [_dot(ti.astype(BF16), jnp.concatenate([v * b, kbi * e], axis=1).astype(BF16))
               for ti, v, b, kbi, e in zip(tinv, vc, beta, kb, egc)]
        for j in par:
            rows = pl.ds(r0s[j], c)
            u_s[rows, :] = sol[j][:, :LANES]
            w_s[rows, :] = sol[j][:, LANES:].astype(BF16)
            a_s[rows, :] = attn[j].astype(BF16)
            qd_s[rows, :] = (qn[j] * egc[j]).astype(BF16)
            g_last = gc[j][c - 1:c, :]
            kd_s[rows, :] = (kn[j] * jnp.exp(g_last - gc[j])).astype(BF16)
            cd_s[pl.ds(pl.multiple_of(ns[j] * 8, 8), 8), :] = jnp.broadcast_to(jnp.exp(g_last), (8, LANES))
        return carry

    lax.fori_loop(0, n_chunks // GDN_LOCKSTEP, intra, 0)

    def inter(n, state):
        r0 = pl.multiple_of(n * c, c)
        rows = pl.ds(r0, c)
        sb = state.astype(BF16)
        v_new = u_s[rows, :] - _dot(w_s[rows, :], sb)
        vb = v_new.astype(BF16)
        o = _dot(qd_s[rows, :], sb) + _dot(a_s[rows, :], vb)
        cd = cd_s[pl.ds(pl.multiple_of(n * 8, 8), 1), :]
        new_state = state * cd + _dot_tn(kd_s[rows, :], vb)
        on = o * lax.rsqrt(jnp.mean(o * o, axis=-1, keepdims=True) + EPS) * nw_ref[...]
        z = z_ref[0, rows, :].astype(F32)
        o_ref[0, rows, :] = (on * _silu(z)).astype(BF16)
        return new_state

    lax.fori_loop(0, n_chunks, inter, jnp.zeros((dk, LANES), F32), unroll=2)


def _gdn(proj, ba, conv_w, a_log, dt_bias, gdn_norm, batch, seq):
    c = GDN_CHUNK
    per = seq
    cw = conv_w.reshape(GDN_CONV, 3 * GDN_HEADS, LANES).transpose(1, 0, 2)
    pad = lambda v: jnp.pad(v, (0, LANES - v.shape[0])).reshape(1, LANES)

    def slab(base):
        return pl.BlockSpec((1, per, LANES), lambda b, h, base=base: (base + h, b, 0))

    def cwspec(base):
        return pl.BlockSpec((1, GDN_CONV, LANES), lambda b, h, base=base: (base + h, 0, 0))

    vec = pl.BlockSpec((1, LANES), lambda b, h: (0, 0))
    return pl.pallas_call(
        functools.partial(_gdn_kernel, seq=seq, chunk=c),
        grid=(batch, GDN_HEADS),
        in_specs=[slab(SLAB_QA), slab(SLAB_KA), slab(SLAB_VA), slab(SLAB_ZA),
                  pl.BlockSpec((per, LANES), lambda b, h: (b, 0)),
                  cwspec(0), cwspec(GDN_HEADS), cwspec(2 * GDN_HEADS), vec, vec, vec],
        out_specs=pl.BlockSpec((1, per, LANES), lambda b, h: (h, b, 0)),
        out_shape=jax.ShapeDtypeStruct((GDN_HEADS, batch * seq, LANES), BF16),
        scratch_shapes=[
            pltpu.VMEM((seq + 8, LANES), F32), pltpu.VMEM((seq + 8, LANES), F32),
            pltpu.VMEM((seq + 8, LANES), F32),
            pltpu.VMEM((seq, LANES), F32),
            pltpu.VMEM((seq, LANES), BF16),
            pltpu.VMEM((seq, c), BF16),
            pltpu.VMEM((seq, LANES), BF16),
            pltpu.VMEM((seq, LANES), BF16),
            pltpu.VMEM((seq // c * 8, LANES), F32),
        ],
        compiler_params=_cparams(("arbitrary", "arbitrary")),
        name="gdn_mixer",
    )(proj, proj, proj, proj, ba, cw, cw, cw, pad(a_log), pad(dt_bias), gdn_norm.reshape(1, LANES))


def _diff_kernel(q_ref, k_ref, v_ref, qw_ref, kw_ref, sw_ref, lam_ref, o_ref,
                 q1_s, q2_s, kn_s, ve_s, sc1, sc2, acc1, acc2, m1, m2, *, seq, tq, tk, lam_init):
    dh = DIFF_QK_DIM
    n_blk = seq // tq
    per_q = tq // tk
    lane = lax.broadcasted_iota(jnp.int32, (1, LANES), 1)
    lo = lane < dh

    lam = (jnp.exp(jnp.sum(lam_ref[0:1, :] * lam_ref[1:2, :], axis=-1, keepdims=True))
           - jnp.exp(jnp.sum(lam_ref[2:3, :] * lam_ref[3:4, :], axis=-1, keepdims=True)) + lam_init)

    def halfnorm(t, w):
        sq = t * t
        s_lo = jnp.sum(jnp.where(lo, sq, 0.0), axis=-1, keepdims=True)
        s_hi = jnp.sum(jnp.where(lo, 0.0, sq), axis=-1, keepdims=True)
        r = jnp.where(lo, lax.rsqrt(s_lo / dh + EPS), lax.rsqrt(s_hi / dh + EPS))
        return t * r * w

    ones_blk = jnp.ones((tk, LANES), BF16)

    def prep(i, carry):
        rows = pl.ds(pl.multiple_of(i * tk, tk), tk)
        kn_s[rows, :] = halfnorm(k_ref[0, rows, :].astype(F32), kw_ref[...]).astype(BF16)
        qn = halfnorm(q_ref[0, rows, :].astype(F32), qw_ref[...]) * (dh ** -0.5)
        q1_s[rows, :] = jnp.where(lo, qn, 0.0).astype(BF16)
        q2_s[rows, :] = jnp.where(lo, 0.0, qn).astype(BF16)
        ve_s[rows, 0:LANES] = v_ref[0, rows, :]
        ve_s[rows, LANES:2 * LANES] = ones_blk
        return carry

    lax.fori_loop(0, seq // tk, prep, 0)

    row = lax.broadcasted_iota(jnp.int32, (tq, tk), 0)
    col = lax.broadcasted_iota(jnp.int32, (tq, tk), 1)

    n_parts = tk // LANES

    def q_block(qi, carry):
        qrows = pl.ds(pl.multiple_of(qi * tq, tq), tq)
        maps = ((q1_s[qrows, :], sc1, m1, acc1), (q2_s[qrows, :], sc2, m2, acc2))
        for _, _, m_r, a_r in maps:
            m_r[...] = jnp.full((tq, LANES), -jnp.inf, F32)
            a_r[...] = jnp.zeros((tq, 2 * LANES), F32)

        def score_step(kj, masked):
            kb = kn_s[pl.ds(pl.multiple_of(kj * tk, tk), tk), :]
            for qq, s_c, m_r, _ in maps:
                s = _dot_nt(qq, kb)
                if masked:
                    s = jnp.where(row + qi * tq >= col + kj * tk, s, -jnp.inf)
                s_c[kj] = s
                smax = m_r[...]
                for part in range(n_parts):
                    smax = jnp.maximum(smax, s[:, part * LANES:(part + 1) * LANES])
                m_r[...] = smax

        def off_diag(kq, c2):
            for part in range(per_q):
                score_step(kq * per_q + part, False)
            return c2

        lax.fori_loop(0, qi, off_diag, 0)
        for dblk in range(per_q):
            score_step(qi * per_q + dblk, True)
        for _, _, m_r, _ in maps:
            m_r[...] = jnp.broadcast_to(jnp.max(m_r[...], axis=-1, keepdims=True), (tq, LANES))

        def value_step(kq, c2):
            for _, s_c, m_r, a_r in maps:
                mrep = m_r[...]
                upd = a_r[...]
                for sub in range(per_q):
                    kj = kq * per_q + sub
                    s = s_c[kj]
                    p = jnp.concatenate(
                        [jnp.exp(s[:, part * LANES:(part + 1) * LANES] - mrep) for part in range(n_parts)], axis=1)
                    upd = upd + _dot(p.astype(BF16), ve_s[pl.ds(pl.multiple_of(kj * tk, tk), tk), :])
                a_r[...] = upd
            return c2

        lax.fori_loop(0, qi + 1, value_step, 0)
        o = (acc1[:, 0:LANES] / acc1[:, LANES:2 * LANES]
             - lam * (acc2[:, 0:LANES] / acc2[:, LANES:2 * LANES]))
        o = o * lax.rsqrt(jnp.mean(o * o, axis=-1, keepdims=True) + EPS) * sw_ref[...] * (1.0 - lam_init)
        o_ref[0, qrows, :] = o.astype(BF16)
        return carry

    lax.fori_loop(0, n_blk, q_block, 0)


def _diff(proj, q_norm, k_norm, subln, lam_vecs, lam_init, batch, seq):
    tq = 512
    tk = 256
    per = seq

    def slab(base):
        return pl.BlockSpec((1, per, LANES), lambda b, h, base=base: (base + h, b, 0))

    vec = pl.BlockSpec((1, LANES), lambda b, h: (0, 0))
    tile2 = lambda w: jnp.concatenate([w, w]).reshape(1, LANES)
    return pl.pallas_call(
        functools.partial(_diff_kernel, seq=seq, tq=tq, tk=tk, lam_init=lam_init),
        grid=(batch, DIFF_HEADS),
        in_specs=[slab(SLAB_QB), slab(SLAB_KB), slab(SLAB_VB), vec, vec, vec,
                  pl.BlockSpec((4, DIFF_QK_DIM), lambda b, h: (0, 0))],
        out_specs=pl.BlockSpec((1, per, LANES), lambda b, h: (h, b, 0)),
        out_shape=jax.ShapeDtypeStruct((DIFF_HEADS, batch * seq, LANES), BF16),
        scratch_shapes=[
            pltpu.VMEM((seq, LANES), BF16), pltpu.VMEM((seq, LANES), BF16), pltpu.VMEM((seq, LANES), BF16),
            pltpu.VMEM((seq, 2 * LANES), BF16),
            pltpu.VMEM((seq // tk, tq, tk), F32), pltpu.VMEM((seq // tk, tq, tk), F32),
            pltpu.VMEM((tq, 2 * LANES), F32), pltpu.VMEM((tq, 2 * LANES), F32),
            pltpu.VMEM((tq, LANES), F32), pltpu.VMEM((tq, LANES), F32),
        ],
        compiler_params=_cparams(("arbitrary", "arbitrary")),
        name="diff_mixer",
    )(proj, proj, proj, tile2(q_norm), tile2(k_norm), subln.reshape(1, LANES), lam_vecs)


def _dil_kernel(q_ref, k_ref, v_ref, qw_ref, kw_ref, o_ref, qn_s, kn_s, vf_s, q4_s, k4_s, v4_s, sc_s, mb_s,
                acc_s, l_s, m_s, acc4_s, l4_s, m4_s, *, seq):
    blk = DIL_BLOCK
    dh = LANES
    pc = 256
    n_pc = seq // pc
    unroll = 4

    def rms(t, w):
        return t * lax.rsqrt(jnp.mean(t * t, axis=-1, keepdims=True) + EPS) * w

    ones_kv = jnp.ones((2 * blk, LANES), BF16)

    def prep(i, carry):
        rows = pl.ds(pl.multiple_of(i * pc, pc), pc)
        qn_s[rows, :] = rms(q_ref[0, rows, :].astype(F32), qw_ref[...]) * (dh ** -0.5)
        kn_s[rows, :] = rms(k_ref[0, rows, :].astype(F32), kw_ref[...])
        vf_s[rows, :] = v_ref[0, rows, :].astype(F32)
        return carry

    lax.fori_loop(0, n_pc, prep, 0)

    sp = DIL_SPLIT
    cls_rows = seq // sp
    cps = cls_rows // pc

    def chunk_slices(i):
        c4 = i // cps
        c = i - c4 * cps
        return (pl.ds(c4 + sp * pc * c, pc, stride=sp),
                pl.ds(pl.multiple_of(c4 * cls_rows + pc * c, pc), pc))

    def split(i, carry):
        nat, spl = chunk_slices(i)
        q4_s[spl, :] = qn_s[nat, :]
        k4_s[spl, :] = kn_s[nat, :]
        v4_s[spl, :] = vf_s[nat, :]
        return carry

    lax.fori_loop(0, sp * cps, split, 0)

    qi = lax.broadcasted_iota(jnp.int32, (blk, 2 * blk), 0)
    kj = lax.broadcasted_iota(jnp.int32, (blk, 2 * blk), 1)
    seen_layouts = set()

    for window, dil in DILATED_PAIRS:
        steps = window // dil
        n_sub = seq // dil
        nb = n_sub // blk
        assert steps == blk and nb >= 2 and (dil * nb) % unroll == 0
        natural = dil == 1
        assert natural or dil % sp == 0
        first = natural not in seen_layouts
        seen_layouts.add(natural)
        qsrc, ksrc, vsrc, acc_r, l_r, m_r = ((qn_s, kn_s, vf_s, acc_s, l_s, m_s) if natural
                                             else (q4_s, k4_s, v4_s, acc4_s, l4_s, m4_s))

        def slices(idx, dil=dil, nb=nb, natural=natural):
            r = idx // nb
            n = idx - r * nb
            kbase = jnp.maximum(n - 1, 0) * blk
            if natural:
                return n, kbase, pl.ds(pl.multiple_of(blk * n, blk), blk), pl.ds(pl.multiple_of(kbase, blk), 2 * blk)
            d2 = dil // sp
            r_hi = r // sp
            off = (r - r_hi * sp) * cls_rows + r_hi
            if d2 == 1:
                return (n, kbase, pl.ds(pl.multiple_of(off + blk * n, blk), blk),
                        pl.ds(pl.multiple_of(off + kbase, blk), 2 * blk))
            return (n, kbase, pl.ds(off + d2 * blk * n, blk, stride=d2),
                    pl.ds(off + d2 * kbase, 2 * blk, stride=d2))

        def score_block(idx, carry, slices=slices, qsrc=qsrc, ksrc=ksrc):
            n, kbase, qsl, ksl = slices(idx)
            s = _dot_nt(qsrc[qsl, :].astype(BF16), ksrc[ksl, :].astype(BF16))
            dist = (n * blk + qi) - (kbase + kj)
            s = jnp.where((dist >= 0) & (dist <= steps), s, -jnp.inf)
            sc_s[idx] = s
            m = jnp.max(jnp.maximum(s[:, 0:blk], s[:, blk:2 * blk]), axis=-1, keepdims=True)
            mb_s[idx] = jnp.broadcast_to(m, (blk, LANES))
            return carry

        lax.fori_loop(0, dil * nb, score_block, 0, unroll=unroll)

        def value_block(idx, carry, slices=slices, first=first, vsrc=vsrc, acc_r=acc_r, l_r=l_r, m_r=m_r):
            _, _, qsl, ksl = slices(idx)
            s = sc_s[idx]
            m = mb_s[idx]
            p = jnp.concatenate([jnp.exp(s[:, 0:blk] - m), jnp.exp(s[:, blk:2 * blk] - m)], axis=1)
            ve = jnp.concatenate([vsrc[ksl, :].astype(BF16), ones_kv], axis=1)
            ol = _dot(p.astype(BF16), ve)
            o = ol[:, 0:LANES]
            l = ol[:, LANES:2 * LANES]
            if first:
                acc_r[qsl, :] = o
                l_r[qsl, :] = l
                m_r[qsl, :] = m
            else:
                m_old = m_r[qsl, :]
                m_new = jnp.maximum(m_old, m)
                a_old = jnp.exp(m_old - m_new)
                a_cur = jnp.exp(m - m_new)
                acc_r[qsl, :] = a_old * acc_r[qsl, :] + a_cur * o
                l_r[qsl, :] = a_old * l_r[qsl, :] + a_cur * l
                m_r[qsl, :] = m_new
            return carry

        lax.fori_loop(0, dil * nb, value_block, 0, unroll=unroll)

    assert seen_layouts == {True, False}

    def merge(i, carry):
        nat, spl = chunk_slices(i)
        m_nat = m_s[nat, :]
        m_spl = m4_s[spl, :]
        m_new = jnp.maximum(m_nat, m_spl)
        a_nat = jnp.exp(m_nat - m_new)
        a_spl = jnp.exp(m_spl - m_new)
        acc_s[nat, :] = ((a_nat * acc_s[nat, :] + a_spl * acc4_s[spl, :])
                         / (a_nat * l_s[nat, :] + a_spl * l4_s[spl, :]))
        return carry

    lax.fori_loop(0, sp * cps, merge, 0)

    def fin(i, carry):
        rows = pl.ds(pl.multiple_of(i * pc, pc), pc)
        o_ref[0, rows, :] = acc_s[rows, :].astype(BF16)
        return carry

    lax.fori_loop(0, n_pc, fin, 0)


def _dil(proj, q_norm, k_norm, batch, seq):
    per = seq

    def slab(base):
        return pl.BlockSpec((1, per, LANES), lambda b, h, base=base: (base + h, b, 0))

    vec = pl.BlockSpec((1, LANES), lambda b, h: (0, 0))
    row = lambda: pltpu.VMEM((seq, LANES), F32)
    return pl.pallas_call(
        functools.partial(_dil_kernel, seq=seq),
        grid=(batch, DIL_HEADS),
        in_specs=[slab(SLAB_QC), slab(SLAB_KC), slab(SLAB_VC), vec, vec],
        out_specs=pl.BlockSpec((1, per, LANES), lambda b, h: (h, b, 0)),
        out_shape=jax.ShapeDtypeStruct((DIL_HEADS, batch * seq, LANES), BF16),
        scratch_shapes=[
            row(), row(), row(),
            row(), row(), row(),
            pltpu.VMEM((seq // DIL_BLOCK, DIL_BLOCK, 2 * DIL_BLOCK), F32),
            pltpu.VMEM((seq // DIL_BLOCK, DIL_BLOCK, LANES), F32),
            row(), row(), row(),
            row(), row(), row(),
        ],
        compiler_params=_cparams(("arbitrary", "arbitrary")),
        name="dil_mixer",
    )(proj, proj, proj, q_norm.reshape(1, LANES), k_norm.reshape(1, LANES))


def _route_tile(sc, bias, run):
    ne, tm = sc.shape
    epg = EXPERTS_PER_GROUP
    sel = sc + bias
    jrow = lax.broadcasted_iota(jnp.int32, (epg, tm), 0)
    gs, i1s, i2s = [], [], []
    for g in range(N_GROUPS):
        sg = sel[g * epg:(g + 1) * epg, :]
        m1 = jnp.max(sg, axis=0, keepdims=True)
        i1 = jnp.min(jnp.where(sg == m1, jrow, epg), axis=0, keepdims=True)
        rest = jnp.where(jrow == i1, -jnp.inf, sg)
        m2 = jnp.max(rest, axis=0, keepdims=True)
        i2 = jnp.min(jnp.where(rest == m2, jrow, epg), axis=0, keepdims=True)
        gs.append(m1 + m2)
        i1s.append(i1)
        i2s.append(i2)
    gmax = functools.reduce(jnp.maximum, gs)
    gidx = jnp.full((1, tm), N_GROUPS - 1, jnp.int32)
    for g in reversed(range(N_GROUPS - 1)):
        gidx = jnp.where(gs[g] == gmax, g, gidx)
    loc1 = jnp.zeros((1, tm), jnp.int32)
    loc2 = jnp.zeros((1, tm), jnp.int32)
    for g in range(N_GROUPS):
        loc1 = jnp.where(gidx == g, i1s[g], loc1)
        loc2 = jnp.where(gidx == g, i2s[g], loc2)
    e1 = gidx * epg + loc1
    e2 = gidx * epg + loc2
    erow = lax.broadcasted_iota(jnp.int32, (ne, tm), 0)
    oh1 = erow == e1
    oh2 = erow == e2
    s1 = jnp.sum(jnp.where(oh1, sc, 0.0), axis=0, keepdims=True)
    s2 = jnp.sum(jnp.where(oh2, sc, 0.0), axis=0, keepdims=True)
    den = s1 + s2
    tr = lax.broadcasted_iota(jnp.int32, (tm, tm), 0)
    tc = lax.broadcasted_iota(jnp.int32, (tm, tm), 1)
    before = jnp.where(tr < tc, 1.0, 0.0).astype(BF16)
    f1 = jnp.where(oh1, 1.0, 0.0)
    f2 = jnp.where(oh2, 1.0, 0.0)
    pre1 = _dot(f1.astype(BF16), before)
    pre2 = _dot(f2.astype(BF16), before)
    tot1 = jnp.sum(f1, axis=1, keepdims=True)
    tot2 = jnp.sum(f2, axis=1, keepdims=True)
    r1 = jnp.sum(jnp.where(oh1, pre1 + run, 0.0), axis=0, keepdims=True)
    r2 = jnp.sum(jnp.where(oh2, pre2 + (run + tot1), 0.0), axis=0, keepdims=True)
    return e1, e2, s1 / den, s2 / den, r1, r2, run + tot1 + tot2


def _outproj_kernel(oa_ref, ob_ref, oc_ref, w_ref, x_ref, g_ref, nw_ref, sc_ref, sh_ref, wrt_ref, rb_ref,
                    xo_ref, hf_ref, ri_ref, rw_ref, cnt_ref, mix_s, y_s, run_s):
    i = pl.program_id(0)

    @pl.when(i == 0)
    def _():
        run_s[...] = jnp.zeros(run_s.shape, F32)
        y_s[...] = jnp.zeros(y_s.shape, F32)

    y_prev = y_s[(i + 1) % 2]

    off = 0
    for ref, heads in ((oa_ref, GDN_HEADS), (ob_ref, DIFF_HEADS), (oc_ref, DIL_HEADS)):
        for k in range(heads):
            mix_s[:, off:off + LANES] = ref[k]
            off += LANES
    y_s[i % 2] = _dot(mix_s[...], w_ref[...])

    xn = x_ref[...] + g_ref[0] * y_prev
    xo_ref[...] = xn
    hb = _modulated_norm(xn, nw_ref[...], sc_ref[0], sh_ref[0]).astype(BF16)
    hf_ref[...] = _pack_bf16_pairs(hb)
    scores_t = jax.nn.sigmoid(_dot_nt(wrt_ref[...], hb))
    run_in = run_s[...]
    e1, e2, w1, w2, r1, r2, run = _route_tile(scores_t, rb_ref[...], run_in)
    run = jnp.where(i >= 1, run, run_in)
    run_s[...] = run
    cnt_ref[...] = jnp.broadcast_to(run, cnt_ref.shape)
    tm = e1.shape[1]
    ri_ref[...] = jnp.concatenate([e1, e2, r1.astype(jnp.int32), r2.astype(jnp.int32),
                                   jnp.zeros((4, tm), jnp.int32)], axis=0)
    rw_ref[...] = jnp.transpose(jnp.concatenate([w1, w2, jnp.zeros((6, tm), F32)], axis=0))


def _outproj(o_a, o_b, o_c, w_out, x2, g_a, nw, sc, sh, w_router_t, router_bias, seq):
    t, d = x2.shape
    tm = 256
    per_b = seq // tm
    ne = w_router_t.shape[0]
    n_tiles = t // tm
    proj_tile = lambda i: jnp.minimum(i, n_tiles - 1)
    done_tile = lambda i: jnp.maximum(i - 1, 0)
    row = pl.BlockSpec((tm, d), lambda i: (done_tile(i), 0))
    bvec = pl.BlockSpec((1, 1, d), lambda i: (done_tile(i) // per_b, 0, 0))
    return pl.pallas_call(
        _outproj_kernel,
        grid=(n_tiles + 1,),
        in_specs=[
            pl.BlockSpec((GDN_HEADS, tm, LANES), lambda i: (0, proj_tile(i), 0)),
            pl.BlockSpec((DIFF_HEADS, tm, LANES), lambda i: (0, proj_tile(i), 0)),
            pl.BlockSpec((DIL_HEADS, tm, LANES), lambda i: (0, proj_tile(i), 0)),
            pl.BlockSpec((d, d), lambda i: (0, 0)),
            row, bvec,
            pl.BlockSpec((1, d), lambda i: (0, 0)),
            bvec, bvec,
            pl.BlockSpec((ne, d), lambda i: (0, 0)),
            pl.BlockSpec((ne, 1), lambda i: (0, 0)),
        ],
        out_specs=[row,
                   pl.BlockSpec((tm, d // 2), lambda i: (done_tile(i), 0)),
                   pl.BlockSpec((8, tm), lambda i: (0, done_tile(i))),
                   pl.BlockSpec((tm, 8), lambda i: (done_tile(i), 0)),
                   pl.BlockSpec((ne, LANES), lambda i: (0, 0))],
        out_shape=[jax.ShapeDtypeStruct((t, d), F32), jax.ShapeDtypeStruct((t, d // 2), jnp.int32),
                   jax.ShapeDtypeStruct((8, t), jnp.int32), jax.ShapeDtypeStruct((t, 8), F32),
                   jax.ShapeDtypeStruct((ne, LANES), F32)],
        scratch_shapes=[pltpu.VMEM((tm, d), BF16), pltpu.VMEM((2, tm, d), F32), pltpu.VMEM((ne, 1), F32)],
        compiler_params=_cparams(("arbitrary",)),
        name="outproj",
    )(o_a, o_b, o_c, w_out, x2, g_a, nw, sc, sh, w_router_t, router_bias.reshape(ne, 1).astype(F32))


def _moe_kernel(be_ref, nv_ref, first_ref, wpar_ref, wnext_ref, tok_ref, tok_next_ref, dst_spare_ref,
                dst_prev_ref, dst_ref, hf_hbm, wg_hbm, wu_hbm, wd_hbm, y_hbm,
                xbuf, ybuf, wg_f, wu_f, wd_f, wg_s, wu_s, wd_s, gsem, ssem, wsem, *, layer):
    i = pl.program_id(0)
    nv = nv_ref[0]
    slot = i % 2
    rows = MOE_ROWS

    def gather_copy(tok_r, r, s):
        return pltpu.make_async_copy(hf_hbm.at[pl.ds(tok_r[0, 0, r], 1)], xbuf.at[s, pl.ds(r, 1)], gsem.at[s])

    def scatter_copy(dst_r, r, s):
        return pltpu.make_async_copy(ybuf.at[s, pl.ds(r, 1)], y_hbm.at[pl.ds(dst_r[0, 0, r], 1)], ssem.at[s])

    def weight_copies(e, par):
        return (pltpu.make_async_copy(wg_hbm.at[layer, e], wg_f.at[par], wsem.at[par]),
                pltpu.make_async_copy(wu_hbm.at[layer, e], wu_f.at[par], wsem.at[par]),
                pltpu.make_async_copy(wd_hbm.at[layer, e], wd_f.at[par], wsem.at[par]))

    @pl.when(i == 0)
    def _():
        ybuf[...] = jnp.zeros(ybuf.shape, jnp.int32)
        for r in range(rows):
            gather_copy(tok_ref, r, 0).start()
        for r in range(rows):
            scatter_copy(dst_spare_ref, r, 0).start(priority=r % 2)
        for c in weight_copies(be_ref[0], wpar_ref[0]):
            c.start(priority=1)

    @pl.when(first_ref[i] == 1)
    def _():
        par = wpar_ref[i]
        for c in weight_copies(be_ref[i], par):
            c.wait()

        @pl.when(wnext_ref[i] >= 0)
        def _():
            for c in weight_copies(wnext_ref[i], 1 - par):
                c.start(priority=1)

        wg_s[...] = wg_f[par].astype(BF16)
        wu_s[...] = wu_f[par].astype(BF16)
        wd_s[...] = wd_f[par].astype(BF16)

    @pl.when(i < nv)
    def _():
        for r in range(rows):
            gather_copy(tok_ref, r, slot).wait()
        x = _unpack_bf16_pairs(xbuf[slot]).astype(BF16)
        for r in range(rows):
            gather_copy(tok_next_ref, r, 1 - slot).start()
        for r in range(rows):
            scatter_copy(dst_prev_ref, r, 1 - slot).start(priority=r % 2)
        a = _dot(x, wg_s[...])
        u = _dot(x, wu_s[...])
        y = _pack_bf16_pairs(_dot((_silu(a) * u).astype(BF16), wd_s[...]).astype(BF16))

        for r in range(rows):
            scatter_copy(dst_ref, r, slot).wait()
        ybuf[slot] = y

    @pl.when(i == nv - 1)
    def _():
        for r in range(rows):
            gather_copy(tok_next_ref, r, 1 - slot).wait()
            scatter_copy(dst_prev_ref, r, 1 - slot).wait()
        for r in range(rows):
            scatter_copy(dst_ref, r, slot).start(priority=r % 2)
        for r in range(rows):
            scatter_copy(dst_ref, r, slot).wait()


def _moe(hf, layout, w_gate, w_up, w_down, layer):
    block_e, n_valid, first, wpar, wnext, row_tok, row_dst = layout
    t = hf.shape[0]
    d = w_gate.shape[2]
    f = w_gate.shape[3]
    rows = MOE_ROWS
    nb = block_e.shape[0]
    tok3 = row_tok.reshape(nb, 1, rows)
    spare = (TOP_K * t + jnp.arange(2 * rows, dtype=jnp.int32)).reshape(2, 1, rows)
    dst3 = jnp.concatenate([spare, row_dst.reshape(nb, 1, rows)], axis=0)
    smem_blk = lambda fn: pl.BlockSpec((1, 1, rows), fn, memory_space=pltpu.SMEM)
    hbm = pl.BlockSpec(memory_space=pl.ANY)
    grid_spec = pltpu.PrefetchScalarGridSpec(
        num_scalar_prefetch=5,
        grid=(nb,),
        in_specs=[
            smem_blk(lambda i, *_: (i, 0, 0)),
            smem_blk(lambda i, *_: (jnp.minimum(i + 1, nb - 1), 0, 0)),
            smem_blk(lambda i, *_: (0, 0, 0)),
            smem_blk(lambda i, *_: (i + 1, 0, 0)),
            smem_blk(lambda i, *_: (i + 2, 0, 0)),
            hbm, hbm, hbm, hbm,
        ],
        out_specs=hbm,
        scratch_shapes=[
            pltpu.VMEM((2, rows, d // 2), jnp.int32),
            pltpu.VMEM((2, rows, d // 2), jnp.int32),
            pltpu.VMEM((2, d, f), F32), pltpu.VMEM((2, d, f), F32), pltpu.VMEM((2, f, d), F32),
            pltpu.VMEM((d, f), BF16), pltpu.VMEM((d, f), BF16), pltpu.VMEM((f, d), BF16),
            pltpu.SemaphoreType.DMA((2,)),
            pltpu.SemaphoreType.DMA((2,)),
            pltpu.SemaphoreType.DMA((2,)),
        ],
    )
    return pl.pallas_call(
        functools.partial(_moe_kernel, layer=layer),
        grid_spec=grid_spec,
        out_shape=jax.ShapeDtypeStruct((TOP_K * t + 2 * rows, d // 2), jnp.int32),
        compiler_params=_cparams(("arbitrary",), has_side_effects=True, disable_bounds_checks=True),
        name="moe_experts",
    )(block_e, n_valid, first, wpar, wnext, tok3, tok3, dst3, dst3, dst3, hf, w_gate, w_up, w_down)


def _block_layout(route_i, counts_f):
    t = route_i.shape[1]
    rows = MOE_ROWS
    n_assign = TOP_K * t
    counts = counts_f[:, 0].astype(jnp.int32)
    blocks_per_e = (counts + rows - 1) // rows
    blk_end = jnp.cumsum(blocks_per_e)
    blk_start = blk_end - blocks_per_e
    flat_e = route_i[0:TOP_K].reshape(n_assign)
    dest = blk_start[flat_e] * rows + route_i[TOP_K:2 * TOP_K].reshape(n_assign)
    nb = n_assign // rows + N_EXPERTS
    n_rows = nb * rows
    assign = jnp.arange(n_assign, dtype=jnp.int32)
    ridx = jnp.arange(n_rows, dtype=jnp.int32)
    spare = n_assign + ((ridx // rows) % 2) * rows + ridx % rows
    row_dst = spare.at[dest].set(assign, unique_indices=True, mode='promise_in_bounds')
    row_tok = jnp.where(row_dst < n_assign, row_dst % t, 0)
    n_valid = blk_end[-1].astype(jnp.int32)
    bidx = jnp.arange(nb, dtype=jnp.int32)
    block_e = jnp.sum((blk_end[None, :] <= bidx[:, None]).astype(jnp.int32), axis=1)
    block_e = jnp.minimum(block_e, N_EXPERTS - 1)
    last_e = block_e[jnp.maximum(n_valid - 1, 0)]
    block_e = jnp.where(bidx < n_valid, block_e, last_e)
    prev_e = jnp.concatenate([jnp.full((1,), -1, jnp.int32), block_e[:-1]])
    first = ((bidx < n_valid) & (block_e != prev_e)).astype(jnp.int32)
    wpar = (jnp.cumsum(first) + 1) % 2
    first_at = jnp.where(first == 1, bidx, nb)
    later = jnp.concatenate([lax.cummin(first_at[::-1])[::-1][1:], jnp.full((1,), nb, jnp.int32)])
    wnext = jnp.where(later < nb, block_e[jnp.minimum(later, nb - 1)], -1).astype(jnp.int32)
    return block_e, n_valid.reshape(1), first, wpar.astype(jnp.int32), wnext, row_tok, row_dst


def _combine_kernel(x_ref, y0_ref, y1_ref, rw_ref, g_ref, o_ref):
    w = rw_ref[...]
    y0 = _unpack_bf16_pairs(y0_ref[...])
    y1 = _unpack_bf16_pairs(y1_ref[...])
    o_ref[...] = x_ref[...] + g_ref[0] * (y0 * w[:, 0:1] + y1 * w[:, 1:2])


def _combine(x2, y2, route_w, g_f, seq):
    t, d = x2.shape
    tm = 256
    per_b = seq // tm
    n_blk = t // tm
    return pl.pallas_call(
        _combine_kernel,
        grid=(n_blk,),
        in_specs=[pl.BlockSpec((tm, d), lambda i: (i, 0)),
                  pl.BlockSpec((tm, d // 2), lambda i: (i, 0)),
                  pl.BlockSpec((tm, d // 2), lambda i: (n_blk + i, 0)),
                  pl.BlockSpec((tm, 8), lambda i: (i, 0)),
                  pl.BlockSpec((1, 1, d), lambda i: (i // per_b, 0, 0))],
        out_specs=pl.BlockSpec((tm, d), lambda i: (i, 0)),
        out_shape=jax.ShapeDtypeStruct((t, d), F32),
        compiler_params=_cparams(("arbitrary",)),
        name="moe_combine",
    )(x2, y2, y2, route_w, g_f)


def kernel(x, c, norm_mix, norm_ffn, w_ada, b_ada, w_in, conv_w, a_log, dt_bias, gdn_norm, diff_q_norm, diff_k_norm, lam_q1, lam_k1, lam_q2, lam_k2, diff_subln, dil_q_norm, dil_k_norm, w_out, w_router, router_bias, w_gate, w_up, w_down):
    batch, seq, d = x.shape
    depth = w_ada.shape[0]
    t = batch * seq
    x2 = x.reshape(t, d)
    mod = _ada(c, w_ada, b_ada).reshape(depth, batch, N_MOD, 1, d)
    ba_lo = 3 * GDN_HEADS * LANES + GDN_HEADS * LANES
    ba_hi = ba_lo + 2 * GDN_HEADS
    w_router_t = w_router.T.astype(BF16)
    for l in range(depth):
        sh_a, sc_a, g_a, sh_f, sc_f, g_f = (mod[l, :, k] for k in range(N_MOD))
        w_main = jnp.concatenate([w_in[l, :, :ba_lo], w_in[l, :, ba_hi:]], axis=1).astype(BF16)
        w_ba = jnp.pad(w_in[l, :, ba_lo:ba_hi], ((0, 0), (0, LANES - 2 * GDN_HEADS))).astype(BF16)
        proj, ba = _inproj(x2, norm_mix[l].reshape(1, d), sc_a, sh_a, w_main, w_ba, seq)
        o_a = _gdn(proj, ba, conv_w[l], a_log[l], dt_bias[l], gdn_norm[l], batch, seq)
        lam_init = 0.8 - 0.6 * math.exp(-0.3 * l)
        lam_vecs = jnp.stack([lam_q1[l], lam_k1[l], lam_q2[l], lam_k2[l]]).astype(F32)
        o_b = _diff(proj, diff_q_norm[l], diff_k_norm[l], diff_subln[l], lam_vecs, lam_init, batch, seq)
        o_c = _dil(proj, dil_q_norm[l], dil_k_norm[l], batch, seq)
        x2, hf, route_i, route_w, counts = _outproj(o_a, o_b, o_c, w_out[l].astype(BF16), x2, g_a,
                                                    norm_ffn[l].reshape(1, d), sc_f, sh_f, w_router_t, router_bias, seq)
        y2 = _moe(hf, _block_layout(route_i, counts), w_gate, w_up, w_down, l)
        x2 = _combine(x2, y2, route_w, g_f, seq)
    return x2.reshape(batch, seq, d)
```

```python
import functools
import math

import jax
import jax.numpy as jnp
from jax import lax
from jax.experimental import pallas as pl
from jax.experimental.pallas import tpu as pltpu

F32 = jnp.float32
BF16 = jnp.bfloat16

LANES = 128
EPS = 1e-6
N_MOD = 6

GDN_HEADS = 6
GDN_CONV = 4
GDN_CHUNK = 256
GDN_LOCKSTEP = 8
DIFF_HEADS = 4
DIFF_QK_DIM = 64
DIL_HEADS = 6
DILATED_PAIRS = ((128, 1), (512, 4), (2048, 16))
DIL_BLOCK = 128
DIL_SPLIT = 4

N_EXPERTS = 64
N_GROUPS = 8
EXPERTS_PER_GROUP = N_EXPERTS // N_GROUPS
TOP_K = 2
MOE_ROWS = 256

SLAB_QA, SLAB_KA, SLAB_VA, SLAB_ZA = 0, 6, 12, 18
SLAB_QB, SLAB_KB, SLAB_VB = 24, 28, 32
SLAB_QC, SLAB_KC, SLAB_VC = 36, 42, 48
N_SLABS = 54

VMEM_LIMIT = 56 * 1024 * 1024


def _cparams(sem, vmem=VMEM_LIMIT, **kw):
    return pltpu.CompilerParams(dimension_semantics=sem, vmem_limit_bytes=vmem, **kw)


def _silu(v):
    return v * jax.nn.sigmoid(v)


def _dot(a, b):
    return jnp.dot(a, b, preferred_element_type=F32)


def _dot_nt(a, b):
    return lax.dot_general(a, b, (((1,), (1,)), ((), ())), preferred_element_type=F32)


def _dot_tn(a, b):
    return lax.dot_general(a, b, (((0,), (0,)), ((), ())), preferred_element_type=F32)


def _pack_bf16_pairs(v):
    half = v.shape[1] // 2
    lo = lax.bitcast_convert_type(v[:, :half].astype(F32), jnp.int32)
    hi = lax.bitcast_convert_type(v[:, half:].astype(F32), jnp.int32)
    return lax.shift_right_logical(lo, 16) | (hi & -65536)


def _unpack_bf16_pairs(w):
    return jnp.concatenate([lax.bitcast_convert_type(lax.shift_left(w, 16), F32),
                            lax.bitcast_convert_type(w & -65536, F32)], axis=1)


def _ada_kernel(c_ref, w_ref, b_ref, o_ref):
    cact = _silu(c_ref[...]).astype(BF16)
    o_ref[0] = _dot(cact, w_ref[0].astype(BF16)) + b_ref[0]


def _ada(c, w_ada, b_ada):
    depth, d, n = w_ada.shape
    b = c.shape[0]
    tn = 1024
    return pl.pallas_call(
        _ada_kernel,
        grid=(depth, n // tn),
        in_specs=[
            pl.BlockSpec((b, d), lambda l, j: (0, 0)),
            pl.BlockSpec((1, d, tn), lambda l, j: (l, 0, j)),
            pl.BlockSpec((1, 1, tn), lambda l, j: (l, 0, j)),
        ],
        out_specs=pl.BlockSpec((1, b, tn), lambda l, j: (l, 0, j)),
        out_shape=jax.ShapeDtypeStruct((depth, b, n), F32),
        compiler_params=_cparams(("arbitrary", "arbitrary")),
        name="ada_mod",
    )(c, w_ada, b_ada.reshape(depth, 1, n))


def _modulated_norm(x, nw, sc, sh):
    ms = jnp.mean(x * x, axis=-1, keepdims=True)
    return (x * lax.rsqrt(ms + EPS) * nw) * (1.0 + sc) + sh


def _inproj_kernel(x_ref, nw_ref, sc_ref, sh_ref, w_ref, wba_ref, o_ref, ba_ref, h_scr, *, n_sub):
    @pl.when(pl.program_id(1) == 0)
    def _():
        h = _modulated_norm(x_ref[...], nw_ref[...], sc_ref[0], sh_ref[0]).astype(BF16)
        h_scr[...] = h
        ba_ref[...] = _dot(h, wba_ref[...])

    acc = _dot(h_scr[...], w_ref[...])
    for k in range(n_sub):
        o_ref[k] = acc[:, k * LANES:(k + 1) * LANES].astype(BF16)


def _inproj(x2, nw, sc, sh, w_main, w_ba, seq):
    t, d = x2.shape
    n = w_main.shape[1]
    tm = min(1024, seq)
    tn = 768
    n_sub = tn // LANES
    per_b = seq // tm
    return pl.pallas_call(
        functools.partial(_inproj_kernel, n_sub=n_sub),
        grid=(t // tm, n // tn),
        in_specs=[
            pl.BlockSpec((tm, d), lambda i, j: (i, 0)),
            pl.BlockSpec((1, d), lambda i, j: (0, 0)),
            pl.BlockSpec((1, 1, d), lambda i, j: (i // per_b, 0, 0)),
            pl.BlockSpec((1, 1, d), lambda i, j: (i // per_b, 0, 0)),
            pl.BlockSpec((d, tn), lambda i, j: (0, j)),
            pl.BlockSpec((d, LANES), lambda i, j: (0, 0)),
        ],
        out_specs=[
            pl.BlockSpec((n_sub, tm, LANES), lambda i, j: (j, i, 0)),
            pl.BlockSpec((tm, LANES), lambda i, j: (i, 0)),
        ],
        out_shape=[
            jax.ShapeDtypeStruct((n // LANES, t, LANES), BF16),
            jax.ShapeDtypeStruct((t, LANES), F32),
        ],
        scratch_shapes=[pltpu.VMEM((tm, d), BF16)],
        compiler_params=_cparams(("arbitrary", "arbitrary")),
        name="inproj",
    )(x2, nw, sc, sh, w_main, w_ba)


def _unit_lower_inverse(lms, xr):
    c = lms[0].shape[0]

    def mm(a, b):
        return [_dot(ai.astype(BF16), bi.astype(BF16)) for ai, bi in zip(a, b)]

    d1 = [jnp.where(xr < 16, lm, 0.0) for lm in lms]
    d2 = mm(d1, d1)
    d4 = mm(d2, d2)
    d8 = mm(d4, d4)
    p = [b - a - t for a, b, t in zip(d1, d2, mm(d1, d2))]
    p = [pi + di + t for pi, di, t in zip(p, d4, mm(p, d4))]
    p = [pi + di + t for pi, di, t in zip(p, d8, mm(p, d8))]
    eye = jnp.where(xr == 0, 1.0, 0.0)
    x = [eye + pi for pi in p]
    blk = 32
    while blk <= c:
        e = [jnp.where((xr < blk) & (xr >= blk // 2), lm, 0.0) for lm in lms]
        x = [xi - t for xi, t in zip(x, mm(x, mm(e, x)))]
        blk *= 2
    return x


def _gdn_kernel(q_ref, k_ref, v_ref, z_ref, ba_ref, cwq_ref, cwk_ref, cwv_ref, al_ref, dtb_ref, nw_ref,
                o_ref, xq, xk, xv, u_s, w_s, a_s, qd_s, kd_s, cd_s, *, seq, chunk):
    c = chunk
    n_chunks = seq // c
    head = pl.program_id(1)
    dk = LANES

    lane1 = lax.broadcasted_iota(jnp.int32, (1, LANES), 1)
    a_exp = jnp.exp(jnp.sum(jnp.where(lane1 == head, al_ref[...], 0.0), axis=-1, keepdims=True))
    dtb = jnp.sum(jnp.where(lane1 == head, dtb_ref[...], 0.0), axis=-1, keepdims=True)

    zeros8 = jnp.zeros((8, LANES), F32)
    for src, dst in ((q_ref, xq), (k_ref, xk), (v_ref, xv)):
        dst[0:8, :] = zeros8

        def stage(i, carry, src=src, dst=dst):
            r0 = pl.multiple_of(i * c, c)
            dst[pl.ds(r0 + 8, c), :] = src[0, pl.ds(r0, c), :].astype(F32)
            return carry

        lax.fori_loop(0, n_chunks, stage, 0)

    row = lax.broadcasted_iota(jnp.int32, (c, c), 0)
    col = lax.broadcasted_iota(jnp.int32, (c, c), 1)
    causal = row >= col
    strict = row > col
    xr = row ^ col
    tril = jnp.where(causal, 1.0, 0.0).astype(BF16)
    lane_c = lax.broadcasted_iota(jnp.int32, (c, LANES), 1)

    def conv(xs, cw_ref, r0):
        acc = cw_ref[0, GDN_CONV - 1:GDN_CONV, :] * xs[pl.ds(r0 + 8, c), :]
        for back in range(1, GDN_CONV):
            tap = GDN_CONV - 1 - back
            acc = acc + cw_ref[0, tap:tap + 1, :] * xs[pl.ds(r0 + 8 - back, c), :]
        return _silu(acc)

    def l2n(t):
        return t * lax.rsqrt(jnp.sum(t * t, axis=-1, keepdims=True) + EPS)

    def log_decay(r0):
        bat = ba_ref[pl.ds(r0, c), :]
        bcol = jnp.sum(jnp.where(lane_c == head, bat, 0.0), axis=-1, keepdims=True)
        acol = jnp.sum(jnp.where(lane_c == head + GDN_HEADS, bat, 0.0), axis=-1, keepdims=True)
        xsp = acol + dtb
        softplus = jnp.maximum(xsp, 0.0) + jnp.log1p(jnp.exp(-jnp.abs(xsp)))
        return jax.nn.sigmoid(bcol), jnp.broadcast_to(-a_exp * softplus, (c, LANES))

    def intra(step, carry):
        par = range(GDN_LOCKSTEP)
        ns = [step * GDN_LOCKSTEP + j for j in par]
        r0s = [pl.multiple_of(n * c, c) for n in ns]
        qn = [l2n(conv(xq, cwq_ref, r0)) * (dk ** -0.5) for r0 in r0s]
        kn = [l2n(conv(xk, cwk_ref, r0)) for r0 in r0s]
        vc = [conv(xv, cwv_ref, r0) for r0 in r0s]
        bg = [log_decay(r0) for r0 in r0s]
        beta = [t[0] for t in bg]
        g_hi = [t[1].astype(BF16) for t in bg]
        g_lo = [(t[1] - h.astype(F32)).astype(BF16) for t, h in zip(bg, g_hi)]
        gc = [_dot(tril, h) + _dot(tril, lo_) for h, lo_ in zip(g_hi, g_lo)]
        decay = []
        for gci in gc:
            gc_row = jnp.transpose(gci)[0:1, :]
            diff = jnp.concatenate([gci] * (c // LANES), axis=1) - gc_row
            decay.append(jnp.where(causal, jnp.exp(jnp.where(causal, diff, 0.0)), 0.0))
        kb = [k * b for k, b in zip(kn, beta)]
        knb = [k.astype(BF16) for k in kn]
        lm = [jnp.where(strict, _dot_nt(kbi.astype(BF16), ki) * dc, 0.0) for kbi, ki, dc in zip(kb, knb, decay)]
        attn = [jnp.where(causal, _dot_nt(q.astype(BF16), ki) * dc, 0.0) for q, ki, dc in zip(qn, knb, decay)]
        tinv = _unit_lower_inverse(lm, xr)
        egc = [jnp.exp(g) for g in gc]
        sol = [_dot(ti.astype(BF16), jnp.concatenate([v * b, kbi * e], axis=1).astype(BF16))
               for ti, v, b, kbi, e in zip(tinv, vc, beta, kb, egc)]
        for j in par:
            rows = pl.ds(r0s[j], c)
            u_s[rows, :] = sol[j][:, :LANES]
            w_s[rows, :] = sol[j][:, LANES:].astype(BF16)
            a_s[rows, :] = attn[j].astype(BF16)
            qd_s[rows, :] = (qn[j] * egc[j]).astype(BF16)
            g_last = gc[j][c - 1:c, :]
            kd_s[rows, :] = (kn[j] * jnp.exp(g_last - gc[j])).astype(BF16)
            cd_s[pl.ds(pl.multiple_of(ns[j] * 8, 8), 8), :] = jnp.broadcast_to(jnp.exp(g_last), (8, LANES))
        return carry

    lax.fori_loop(0, n_chunks // GDN_LOCKSTEP, intra, 0)

    def inter(n, state):
        r0 = pl.multiple_of(n * c, c)
        rows = pl.ds(r0, c)
        sb = state.astype(BF16)
        v_new = u_s[rows, :] - _dot(w_s[rows, :], sb)
        vb = v_new.astype(BF16)
        o = _dot(qd_s[rows, :], sb) + _dot(a_s[rows, :], vb)
        cd = cd_s[pl.ds(pl.multiple_of(n * 8, 8), 1), :]
        new_state = state * cd + _dot_tn(kd_s[rows, :], vb)
        on = o * lax.rsqrt(jnp.mean(o * o, axis=-1, keepdims=True) + EPS) * nw_ref[...]
        z = z_ref[0, rows, :].astype(F32)
        o_ref[0, rows, :] = (on * _silu(z)).astype(BF16)
        return new_state

    lax.fori_loop(0, n_chunks, inter, jnp.zeros((dk, LANES), F32), unroll=4)


def _gdn(proj, ba, conv_w, a_log, dt_bias, gdn_norm, batch, seq):
    c = GDN_CHUNK
    per = seq
    cw = conv_w.reshape(GDN_CONV, 3 * GDN_HEADS, LANES).transpose(1, 0, 2)
    pad = lambda v: jnp.pad(v, (0, LANES - v.shape[0])).reshape(1, LANES)

    def slab(base):
        return pl.BlockSpec((1, per, LANES), lambda b, h, base=base: (base + h, b, 0))

    def cwspec(base):
        return pl.BlockSpec((1, GDN_CONV, LANES), lambda b, h, base=base: (base + h, 0, 0))

    vec = pl.BlockSpec((1, LANES), lambda b, h: (0, 0))
    return pl.pallas_call(
        functools.partial(_gdn_kernel, seq=seq, chunk=c),
        grid=(batch, GDN_HEADS),
        in_specs=[slab(SLAB_QA), slab(SLAB_KA), slab(SLAB_VA), slab(SLAB_ZA),
                  pl.BlockSpec((per, LANES), lambda b, h: (b, 0)),
                  cwspec(0), cwspec(GDN_HEADS), cwspec(2 * GDN_HEADS), vec, vec, vec],
        out_specs=pl.BlockSpec((1, per, LANES), lambda b, h: (h, b, 0)),
        out_shape=jax.ShapeDtypeStruct((GDN_HEADS, batch * seq, LANES), BF16),
        scratch_shapes=[
            pltpu.VMEM((seq + 8, LANES), F32), pltpu.VMEM((seq + 8, LANES), F32),
            pltpu.VMEM((seq + 8, LANES), F32),
            pltpu.VMEM((seq, LANES), F32),
            pltpu.VMEM((seq, LANES), BF16),
            pltpu.VMEM((seq, c), BF16),
            pltpu.VMEM((seq, LANES), BF16),
            pltpu.VMEM((seq, LANES), BF16),
            pltpu.VMEM((seq // c * 8, LANES), F32),
        ],
        compiler_params=_cparams(("arbitrary", "arbitrary")),
        name="gdn_mixer",
    )(proj, proj, proj, proj, ba, cw, cw, cw, pad(a_log), pad(dt_bias), gdn_norm.reshape(1, LANES))


def _diff_kernel(q_ref, k_ref, v_ref, qw_ref, kw_ref, sw_ref, lam_ref, o_ref,
                 q1_s, q2_s, kn_s, ve_s, sc1, sc2, acc1, acc2, m1, m2, *, seq, tq, tk, lam_init):
    dh = DIFF_QK_DIM
    n_blk = seq // tq
    per_q = tq // tk
    lane = lax.broadcasted_iota(jnp.int32, (1, LANES), 1)
    lo = lane < dh

    lam = (jnp.exp(jnp.sum(lam_ref[0:1, :] * lam_ref[1:2, :], axis=-1, keepdims=True))
           - jnp.exp(jnp.sum(lam_ref[2:3, :] * lam_ref[3:4, :], axis=-1, keepdims=True)) + lam_init)

    def halfnorm(t, w):
        sq = t * t
        s_lo = jnp.sum(jnp.where(lo, sq, 0.0), axis=-1, keepdims=True)
        s_hi = jnp.sum(jnp.where(lo, 0.0, sq), axis=-1, keepdims=True)
        r = jnp.where(lo, lax.rsqrt(s_lo / dh + EPS), lax.rsqrt(s_hi / dh + EPS))
        return t * r * w

    ones_blk = jnp.ones((tk, LANES), BF16)

    def prep(i, carry):
        rows = pl.ds(pl.multiple_of(i * tk, tk), tk)
        kn_s[rows, :] = halfnorm(k_ref[0, rows, :].astype(F32), kw_ref[...]).astype(BF16)
        qn = halfnorm(q_ref[0, rows, :].astype(F32), qw_ref[...]) * (dh ** -0.5)
        q1_s[rows, :] = jnp.where(lo, qn, 0.0).astype(BF16)
        q2_s[rows, :] = jnp.where(lo, 0.0, qn).astype(BF16)
        ve_s[rows, 0:LANES] = v_ref[0, rows, :]
        ve_s[rows, LANES:2 * LANES] = ones_blk
        return carry

    lax.fori_loop(0, seq // tk, prep, 0)

    row = lax.broadcasted_iota(jnp.int32, (tq, tk), 0)
    col = lax.broadcasted_iota(jnp.int32, (tq, tk), 1)

    n_parts = tk // LANES

    def q_block(qi, carry):
        qrows = pl.ds(pl.multiple_of(qi * tq, tq), tq)
        maps = ((q1_s[qrows, :], sc1, m1, acc1), (q2_s[qrows, :], sc2, m2, acc2))
        for _, _, m_r, a_r in maps:
            m_r[...] = jnp.full((tq, LANES), -jnp.inf, F32)
            a_r[...] = jnp.zeros((tq, 2 * LANES), F32)

        def score_step(kj, masked):
            kb = kn_s[pl.ds(pl.multiple_of(kj * tk, tk), tk), :]
            for qq, s_c, m_r, _ in maps:
                s = _dot_nt(qq, kb)
                if masked:
                    s = jnp.where(row + qi * tq >= col + kj * tk, s, -jnp.inf)
                s_c[kj] = s
                smax = m_r[...]
                for part in range(n_parts):
                    smax = jnp.maximum(smax, s[:, part * LANES:(part + 1) * LANES])
                m_r[...] = smax

        def off_diag(kq, c2):
            for part in range(per_q):
                score_step(kq * per_q + part, False)
            return c2

        lax.fori_loop(0, qi, off_diag, 0)
        for dblk in range(per_q):
            score_step(qi * per_q + dblk, True)
        for _, _, m_r, _ in maps:
            m_r[...] = jnp.broadcast_to(jnp.max(m_r[...], axis=-1, keepdims=True), (tq, LANES))

        def value_step(kq, c2):
            for _, s_c, m_r, a_r in maps:
                mrep = m_r[...]
                upd = a_r[...]
                for sub in range(per_q):
                    kj = kq * per_q + sub
                    s = s_c[kj]
                    p = jnp.concatenate(
                        [jnp.exp(s[:, part * LANES:(part + 1) * LANES] - mrep) for part in range(n_parts)], axis=1)
                    upd = upd + _dot(p.astype(BF16), ve_s[pl.ds(pl.multiple_of(kj * tk, tk), tk), :])
                a_r[...] = upd
            return c2

        lax.fori_loop(0, qi + 1, value_step, 0)
        o = (acc1[:, 0:LANES] / acc1[:, LANES:2 * LANES]
             - lam * (acc2[:, 0:LANES] / acc2[:, LANES:2 * LANES]))
        o = o * lax.rsqrt(jnp.mean(o * o, axis=-1, keepdims=True) + EPS) * sw_ref[...] * (1.0 - lam_init)
        o_ref[0, qrows, :] = o.astype(BF16)
        return carry

    lax.fori_loop(0, n_blk, q_block, 0)


def _diff(proj, q_norm, k_norm, subln, lam_vecs, lam_init, batch, seq):
    tq = 512
    tk = 256
    per = seq

    def slab(base):
        return pl.BlockSpec((1, per, LANES), lambda b, h, base=base: (base + h, b, 0))

    vec = pl.BlockSpec((1, LANES), lambda b, h: (0, 0))
    tile2 = lambda w: jnp.concatenate([w, w]).reshape(1, LANES)
    return pl.pallas_call(
        functools.partial(_diff_kernel, seq=seq, tq=tq, tk=tk, lam_init=lam_init),
        grid=(batch, DIFF_HEADS),
        in_specs=[slab(SLAB_QB), slab(SLAB_KB), slab(SLAB_VB), vec, vec, vec,
                  pl.BlockSpec((4, DIFF_QK_DIM), lambda b, h: (0, 0))],
        out_specs=pl.BlockSpec((1, per, LANES), lambda b, h: (h, b, 0)),
        out_shape=jax.ShapeDtypeStruct((DIFF_HEADS, batch * seq, LANES), BF16),
        scratch_shapes=[
            pltpu.VMEM((seq, LANES), BF16), pltpu.VMEM((seq, LANES), BF16), pltpu.VMEM((seq, LANES), BF16),
            pltpu.VMEM((seq, 2 * LANES), BF16),
            pltpu.VMEM((seq // tk, tq, tk), F32), pltpu.VMEM((seq // tk, tq, tk), F32),
            pltpu.VMEM((tq, 2 * LANES), F32), pltpu.VMEM((tq, 2 * LANES), F32),
            pltpu.VMEM((tq, LANES), F32), pltpu.VMEM((tq, LANES), F32),
        ],
        compiler_params=_cparams(("arbitrary", "arbitrary")),
        name="diff_mixer",
    )(proj, proj, proj, tile2(q_norm), tile2(k_norm), subln.reshape(1, LANES), lam_vecs)


def _dil_kernel(q_ref, k_ref, v_ref, qw_ref, kw_ref, o_ref, qn_s, kn_s, vf_s, q4_s, k4_s, v4_s, sc_s, mb_s,
                acc_s, l_s, m_s, acc4_s, l4_s, m4_s, *, seq):
    blk = DIL_BLOCK
    dh = LANES
    pc = 256
    n_pc = seq // pc
    unroll = 4

    def rms(t, w):
        return t * lax.rsqrt(jnp.mean(t * t, axis=-1, keepdims=True) + EPS) * w

    ones_kv = jnp.ones((2 * blk, LANES), BF16)

    def prep(i, carry):
        rows = pl.ds(pl.multiple_of(i * pc, pc), pc)
        qn_s[rows, :] = rms(q_ref[0, rows, :].astype(F32), qw_ref[...]) * (dh ** -0.5)
        kn_s[rows, :] = rms(k_ref[0, rows, :].astype(F32), kw_ref[...])
        vf_s[rows, :] = v_ref[0, rows, :].astype(F32)
        return carry

    lax.fori_loop(0, n_pc, prep, 0)

    sp = DIL_SPLIT
    cls_rows = seq // sp
    cps = cls_rows // pc

    def chunk_slices(i):
        c4 = i // cps
        c = i - c4 * cps
        return (pl.ds(c4 + sp * pc * c, pc, stride=sp),
                pl.ds(pl.multiple_of(c4 * cls_rows + pc * c, pc), pc))

    def split(i, carry):
        nat, spl = chunk_slices(i)
        q4_s[spl, :] = qn_s[nat, :]
        k4_s[spl, :] = kn_s[nat, :]
        v4_s[spl, :] = vf_s[nat, :]
        return carry

    lax.fori_loop(0, sp * cps, split, 0)

    qi = lax.broadcasted_iota(jnp.int32, (blk, 2 * blk), 0)
    kj = lax.broadcasted_iota(jnp.int32, (blk, 2 * blk), 1)
    seen_layouts = set()

    for window, dil in DILATED_PAIRS:
        steps = window // dil
        n_sub = seq // dil
        nb = n_sub // blk
        assert steps == blk and nb >= 2 and (dil * nb) % unroll == 0
        natural = dil == 1
        assert natural or dil % sp == 0
        first = natural not in seen_layouts
        seen_layouts.add(natural)
        qsrc, ksrc, vsrc, acc_r, l_r, m_r = ((qn_s, kn_s, vf_s, acc_s, l_s, m_s) if natural
                                             else (q4_s, k4_s, v4_s, acc4_s, l4_s, m4_s))

        def slices(idx, dil=dil, nb=nb, natural=natural):
            r = idx // nb
            n = idx - r * nb
            kbase = jnp.maximum(n - 1, 0) * blk
            if natural:
                return n, kbase, pl.ds(pl.multiple_of(blk * n, blk), blk), pl.ds(pl.multiple_of(kbase, blk), 2 * blk)
            d2 = dil // sp
            r_hi = r // sp
            off = (r - r_hi * sp) * cls_rows + r_hi
            if d2 == 1:
                return (n, kbase, pl.ds(pl.multiple_of(off + blk * n, blk), blk),
                        pl.ds(pl.multiple_of(off + kbase, blk), 2 * blk))
            return (n, kbase, pl.ds(off + d2 * blk * n, blk, stride=d2),
                    pl.ds(off + d2 * kbase, 2 * blk, stride=d2))

        def score_block(idx, carry, slices=slices, qsrc=qsrc, ksrc=ksrc):
            n, kbase, qsl, ksl = slices(idx)
            s = _dot_nt(qsrc[qsl, :].astype(BF16), ksrc[ksl, :].astype(BF16))
            dist = (n * blk + qi) - (kbase + kj)
            s = jnp.where((dist >= 0) & (dist <= steps), s, -jnp.inf)
            sc_s[idx] = s
            m = jnp.max(jnp.maximum(s[:, 0:blk], s[:, blk:2 * blk]), axis=-1, keepdims=True)
            mb_s[idx] = jnp.broadcast_to(m, (blk, LANES))
            return carry

        lax.fori_loop(0, dil * nb, score_block, 0, unroll=unroll)

        def value_block(idx, carry, slices=slices, first=first, vsrc=vsrc, acc_r=acc_r, l_r=l_r, m_r=m_r):
            _, _, qsl, ksl = slices(idx)
            s = sc_s[idx]
            m = mb_s[idx]
            p = jnp.concatenate([jnp.exp(s[:, 0:blk] - m), jnp.exp(s[:, blk:2 * blk] - m)], axis=1)
            ve = jnp.concatenate([vsrc[ksl, :].astype(BF16), ones_kv], axis=1)
            ol = _dot(p.astype(BF16), ve)
            o = ol[:, 0:LANES]
            l = ol[:, LANES:2 * LANES]
            if first:
                acc_r[qsl, :] = o
                l_r[qsl, :] = l
                m_r[qsl, :] = m
            else:
                m_old = m_r[qsl, :]
                m_new = jnp.maximum(m_old, m)
                a_old = jnp.exp(m_old - m_new)
                a_cur = jnp.exp(m - m_new)
                acc_r[qsl, :] = a_old * acc_r[qsl, :] + a_cur * o
                l_r[qsl, :] = a_old * l_r[qsl, :] + a_cur * l
                m_r[qsl, :] = m_new
            return carry

        lax.fori_loop(0, dil * nb, value_block, 0, unroll=unroll)

    assert seen_layouts == {True, False}

    def merge(i, carry):
        nat, spl = chunk_slices(i)
        m_nat = m_s[nat, :]
        m_spl = m4_s[spl, :]
        m_new = jnp.maximum(m_nat, m_spl)
        a_nat = jnp.exp(m_nat - m_new)
        a_spl = jnp.exp(m_spl - m_new)
        acc_s[nat, :] = ((a_nat * acc_s[nat, :] + a_spl * acc4_s[spl, :])
                         / (a_nat * l_s[nat, :] + a_spl * l4_s[spl, :]))
        return carry

    lax.fori_loop(0, sp * cps, merge, 0)

    def fin(i, carry):
        rows = pl.ds(pl.multiple_of(i * pc, pc), pc)
        o_ref[0, rows, :] = acc_s[rows, :].astype(BF16)
        return carry

    lax.fori_loop(0, n_pc, fin, 0)


def _dil(proj, q_norm, k_norm, batch, seq):
    per = seq

    def slab(base):
        return pl.BlockSpec((1, per, LANES), lambda b, h, base=base: (base + h, b, 0))

    vec = pl.BlockSpec((1, LANES), lambda b, h: (0, 0))
    row = lambda: pltpu.VMEM((seq, LANES), F32)
    return pl.pallas_call(
        functools.partial(_dil_kernel, seq=seq),
        grid=(batch, DIL_HEADS),
        in_specs=[slab(SLAB_QC), slab(SLAB_KC), slab(SLAB_VC), vec, vec],
        out_specs=pl.BlockSpec((1, per, LANES), lambda b, h: (h, b, 0)),
        out_shape=jax.ShapeDtypeStruct((DIL_HEADS, batch * seq, LANES), BF16),
        scratch_shapes=[
            row(), row(), row(),
            row(), row(), row(),
            pltpu.VMEM((seq // DIL_BLOCK, DIL_BLOCK, 2 * DIL_BLOCK), F32),
            pltpu.VMEM((seq // DIL_BLOCK, DIL_BLOCK, LANES), F32),
            row(), row(), row(),
            row(), row(), row(),
        ],
        compiler_params=_cparams(("arbitrary", "arbitrary")),
        name="dil_mixer",
    )(proj, proj, proj, q_norm.reshape(1, LANES), k_norm.reshape(1, LANES))


def _route_tile(sc, bias, run):
    ne, tm = sc.shape
    epg = EXPERTS_PER_GROUP
    sel = sc + bias
    jrow = lax.broadcasted_iota(jnp.int32, (epg, tm), 0)
    gs, i1s, i2s = [], [], []
    for g in range(N_GROUPS):
        sg = sel[g * epg:(g + 1) * epg, :]
        m1 = jnp.max(sg, axis=0, keepdims=True)
        i1 = jnp.min(jnp.where(sg == m1, jrow, epg), axis=0, keepdims=True)
        rest = jnp.where(jrow == i1, -jnp.inf, sg)
        m2 = jnp.max(rest, axis=0, keepdims=True)
        i2 = jnp.min(jnp.where(rest == m2, jrow, epg), axis=0, keepdims=True)
        gs.append(m1 + m2)
        i1s.append(i1)
        i2s.append(i2)
    gmax = functools.reduce(jnp.maximum, gs)
    gidx = jnp.full((1, tm), N_GROUPS - 1, jnp.int32)
    for g in reversed(range(N_GROUPS - 1)):
        gidx = jnp.where(gs[g] == gmax, g, gidx)
    loc1 = jnp.zeros((1, tm), jnp.int32)
    loc2 = jnp.zeros((1, tm), jnp.int32)
    for g in range(N_GROUPS):
        loc1 = jnp.where(gidx == g, i1s[g], loc1)
        loc2 = jnp.where(gidx == g, i2s[g], loc2)
    e1 = gidx * epg + loc1
    e2 = gidx * epg + loc2
    erow = lax.broadcasted_iota(jnp.int32, (ne, tm), 0)
    oh1 = erow == e1
    oh2 = erow == e2
    s1 = jnp.sum(jnp.where(oh1, sc, 0.0), axis=0, keepdims=True)
    s2 = jnp.sum(jnp.where(oh2, sc, 0.0), axis=0, keepdims=True)
    den = s1 + s2
    tr = lax.broadcasted_iota(jnp.int32, (tm, tm), 0)
    tc = lax.broadcasted_iota(jnp.int32, (tm, tm), 1)
    before = jnp.where(tr < tc, 1.0, 0.0).astype(BF16)
    f1 = jnp.where(oh1, 1.0, 0.0)
    f2 = jnp.where(oh2, 1.0, 0.0)
    pre1 = _dot(f1.astype(BF16), before)
    pre2 = _dot(f2.astype(BF16), before)
    tot1 = jnp.sum(f1, axis=1, keepdims=True)
    tot2 = jnp.sum(f2, axis=1, keepdims=True)
    r1 = jnp.sum(jnp.where(oh1, pre1 + run, 0.0), axis=0, keepdims=True)
    r2 = jnp.sum(jnp.where(oh2, pre2 + (run + tot1), 0.0), axis=0, keepdims=True)
    return e1, e2, s1 / den, s2 / den, r1, r2, run + tot1 + tot2


def _outproj_kernel(oa_ref, ob_ref, oc_ref, w_ref, x_ref, g_ref, nw_ref, sc_ref, sh_ref, wrt_ref, rb_ref,
                    xo_ref, hf_ref, ri_ref, rw_ref, cnt_ref, mix_s, y_s, run_s):
    i = pl.program_id(0)

    @pl.when(i == 0)
    def _():
        run_s[...] = jnp.zeros(run_s.shape, F32)
        y_s[...] = jnp.zeros(y_s.shape, F32)

    y_prev = y_s[(i + 1) % 2]

    off = 0
    for ref, heads in ((oa_ref, GDN_HEADS), (ob_ref, DIFF_HEADS), (oc_ref, DIL_HEADS)):
        for k in range(heads):
            mix_s[:, off:off + LANES] = ref[k]
            off += LANES
    y_s[i % 2] = _dot(mix_s[...], w_ref[...])

    xn = x_ref[...] + g_ref[0] * y_prev
    xo_ref[...] = xn
    hb = _modulated_norm(xn, nw_ref[...], sc_ref[0], sh_ref[0]).astype(BF16)
    hf_ref[...] = _pack_bf16_pairs(hb)
    scores_t = jax.nn.sigmoid(_dot_nt(wrt_ref[...], hb))
    run_in = run_s[...]
    e1, e2, w1, w2, r1, r2, run = _route_tile(scores_t, rb_ref[...], run_in)
    run = jnp.where(i >= 1, run, run_in)
    run_s[...] = run
    cnt_ref[...] = jnp.broadcast_to(run, cnt_ref.shape)
    tm = e1.shape[1]
    ri_ref[...] = jnp.concatenate([e1, e2, r1.astype(jnp.int32), r2.astype(jnp.int32),
                                   jnp.zeros((4, tm), jnp.int32)], axis=0)
    rw_ref[...] = jnp.transpose(jnp.concatenate([w1, w2, jnp.zeros((6, tm), F32)], axis=0))


def _outproj(o_a, o_b, o_c, w_out, x2, g_a, nw, sc, sh, w_router_t, router_bias, seq):
    t, d = x2.shape
    tm = 256
    per_b = seq // tm
    ne = w_router_t.shape[0]
    n_tiles = t // tm
    proj_tile = lambda i: jnp.minimum(i, n_tiles - 1)
    done_tile = lambda i: jnp.maximum(i - 1, 0)
    row = pl.BlockSpec((tm, d), lambda i: (done_tile(i), 0))
    bvec = pl.BlockSpec((1, 1, d), lambda i: (done_tile(i) // per_b, 0, 0))
    return pl.pallas_call(
        _outproj_kernel,
        grid=(n_tiles + 1,),
        in_specs=[
            pl.BlockSpec((GDN_HEADS, tm, LANES), lambda i: (0, proj_tile(i), 0)),
            pl.BlockSpec((DIFF_HEADS, tm, LANES), lambda i: (0, proj_tile(i), 0)),
            pl.BlockSpec((DIL_HEADS, tm, LANES), lambda i: (0, proj_tile(i), 0)),
            pl.BlockSpec((d, d), lambda i: (0, 0)),
            row, bvec,
            pl.BlockSpec((1, d), lambda i: (0, 0)),
            bvec, bvec,
            pl.BlockSpec((ne, d), lambda i: (0, 0)),
            pl.BlockSpec((ne, 1), lambda i: (0, 0)),
        ],
        out_specs=[row,
                   pl.BlockSpec((tm, d // 2), lambda i: (done_tile(i), 0)),
                   pl.BlockSpec((8, tm), lambda i: (0, done_tile(i))),
                   pl.BlockSpec((tm, 8), lambda i: (done_tile(i), 0)),
                   pl.BlockSpec((ne, LANES), lambda i: (0, 0))],
        out_shape=[jax.ShapeDtypeStruct((t, d), F32), jax.ShapeDtypeStruct((t, d // 2), jnp.int32),
                   jax.ShapeDtypeStruct((8, t), jnp.int32), jax.ShapeDtypeStruct((t, 8), F32),
                   jax.ShapeDtypeStruct((ne, LANES), F32)],
        scratch_shapes=[pltpu.VMEM((tm, d), BF16), pltpu.VMEM((2, tm, d), F32), pltpu.VMEM((ne, 1), F32)],
        compiler_params=_cparams(("arbitrary",)),
        name="outproj",
    )(o_a, o_b, o_c, w_out, x2, g_a, nw, sc, sh, w_router_t, router_bias.reshape(ne, 1).astype(F32))


def _moe_kernel(be_ref, nv_ref, first_ref, wpar_ref, wnext_ref, tok_ref, tok_next_ref, dst_spare_ref,
                dst_prev_ref, dst_ref, hf_hbm, wg_hbm, wu_hbm, wd_hbm, y_hbm,
                xbuf, ybuf, wg_f, wu_f, wd_f, wg_s, wu_s, wd_s, gsem, ssem, wsem, *, layer):
    i = pl.program_id(0)
    nv = nv_ref[0]
    slot = i % 2
    rows = MOE_ROWS

    def gather_copy(tok_r, r, s):
        return pltpu.make_async_copy(hf_hbm.at[pl.ds(tok_r[0, 0, r], 1)], xbuf.at[s, pl.ds(r, 1)], gsem.at[s])

    def scatter_copy(dst_r, r, s):
        return pltpu.make_async_copy(ybuf.at[s, pl.ds(r, 1)], y_hbm.at[pl.ds(dst_r[0, 0, r], 1)], ssem.at[s])

    def weight_copies(e, par):
        return (pltpu.make_async_copy(wg_hbm.at[layer, e], wg_f.at[par], wsem.at[par]),
                pltpu.make_async_copy(wu_hbm.at[layer, e], wu_f.at[par], wsem.at[par]),
                pltpu.make_async_copy(wd_hbm.at[layer, e], wd_f.at[par], wsem.at[par]))

    @pl.when(i == 0)
    def _():
        ybuf[...] = jnp.zeros(ybuf.shape, jnp.int32)
        for r in range(rows):
            gather_copy(tok_ref, r, 0).start()
        for r in range(rows):
            scatter_copy(dst_spare_ref, r, 0).start(priority=r % 2)
        for c in weight_copies(be_ref[0], wpar_ref[0]):
            c.start(priority=1)

    @pl.when(first_ref[i] == 1)
    def _():
        par = wpar_ref[i]
        for c in weight_copies(be_ref[i], par):
            c.wait()

        @pl.when(wnext_ref[i] >= 0)
        def _():
            for c in weight_copies(wnext_ref[i], 1 - par):
                c.start(priority=1)

        wg_s[...] = wg_f[par].astype(BF16)
        wu_s[...] = wu_f[par].astype(BF16)
        wd_s[...] = wd_f[par].astype(BF16)

    @pl.when(i < nv)
    def _():
        for r in range(rows):
            gather_copy(tok_ref, r, slot).wait()
        x = _unpack_bf16_pairs(xbuf[slot]).astype(BF16)
        for r in range(rows):
            gather_copy(tok_next_ref, r, 1 - slot).start()
        for r in range(rows):
            scatter_copy(dst_prev_ref, r, 1 - slot).start(priority=r % 2)
        a = _dot(x, wg_s[...])
        u = _dot(x, wu_s[...])
        y = _pack_bf16_pairs(_dot((_silu(a) * u).astype(BF16), wd_s[...]).astype(BF16))

        for r in range(rows):
            scatter_copy(dst_ref, r, slot).wait()
        ybuf[slot] = y

    @pl.when(i == nv - 1)
    def _():
        for r in range(rows):
            gather_copy(tok_next_ref, r, 1 - slot).wait()
            scatter_copy(dst_prev_ref, r, 1 - slot).wait()
        for r in range(rows):
            scatter_copy(dst_ref, r, slot).start(priority=r % 2)
        for r in range(rows):
            scatter_copy(dst_ref, r, slot).wait()


def _moe(hf, layout, w_gate, w_up, w_down, layer):
    block_e, n_valid, first, wpar, wnext, row_tok, row_dst = layout
    t = hf.shape[0]
    d = w_gate.shape[2]
    f = w_gate.shape[3]
    rows = MOE_ROWS
    nb = block_e.shape[0]
    tok3 = row_tok.reshape(nb, 1, rows)
    spare = (TOP_K * t + jnp.arange(2 * rows, dtype=jnp.int32)).reshape(2, 1, rows)
    dst3 = jnp.concatenate([spare, row_dst.reshape(nb, 1, rows)], axis=0)
    smem_blk = lambda fn: pl.BlockSpec((1, 1, rows), fn, memory_space=pltpu.SMEM)
    hbm = pl.BlockSpec(memory_space=pl.ANY)
    grid_spec = pltpu.PrefetchScalarGridSpec(
        num_scalar_prefetch=5,
        grid=(nb,),
        in_specs=[
            smem_blk(lambda i, *_: (i, 0, 0)),
            smem_blk(lambda i, *_: (jnp.minimum(i + 1, nb - 1), 0, 0)),
            smem_blk(lambda i, *_: (0, 0, 0)),
            smem_blk(lambda i, *_: (i + 1, 0, 0)),
            smem_blk(lambda i, *_: (i + 2, 0, 0)),
            hbm, hbm, hbm, hbm,
        ],
        out_specs=hbm,
        scratch_shapes=[
            pltpu.VMEM((2, rows, d // 2), jnp.int32),
            pltpu.VMEM((2, rows, d // 2), jnp.int32),
            pltpu.VMEM((2, d, f), F32), pltpu.VMEM((2, d, f), F32), pltpu.VMEM((2, f, d), F32),
            pltpu.VMEM((d, f), BF16), pltpu.VMEM((d, f), BF16), pltpu.VMEM((f, d), BF16),
            pltpu.SemaphoreType.DMA((2,)),
            pltpu.SemaphoreType.DMA((2,)),
            pltpu.SemaphoreType.DMA((2,)),
        ],
    )
    return pl.pallas_call(
        functools.partial(_moe_kernel, layer=layer),
        grid_spec=grid_spec,
        out_shape=jax.ShapeDtypeStruct((TOP_K * t + 2 * rows, d // 2), jnp.int32),
        compiler_params=_cparams(("arbitrary",), has_side_effects=True, disable_bounds_checks=True),
        name="moe_experts",
    )(block_e, n_valid, first, wpar, wnext, tok3, tok3, dst3, dst3, dst3, hf, w_gate, w_up, w_down)


def _block_layout(route_i, counts_f):
    t = route_i.shape[1]
    rows = MOE_ROWS
    n_assign = TOP_K * t
    counts = counts_f[:, 0].astype(jnp.int32)
    blocks_per_e = (counts + rows - 1) // rows
    blk_end = jnp.cumsum(blocks_per_e)
    blk_start = blk_end - blocks_per_e
    flat_e = route_i[0:TOP_K].reshape(n_assign)
    dest = blk_start[flat_e] * rows + route_i[TOP_K:2 * TOP_K].reshape(n_assign)
    nb = n_assign // rows + N_EXPERTS
    n_rows = nb * rows
    assign = jnp.arange(n_assign, dtype=jnp.int32)
    ridx = jnp.arange(n_rows, dtype=jnp.int32)
    spare = n_assign + ((ridx // rows) % 2) * rows + ridx % rows
    row_dst = spare.at[dest].set(assign, unique_indices=True, mode='promise_in_bounds')
    row_tok = jnp.where(row_dst < n_assign, row_dst % t, 0)
    n_valid = blk_end[-1].astype(jnp.int32)
    bidx = jnp.arange(nb, dtype=jnp.int32)
    block_e = jnp.sum((blk_end[None, :] <= bidx[:, None]).astype(jnp.int32), axis=1)
    block_e = jnp.minimum(block_e, N_EXPERTS - 1)
    last_e = block_e[jnp.maximum(n_valid - 1, 0)]
    block_e = jnp.where(bidx < n_valid, block_e, last_e)
    prev_e = jnp.concatenate([jnp.full((1,), -1, jnp.int32), block_e[:-1]])
    first = ((bidx < n_valid) & (block_e != prev_e)).astype(jnp.int32)
    wpar = (jnp.cumsum(first) + 1) % 2
    first_at = jnp.where(first == 1, bidx, nb)
    later = jnp.concatenate([lax.cummin(first_at[::-1])[::-1][1:], jnp.full((1,), nb, jnp.int32)])
    wnext = jnp.where(later < nb, block_e[jnp.minimum(later, nb - 1)], -1).astype(jnp.int32)
    return block_e, n_valid.reshape(1), first, wpar.astype(jnp.int32), wnext, row_tok, row_dst


def _combine_kernel(x_ref, y0_ref, y1_ref, rw_ref, g_ref, o_ref):
    w = rw_ref[...]
    y0 = _unpack_bf16_pairs(y0_ref[...])
    y1 = _unpack_bf16_pairs(y1_ref[...])
    o_ref[...] = x_ref[...] + g_ref[0] * (y0 * w[:, 0:1] + y1 * w[:, 1:2])


def _combine(x2, y2, route_w, g_f, seq):
    t, d = x2.shape
    tm = 256
    per_b = seq // tm
    n_blk = t // tm
    return pl.pallas_call(
        _combine_kernel,
        grid=(n_blk,),
        in_specs=[pl.BlockSpec((tm, d), lambda i: (i, 0)),
                  pl.BlockSpec((tm, d // 2), lambda i: (i, 0)),
                  pl.BlockSpec((tm, d // 2), lambda i: (n_blk + i, 0)),
                  pl.BlockSpec((tm, 8), lambda i: (i, 0)),
                  pl.BlockSpec((1, 1, d), lambda i: (i // per_b, 0, 0))],
        out_specs=pl.BlockSpec((tm, d), lambda i: (i, 0)),
        out_shape=jax.ShapeDtypeStruct((t, d), F32),
        compiler_params=_cparams(("arbitrary",)),
        name="moe_combine",
    )(x2, y2, y2, route_w, g_f)


def kernel(x, c, norm_mix, norm_ffn, w_ada, b_ada, w_in, conv_w, a_log, dt_bias, gdn_norm, diff_q_norm, diff_k_norm, lam_q1, lam_k1, lam_q2, lam_k2, diff_subln, dil_q_norm, dil_k_norm, w_out, w_router, router_bias, w_gate, w_up, w_down):
    batch, seq, d = x.shape
    depth = w_ada.shape[0]
    t = batch * seq
    x2 = x.reshape(t, d)
    mod = _ada(c, w_ada, b_ada).reshape(depth, batch, N_MOD, 1, d)
    ba_lo = 3 * GDN_HEADS * LANES + GDN_HEADS * LANES
    ba_hi = ba_lo + 2 * GDN_HEADS
    w_router_t = w_router.T.astype(BF16)
    for l in range(depth):
        sh_a, sc_a, g_a, sh_f, sc_f, g_f = (mod[l, :, k] for k in range(N_MOD))
        w_main = jnp.concatenate([w_in[l, :, :ba_lo], w_in[l, :, ba_hi:]], axis=1).astype(BF16)
        w_ba = jnp.pad(w_in[l, :, ba_lo:ba_hi], ((0, 0), (0, LANES - 2 * GDN_HEADS))).astype(BF16)
        proj, ba = _inproj(x2, norm_mix[l].reshape(1, d), sc_a, sh_a, w_main, w_ba, seq)
        o_a = _gdn(proj, ba, conv_w[l], a_log[l], dt_bias[l], gdn_norm[l], batch, seq)
        lam_init = 0.8 - 0.6 * math.exp(-0.3 * l)
        lam_vecs = jnp.stack([lam_q1[l], lam_k1[l], lam_q2[l], lam_k2[l]]).astype(F32)
        o_b = _diff(proj, diff_q_norm[l], diff_k_norm[l], diff_subln[l], lam_vecs, lam_init, batch, seq)
        o_c = _dil(proj, dil_q_norm[l], dil_k_norm[l], batch, seq)
        x2, hf, route_i, route_w, counts = _outproj(o_a, o_b, o_c, w_out[l].astype(BF16), x2, g_a,
                                                    norm_ffn[l].reshape(1, d), sc_f, sh_f, w_router_t, router_bias, seq)
        y2 = _moe(hf, _block_layout(route_i, counts), w_gate, w_up, w_down, l)
        x2 = _combine(x2, y2, route_w, g_f, seq)
    return x2.reshape(batch, seq, d)
```

```python
import functools
import math

import jax
import jax.numpy as jnp
from jax import lax
from jax.experimental import pallas as pl
from jax.experimental.pallas import tpu as pltpu

F32 = jnp.float32
BF16 = jnp.bfloat16

LANES = 128
EPS = 1e-6
N_MOD = 6

GDN_HEADS = 6
GDN_CONV = 4
GDN_CHUNK = 256
GDN_LOCKSTEP = 8
DIFF_HEADS = 4
DIFF_QK_DIM = 64
DIL_HEADS = 6
DILATED_PAIRS = ((128, 1), (512, 4), (2048, 16))
DIL_BLOCK = 128
DIL_SPLIT = 4

N_EXPERTS = 64
N_GROUPS = 8
EXPERTS_PER_GROUP = N_EXPERTS // N_GROUPS
TOP_K = 2
MOE_ROWS = 256

SLAB_QA, SLAB_KA, SLAB_VA, SLAB_ZA = 0, 6, 12, 18
SLAB_QB, SLAB_KB, SLAB_VB = 24, 28, 32
SLAB_QC, SLAB_KC, SLAB_VC = 36, 42, 48
N_SLABS = 54

VMEM_LIMIT = 56 * 1024 * 1024


def _cparams(sem, vmem=VMEM_LIMIT, **kw):
    return pltpu.CompilerParams(dimension_semantics=sem, vmem_limit_bytes=vmem, **kw)


def _silu(v):
    return v * jax.nn.sigmoid(v)


def _dot(a, b):
    return jnp.dot(a, b, preferred_element_type=F32)


def _dot_nt(a, b):
    return lax.dot_general(a, b, (((1,), (1,)), ((), ())), preferred_element_type=F32)


def _dot_tn(a, b):
    return lax.dot_general(a, b, (((0,), (0,)), ((), ())), preferred_element_type=F32)


def _pack_bf16_pairs(v):
    half = v.shape[1] // 2
    lo = lax.bitcast_convert_type(v[:, :half].astype(F32), jnp.int32)
    hi = lax.bitcast_convert_type(v[:, half:].astype(F32), jnp.int32)
    return lax.shift_right_logical(lo, 16) | (hi & -65536)


def _unpack_bf16_pairs(w):
    return jnp.concatenate([lax.bitcast_convert_type(lax.shift_left(w, 16), F32),
                            lax.bitcast_convert_type(w & -65536, F32)], axis=1)


def _ada_kernel(c_ref, w_ref, b_ref, o_ref):
    cact = _silu(c_ref[...]).astype(BF16)
    o_ref[0] = _dot(cact, w_ref[0].astype(BF16)) + b_ref[0]


def _ada(c, w_ada, b_ada):
    depth, d, n = w_ada.shape
    b = c.shape[0]
    tn = 1024
    return pl.pallas_call(
        _ada_kernel,
        grid=(depth, n // tn),
        in_specs=[
            pl.BlockSpec((b, d), lambda l, j: (0, 0)),
            pl.BlockSpec((1, d, tn), lambda l, j: (l, 0, j)),
            pl.BlockSpec((1, 1, tn), lambda l, j: (l, 0, j)),
        ],
        out_specs=pl.BlockSpec((1, b, tn), lambda l, j: (l, 0, j)),
        out_shape=jax.ShapeDtypeStruct((depth, b, n), F32),
        compiler_params=_cparams(("arbitrary", "arbitrary")),
        name="ada_mod",
    )(c, w_ada, b_ada.reshape(depth, 1, n))


def _modulated_norm(x, nw, sc, sh):
    ms = jnp.mean(x * x, axis=-1, keepdims=True)
    return (x * lax.rsqrt(ms + EPS) * nw) * (1.0 + sc) + sh


def _inproj_kernel(x_ref, nw_ref, sc_ref, sh_ref, w_ref, wba_ref, o_ref, ba_ref, h_scr, *, n_sub):
    @pl.when(pl.program_id(1) == 0)
    def _():
        h = _modulated_norm(x_ref[...], nw_ref[...], sc_ref[0], sh_ref[0]).astype(BF16)
        h_scr[...] = h
        ba_ref[...] = _dot(h, wba_ref[...])

    acc = _dot(h_scr[...], w_ref[...])
    for k in range(n_sub):
        o_ref[k] = acc[:, k * LANES:(k + 1) * LANES].astype(BF16)


def _inproj(x2, nw, sc, sh, w_main, w_ba, seq):
    t, d = x2.shape
    n = w_main.shape[1]
    tm = min(1024, seq)
    tn = 768
    n_sub = tn // LANES
    per_b = seq // tm
    return pl.pallas_call(
        functools.partial(_inproj_kernel, n_sub=n_sub),
        grid=(t // tm, n // tn),
        in_specs=[
            pl.BlockSpec((tm, d), lambda i, j: (i, 0)),
            pl.BlockSpec((1, d), lambda i, j: (0, 0)),
            pl.BlockSpec((1, 1, d), lambda i, j: (i // per_b, 0, 0)),
            pl.BlockSpec((1, 1, d), lambda i, j: (i // per_b, 0, 0)),
            pl.BlockSpec((d, tn), lambda i, j: (0, j)),
            pl.BlockSpec((d, LANES), lambda i, j: (0, 0)),
        ],
        out_specs=[
            pl.BlockSpec((n_sub, tm, LANES), lambda i, j: (j, i, 0)),
            pl.BlockSpec((tm, LANES), lambda i, j: (i, 0)),
        ],
        out_shape=[
            jax.ShapeDtypeStruct((n // LANES, t, LANES), BF16),
            jax.ShapeDtypeStruct((t, LANES), F32),
        ],
        scratch_shapes=[pltpu.VMEM((tm, d), BF16)],
        compiler_params=_cparams(("arbitrary", "arbitrary")),
        name="inproj",
    )(x2, nw, sc, sh, w_main, w_ba)


def _unit_lower_inverse(lms, xr):
    c = lms[0].shape[0]

    def mm(a, b):
        return [_dot(ai.astype(BF16), bi.astype(BF16)) for ai, bi in zip(a, b)]

    d1 = [jnp.where(xr < 16, lm, 0.0) for lm in lms]
    d2 = mm(d1, d1)
    d4 = mm(d2, d2)
    d8 = mm(d4, d4)
    p = [b - a - t for a, b, t in zip(d1, d2, mm(d1, d2))]
    p = [pi + di + t for pi, di, t in zip(p, d4, mm(p, d4))]
    p = [pi + di + t for pi, di, t in zip(p, d8, mm(p, d8))]
    eye = jnp.where(xr == 0, 1.0, 0.0)
    x = [eye + pi for pi in p]
    blk = 32
    while blk <= c:
        e = [jnp.where((xr < blk) & (xr >= blk // 2), lm, 0.0) for lm in lms]
        x = [xi - t for xi, t in zip(x, mm(x, mm(e, x)))]
        blk *= 2
    return x


def _gdn_kernel(q_ref, k_ref, v_ref, z_ref, ba_ref, cwq_ref, cwk_ref, cwv_ref, al_ref, dtb_ref, nw_ref,
                o_ref, xq, xk, xv, u_s, w_s, a_s, qd_s, kd_s, cd_s, *, seq, chunk):
    c = chunk
    n_chunks = seq // c
    head = pl.program_id(1)
    dk = LANES

    lane1 = lax.broadcasted_iota(jnp.int32, (1, LANES), 1)
    a_exp = jnp.exp(jnp.sum(jnp.where(lane1 == head, al_ref[...], 0.0), axis=-1, keepdims=True))
    dtb = jnp.sum(jnp.where(lane1 == head, dtb_ref[...], 0.0), axis=-1, keepdims=True)

    zeros8 = jnp.zeros((8, LANES), F32)
    for src, dst in ((q_ref, xq), (k_ref, xk), (v_ref, xv)):
        dst[0:8, :] = zeros8

        def stage(i, carry, src=src, dst=dst):
            r0 = pl.multiple_of(i * c, c)
            dst[pl.ds(r0 + 8, c), :] = src[0, pl.ds(r0, c), :].astype(F32)
            return carry

        lax.fori_loop(0, n_chunks, stage, 0)

    row = lax.broadcasted_iota(jnp.int32, (c, c), 0)
    col = lax.broadcasted_iota(jnp.int32, (c, c), 1)
    causal = row >= col
    strict = row > col
    xr = row ^ col
    tril = jnp.where(causal, 1.0, 0.0).astype(BF16)
    lane_c = lax.broadcasted_iota(jnp.int32, (c, LANES), 1)

    def conv(xs, cw_ref, r0):
        acc = cw_ref[0, GDN_CONV - 1:GDN_CONV, :] * xs[pl.ds(r0 + 8, c), :]
        for back in range(1, GDN_CONV):
            tap = GDN_CONV - 1 - back
            acc = acc + cw_ref[0, tap:tap + 1, :] * xs[pl.ds(r0 + 8 - back, c), :]
        return _silu(acc)

    def l2n(t):
        return t * lax.rsqrt(jnp.sum(t * t, axis=-1, keepdims=True) + EPS)

    def log_decay(r0):
        bat = ba_ref[pl.ds(r0, c), :]
        bcol = jnp.sum(jnp.where(lane_c == head, bat, 0.0), axis=-1, keepdims=True)
        acol = jnp.sum(jnp.where(lane_c == head + GDN_HEADS, bat, 0.0), axis=-1, keepdims=True)
        xsp = acol + dtb
        softplus = jnp.maximum(xsp, 0.0) + jnp.log1p(jnp.exp(-jnp.abs(xsp)))
        return jax.nn.sigmoid(bcol), jnp.broadcast_to(-a_exp * softplus, (c, LANES))

    def intra(step, carry):
        par = range(GDN_LOCKSTEP)
        ns = [step * GDN_LOCKSTEP + j for j in par]
        r0s = [pl.multiple_of(n * c, c) for n in ns]
        qn = [l2n(conv(xq, cwq_ref, r0)) * (dk ** -0.5) for r0 in r0s]
        kn = [l2n(conv(xk, cwk_ref, r0)) for r0 in r0s]
        vc = [conv(xv, cwv_ref, r0) for r0 in r0s]
        bg = [log_decay(r0) for r0 in r0s]
        beta = [t[0] for t in bg]
        g_hi = [t[1].astype(BF16) for t in bg]
        g_lo = [(t[1] - h.astype(F32)).astype(BF16) for t, h in zip(bg, g_hi)]
        gc = [_dot(tril, h) + _dot(tril, lo_) for h, lo_ in zip(g_hi, g_lo)]
        decay = []
        for gci in gc:
            gc_row = jnp.transpose(gci)[0:1, :]
            diff = jnp.concatenate([gci] * (c // LANES), axis=1) - gc_row
            decay.append(jnp.where(causal, jnp.exp(jnp.where(causal, diff, 0.0)), 0.0))
        kb = [k * b for k, b in zip(kn, beta)]
        knb = [k.astype(BF16) for k in kn]
        lm = [jnp.where(strict, _dot_nt(kbi.astype(BF16), ki) * dc, 0.0) for kbi, ki, dc in zip(kb, knb, decay)]
        attn = [jnp.where(causal, _dot_nt(q.astype(BF16), ki) * dc, 0.0) for q, ki, dc in zip(qn, knb, decay)]
        tinv = _unit_lower_inverse(lm, xr)
        egc = [jnp.exp(g) for g in gc]
        sol = [_dot(ti.astype(BF16), jnp.concatenate([v * b, kbi * e], axis=1).astype(BF16))
               for ti, v, b, kbi, e in zip(tinv, vc, beta, kb, egc)]
        for j in par:
            rows = pl.ds(r0s[j], c)
            u_s[rows, :] = sol[j][:, :LANES]
            w_s[rows, :] = sol[j][:, LANES:].astype(BF16)
            a_s[rows, :] = attn[j].astype(BF16)
            qd_s[rows, :] = (qn[j] * egc[j]).astype(BF16)
            g_last = gc[j][c - 1:c, :]
            kd_s[rows, :] = (kn[j] * jnp.exp(g_last - gc[j])).astype(BF16)
            cd_s[pl.ds(pl.multiple_of(ns[j] * 8, 8), 8), :] = jnp.broadcast_to(jnp.exp(g_last), (8, LANES))
        return carry

    lax.fori_loop(0, n_chunks // GDN_LOCKSTEP, intra, 0)

    def inter(n, state):
        r0 = pl.multiple_of(n * c, c)
        rows = pl.ds(r0, c)
        sb = state.astype(BF16)
        v_new = u_s[rows, :] - _dot(w_s[rows, :], sb)
        vb = v_new.astype(BF16)
        o = _dot(qd_s[rows, :], sb) + _dot(a_s[rows, :], vb)
        cd = cd_s[pl.ds(pl.multiple_of(n * 8, 8), 1), :]
        new_state = state * cd + _dot_tn(kd_s[rows, :], vb)
        on = o * lax.rsqrt(jnp.mean(o * o, axis=-1, keepdims=True) + EPS) * nw_ref[...]
        z = z_ref[0, rows, :].astype(F32)
        o_ref[0, rows, :] = (on * _silu(z)).astype(BF16)
        return new_state

    lax.fori_loop(0, n_chunks, inter, jnp.zeros((dk, LANES), F32), unroll=4)


def _gdn(proj, ba, conv_w, a_log, dt_bias, gdn_norm, batch, seq):
    c = GDN_CHUNK
    per = seq
    cw = conv_w.reshape(GDN_CONV, 3 * GDN_HEADS, LANES).transpose(1, 0, 2)
    pad = lambda v: jnp.pad(v, (0, LANES - v.shape[0])).reshape(1, LANES)

    def slab(base):
        return pl.BlockSpec((1, per, LANES), lambda b, h, base=base: (base + h, b, 0))

    def cwspec(base):
        return pl.BlockSpec((1, GDN_CONV, LANES), lambda b, h, base=base: (base + h, 0, 0))

    vec = pl.BlockSpec((1, LANES), lambda b, h: (0, 0))
    return pl.pallas_call(
        functools.partial(_gdn_kernel, seq=seq, chunk=c),
        grid=(batch, GDN_HEADS),
        in_specs=[slab(SLAB_QA), slab(SLAB_KA), slab(SLAB_VA), slab(SLAB_ZA),
                  pl.BlockSpec((per, LANES), lambda b, h: (b, 0)),
                  cwspec(0), cwspec(GDN_HEADS), cwspec(2 * GDN_HEADS), vec, vec, vec],
        out_specs=pl.BlockSpec((1, per, LANES), lambda b, h: (h, b, 0)),
        out_shape=jax.ShapeDtypeStruct((GDN_HEADS, batch * seq, LANES), BF16),
        scratch_shapes=[
            pltpu.VMEM((seq + 8, LANES), F32), pltpu.VMEM((seq + 8, LANES), F32),
            pltpu.VMEM((seq + 8, LANES), F32),
            pltpu.VMEM((seq, LANES), F32),
            pltpu.VMEM((seq, LANES), BF16),
            pltpu.VMEM((seq, c), BF16),
            pltpu.VMEM((seq, LANES), BF16),
            pltpu.VMEM((seq, LANES), BF16),
            pltpu.VMEM((seq // c * 8, LANES), F32),
        ],
        compiler_params=_cparams(("arbitrary", "arbitrary")),
        name="gdn_mixer",
    )(proj, proj, proj, proj, ba, cw, cw, cw, pad(a_log), pad(dt_bias), gdn_norm.reshape(1, LANES))


def _diff_kernel(q_ref, k_ref, v_ref, qw_ref, kw_ref, sw_ref, lam_ref, o_ref,
                 q1_s, q2_s, kn_s, ve_s, sc1, sc2, acc1, acc2, m1, m2, *, seq, tq, tk, lam_init):
    dh = DIFF_QK_DIM
    n_blk = seq // tq
    per_q = tq // tk
    lane = lax.broadcasted_iota(jnp.int32, (1, LANES), 1)
    lo = lane < dh

    lam = (jnp.exp(jnp.sum(lam_ref[0:1, :] * lam_ref[1:2, :], axis=-1, keepdims=True))
           - jnp.exp(jnp.sum(lam_ref[2:3, :] * lam_ref[3:4, :], axis=-1, keepdims=True)) + lam_init)

    def halfnorm(t, w):
        sq = t * t
        s_lo = jnp.sum(jnp.where(lo, sq, 0.0), axis=-1, keepdims=True)
        s_hi = jnp.sum(jnp.where(lo, 0.0, sq), axis=-1, keepdims=True)
        r = jnp.where(lo, lax.rsqrt(s_lo / dh + EPS), lax.rsqrt(s_hi / dh + EPS))
        return t * r * w

    ones_blk = jnp.ones((tk, LANES), BF16)

    def prep(i, carry):
        rows = pl.ds(pl.multiple_of(i * tk, tk), tk)
        kn_s[rows, :] = halfnorm(k_ref[0, rows, :].astype(F32), kw_ref[...]).astype(BF16)
        qn = halfnorm(q_ref[0, rows, :].astype(F32), qw_ref[...]) * (dh ** -0.5)
        q1_s[rows, :] = jnp.where(lo, qn, 0.0).astype(BF16)
        q2_s[rows, :] = jnp.where(lo, 0.0, qn).astype(BF16)
        ve_s[rows, 0:LANES] = v_ref[0, rows, :]
        ve_s[rows, LANES:2 * LANES] = ones_blk
        return carry

    lax.fori_loop(0, seq // tk, prep, 0)

    row = lax.broadcasted_iota(jnp.int32, (tq, tk), 0)
    col = lax.broadcasted_iota(jnp.int32, (tq, tk), 1)

    n_parts = tk // LANES

    def q_block(qi, carry):
        qrows = pl.ds(pl.multiple_of(qi * tq, tq), tq)
        maps = ((q1_s[qrows, :], sc1, m1, acc1), (q2_s[qrows, :], sc2, m2, acc2))
        for _, _, m_r, a_r in maps:
            m_r[...] = jnp.full((tq, LANES), -jnp.inf, F32)
            a_r[...] = jnp.zeros((tq, 2 * LANES), F32)

        def score_step(kj, masked):
            kb = kn_s[pl.ds(pl.multiple_of(kj * tk, tk), tk), :]
            for qq, s_c, m_r, _ in maps:
                s = _dot_nt(qq, kb)
                if masked:
                    s = jnp.where(row + qi * tq >= col + kj * tk, s, -jnp.inf)
                s_c[kj] = s
                smax = m_r[...]
                for part in range(n_parts):
                    smax = jnp.maximum(smax, s[:, part * LANES:(part + 1) * LANES])
                m_r[...] = smax

        def off_diag(kq, c2):
            for part in range(per_q):
                score_step(kq * per_q + part, False)
            return c2

        lax.fori_loop(0, qi, off_diag, 0)
        for dblk in range(per_q):
            score_step(qi * per_q + dblk, True)
        for _, _, m_r, _ in maps:
            m_r[...] = jnp.broadcast_to(jnp.max(m_r[...], axis=-1, keepdims=True), (tq, LANES))

        def value_step(kq, c2):
            for _, s_c, m_r, a_r in maps:
                mrep = m_r[...]
                upd = a_r[...]
                for sub in range(per_q):
                    kj = kq * per_q + sub
                    s = s_c[kj]
                    p = jnp.concatenate(
                        [jnp.exp(s[:, part * LANES:(part + 1) * LANES] - mrep) for part in range(n_parts)], axis=1)
                    upd = upd + _dot(p.astype(BF16), ve_s[pl.ds(pl.multiple_of(kj * tk, tk), tk), :])
                a_r[...] = upd
            return c2

        lax.fori_loop(0, qi + 1, value_step, 0)
        o = (acc1[:, 0:LANES] / acc1[:, LANES:2 * LANES]
             - lam * (acc2[:, 0:LANES] / acc2[:, LANES:2 * LANES]))
        o = o * lax.rsqrt(jnp.mean(o * o, axis=-1, keepdims=True) + EPS) * sw_ref[...] * (1.0 - lam_init)
        o_ref[0, qrows, :] = o.astype(BF16)
        return carry

    lax.fori_loop(0, n_blk, q_block, 0)


def _diff(proj, q_norm, k_norm, subln, lam_vecs, lam_init, batch, seq):
    tq = 512
    tk = 256
    per = seq

    def slab(base):
        return pl.BlockSpec((1, per, LANES), lambda b, h, base=base: (base + h, b, 0))

    vec = pl.BlockSpec((1, LANES), lambda b, h: (0, 0))
    tile2 = lambda w: jnp.concatenate([w, w]).reshape(1, LANES)
    return pl.pallas_call(
        functools.partial(_diff_kernel, seq=seq, tq=tq, tk=tk, lam_init=lam_init),
        grid=(batch, DIFF_HEADS),
        in_specs=[slab(SLAB_QB), slab(SLAB_KB), slab(SLAB_VB), vec, vec, vec,
                  pl.BlockSpec((4, DIFF_QK_DIM), lambda b, h: (0, 0))],
        out_specs=pl.BlockSpec((1, per, LANES), lambda b, h: (h, b, 0)),
        out_shape=jax.ShapeDtypeStruct((DIFF_HEADS, batch * seq, LANES), BF16),
        scratch_shapes=[
            pltpu.VMEM((seq, LANES), BF16), pltpu.VMEM((seq, LANES), BF16), pltpu.VMEM((seq, LANES), BF16),
            pltpu.VMEM((seq, 2 * LANES), BF16),
            pltpu.VMEM((seq // tk, tq, tk), F32), pltpu.VMEM((seq // tk, tq, tk), F32),
            pltpu.VMEM((tq, 2 * LANES), F32), pltpu.VMEM((tq, 2 * LANES), F32),
            pltpu.VMEM((tq, LANES), F32), pltpu.VMEM((tq, LANES), F32),
        ],
        compiler_params=_cparams(("arbitrary", "arbitrary")),
        name="diff_mixer",
    )(proj, proj, proj, tile2(q_norm), tile2(k_norm), subln.reshape(1, LANES), lam_vecs)


def _dil_kernel(q_ref, k_ref, v_ref, qw_ref, kw_ref, o_ref, qn_s, kn_s, vf_s, q4_s, k4_s, v4_s, sc_s, mb_s,
                acc_s, l_s, m_s, acc4_s, l4_s, m4_s, *, seq):
    blk = DIL_BLOCK
    dh = LANES
    pc = 256
    n_pc = seq // pc
    unroll = 8

    def rms(t, w):
        return t * lax.rsqrt(jnp.mean(t * t, axis=-1, keepdims=True) + EPS) * w

    ones_kv = jnp.ones((2 * blk, LANES), BF16)

    def prep(i, carry):
        rows = pl.ds(pl.multiple_of(i * pc, pc), pc)
        qn_s[rows, :] = rms(q_ref[0, rows, :].astype(F32), qw_ref[...]) * (dh ** -0.5)
        kn_s[rows, :] = rms(k_ref[0, rows, :].astype(F32), kw_ref[...])
        vf_s[rows, :] = v_ref[0, rows, :].astype(F32)
        return carry

    lax.fori_loop(0, n_pc, prep, 0)

    sp = DIL_SPLIT
    cls_rows = seq // sp
    cps = cls_rows // pc

    def chunk_slices(i):
        c4 = i // cps
        c = i - c4 * cps
        return (pl.ds(c4 + sp * pc * c, pc, stride=sp),
                pl.ds(pl.multiple_of(c4 * cls_rows + pc * c, pc), pc))

    def split(i, carry):
        nat, spl = chunk_slices(i)
        q4_s[spl, :] = qn_s[nat, :]
        k4_s[spl, :] = kn_s[nat, :]
        v4_s[spl, :] = vf_s[nat, :]
        return carry

    lax.fori_loop(0, sp * cps, split, 0)

    qi = lax.broadcasted_iota(jnp.int32, (blk, 2 * blk), 0)
    kj = lax.broadcasted_iota(jnp.int32, (blk, 2 * blk), 1)
    seen_layouts = set()

    for window, dil in DILATED_PAIRS:
        steps = window // dil
        n_sub = seq // dil
        nb = n_sub // blk
        assert steps == blk and nb >= 2 and (dil * nb) % unroll == 0
        natural = dil == 1
        assert natural or dil % sp == 0
        first = natural not in seen_layouts
        seen_layouts.add(natural)
        qsrc, ksrc, vsrc, acc_r, l_r, m_r = ((qn_s, kn_s, vf_s, acc_s, l_s, m_s) if natural
                                             else (q4_s, k4_s, v4_s, acc4_s, l4_s, m4_s))

        def slices(idx, dil=dil, nb=nb, natural=natural):
            r = idx // nb
            n = idx - r * nb
            kbase = jnp.maximum(n - 1, 0) * blk
            if natural:
                return n, kbase, pl.ds(pl.multiple_of(blk * n, blk), blk), pl.ds(pl.multiple_of(kbase, blk), 2 * blk)
            d2 = dil // sp
            r_hi = r // sp
            off = (r - r_hi * sp) * cls_rows + r_hi
            if d2 == 1:
                return (n, kbase, pl.ds(pl.multiple_of(off + blk * n, blk), blk),
                        pl.ds(pl.multiple_of(off + kbase, blk), 2 * blk))
            return (n, kbase, pl.ds(off + d2 * blk * n, blk, stride=d2),
                    pl.ds(off + d2 * kbase, 2 * blk, stride=d2))

        def score_block(idx, carry, slices=slices, qsrc=qsrc, ksrc=ksrc):
            n, kbase, qsl, ksl = slices(idx)
            s = _dot_nt(qsrc[qsl, :].astype(BF16), ksrc[ksl, :].astype(BF16))
            dist = (n * blk + qi) - (kbase + kj)
            s = jnp.where((dist >= 0) & (dist <= steps), s, -jnp.inf)
            sc_s[idx] = s
            m = jnp.max(jnp.maximum(s[:, 0:blk], s[:, blk:2 * blk]), axis=-1, keepdims=True)
            mb_s[idx] = jnp.broadcast_to(m, (blk, LANES))
            return carry

        lax.fori_loop(0, dil * nb, score_block, 0, unroll=unroll)

        def value_block(idx, carry, slices=slices, first=first, vsrc=vsrc, acc_r=acc_r, l_r=l_r, m_r=m_r):
            _, _, qsl, ksl = slices(idx)
            s = sc_s[idx]
            m = mb_s[idx]
            p = jnp.concatenate([jnp.exp(s[:, 0:blk] - m), jnp.exp(s[:, blk:2 * blk] - m)], axis=1)
            ve = jnp.concatenate([vsrc[ksl, :].astype(BF16), ones_kv], axis=1)
            ol = _dot(p.astype(BF16), ve)
            o = ol[:, 0:LANES]
            l = ol[:, LANES:2 * LANES]
            if first:
                acc_r[qsl, :] = o
                l_r[qsl, :] = l
                m_r[qsl, :] = m
            else:
                m_old = m_r[qsl, :]
                m_new = jnp.maximum(m_old, m)
                a_old = jnp.exp(m_old - m_new)
                a_cur = jnp.exp(m - m_new)
                acc_r[qsl, :] = a_old * acc_r[qsl, :] + a_cur * o
                l_r[qsl, :] = a_old * l_r[qsl, :] + a_cur * l
                m_r[qsl, :] = m_new
            return carry

        lax.fori_loop(0, dil * nb, value_block, 0, unroll=unroll)

    assert seen_layouts == {True, False}

    def merge(i, carry):
        nat, spl = chunk_slices(i)
        m_nat = m_s[nat, :]
        m_spl = m4_s[spl, :]
        m_new = jnp.maximum(m_nat, m_spl)
        a_nat = jnp.exp(m_nat - m_new)
        a_spl = jnp.exp(m_spl - m_new)
        acc_s[nat, :] = ((a_nat * acc_s[nat, :] + a_spl * acc4_s[spl, :])
                         / (a_nat * l_s[nat, :] + a_spl * l4_s[spl, :]))
        return carry

    lax.fori_loop(0, sp * cps, merge, 0)

    def fin(i, carry):
        rows = pl.ds(pl.multiple_of(i * pc, pc), pc)
        o_ref[0, rows, :] = acc_s[rows, :].astype(BF16)
        return carry

    lax.fori_loop(0, n_pc, fin, 0)


def _dil(proj, q_norm, k_norm, batch, seq):
    per = seq

    def slab(base):
        return pl.BlockSpec((1, per, LANES), lambda b, h, base=base: (base + h, b, 0))

    vec = pl.BlockSpec((1, LANES), lambda b, h: (0, 0))
    row = lambda: pltpu.VMEM((seq, LANES), F32)
    return pl.pallas_call(
        functools.partial(_dil_kernel, seq=seq),
        grid=(batch, DIL_HEADS),
        in_specs=[slab(SLAB_QC), slab(SLAB_KC), slab(SLAB_VC), vec, vec],
        out_specs=pl.BlockSpec((1, per, LANES), lambda b, h: (h, b, 0)),
        out_shape=jax.ShapeDtypeStruct((DIL_HEADS, batch * seq, LANES), BF16),
        scratch_shapes=[
            row(), row(), row(),
            row(), row(), row(),
            pltpu.VMEM((seq // DIL_BLOCK, DIL_BLOCK, 2 * DIL_BLOCK), F32),
            pltpu.VMEM((seq // DIL_BLOCK, DIL_BLOCK, LANES), F32),
            row(), row(), row(),
            row(), row(), row(),
        ],
        compiler_params=_cparams(("arbitrary", "arbitrary")),
        name="dil_mixer",
    )(proj, proj, proj, q_norm.reshape(1, LANES), k_norm.reshape(1, LANES))


def _route_tile(sc, bias, run):
    ne, tm = sc.shape
    epg = EXPERTS_PER_GROUP
    sel = sc + bias
    jrow = lax.broadcasted_iota(jnp.int32, (epg, tm), 0)
    gs, i1s, i2s = [], [], []
    for g in range(N_GROUPS):
        sg = sel[g * epg:(g + 1) * epg, :]
        m1 = jnp.max(sg, axis=0, keepdims=True)
        i1 = jnp.min(jnp.where(sg == m1, jrow, epg), axis=0, keepdims=True)
        rest = jnp.where(jrow == i1, -jnp.inf, sg)
        m2 = jnp.max(rest, axis=0, keepdims=True)
        i2 = jnp.min(jnp.where(rest == m2, jrow, epg), axis=0, keepdims=True)
        gs.append(m1 + m2)
        i1s.append(i1)
        i2s.append(i2)
    gmax = functools.reduce(jnp.maximum, gs)
    gidx = jnp.full((1, tm), N_GROUPS - 1, jnp.int32)
    for g in reversed(range(N_GROUPS - 1)):
        gidx = jnp.where(gs[g] == gmax, g, gidx)
    loc1 = jnp.zeros((1, tm), jnp.int32)
    loc2 = jnp.zeros((1, tm), jnp.int32)
    for g in range(N_GROUPS):
        loc1 = jnp.where(gidx == g, i1s[g], loc1)
        loc2 = jnp.where(gidx == g, i2s[g], loc2)
    e1 = gidx * epg + loc1
    e2 = gidx * epg + loc2
    erow = lax.broadcasted_iota(jnp.int32, (ne, tm), 0)
    oh1 = erow == e1
    oh2 = erow == e2
    s1 = jnp.sum(jnp.where(oh1, sc, 0.0), axis=0, keepdims=True)
    s2 = jnp.sum(jnp.where(oh2, sc, 0.0), axis=0, keepdims=True)
    den = s1 + s2
    tr = lax.broadcasted_iota(jnp.int32, (tm, tm), 0)
    tc = lax.broadcasted_iota(jnp.int32, (tm, tm), 1)
    before = jnp.where(tr < tc, 1.0, 0.0).astype(BF16)
    f1 = jnp.where(oh1, 1.0, 0.0)
    f2 = jnp.where(oh2, 1.0, 0.0)
    pre1 = _dot(f1.astype(BF16), before)
    pre2 = _dot(f2.astype(BF16), before)
    tot1 = jnp.sum(f1, axis=1, keepdims=True)
    tot2 = jnp.sum(f2, axis=1, keepdims=True)
    r1 = jnp.sum(jnp.where(oh1, pre1 + run, 0.0), axis=0, keepdims=True)
    r2 = jnp.sum(jnp.where(oh2, pre2 + (run + tot1), 0.0), axis=0, keepdims=True)
    return e1, e2, s1 / den, s2 / den, r1, r2, run + tot1 + tot2


def _outproj_kernel(oa_ref, ob_ref, oc_ref, w_ref, x_ref, g_ref, nw_ref, sc_ref, sh_ref, wrt_ref, rb_ref,
                    xo_ref, hf_ref, ri_ref, rw_ref, cnt_ref, mix_s, y_s, run_s):
    i = pl.program_id(0)

    @pl.when(i == 0)
    def _():
        run_s[...] = jnp.zeros(run_s.shape, F32)
        y_s[...] = jnp.zeros(y_s.shape, F32)

    y_prev = y_s[(i + 1) % 2]

    off = 0
    for ref, heads in ((oa_ref, GDN_HEADS), (ob_ref, DIFF_HEADS), (oc_ref, DIL_HEADS)):
        for k in range(heads):
            mix_s[:, off:off + LANES] = ref[k]
            off += LANES
    y_s[i % 2] = _dot(mix_s[...], w_ref[...])

    xn = x_ref[...] + g_ref[0] * y_prev
    xo_ref[...] = xn
    hb = _modulated_norm(xn, nw_ref[...], sc_ref[0], sh_ref[0]).astype(BF16)
    hf_ref[...] = _pack_bf16_pairs(hb)
    scores_t = jax.nn.sigmoid(_dot_nt(wrt_ref[...], hb))
    run_in = run_s[...]
    e1, e2, w1, w2, r1, r2, run = _route_tile(scores_t, rb_ref[...], run_in)
    run = jnp.where(i >= 1, run, run_in)
    run_s[...] = run
    cnt_ref[...] = jnp.broadcast_to(run, cnt_ref.shape)
    tm = e1.shape[1]
    ri_ref[...] = jnp.concatenate([e1, e2, r1.astype(jnp.int32), r2.astype(jnp.int32),
                                   jnp.zeros((4, tm), jnp.int32)], axis=0)
    rw_ref[...] = jnp.transpose(jnp.concatenate([w1, w2, jnp.zeros((6, tm), F32)], axis=0))


def _outproj(o_a, o_b, o_c, w_out, x2, g_a, nw, sc, sh, w_router_t, router_bias, seq):
    t, d = x2.shape
    tm = 256
    per_b = seq // tm
    ne = w_router_t.shape[0]
    n_tiles = t // tm
    proj_tile = lambda i: jnp.minimum(i, n_tiles - 1)
    done_tile = lambda i: jnp.maximum(i - 1, 0)
    row = pl.BlockSpec((tm, d), lambda i: (done_tile(i), 0))
    bvec = pl.BlockSpec((1, 1, d), lambda i: (done_tile(i) // per_b, 0, 0))
    return pl.pallas_call(
        _outproj_kernel,
        grid=(n_tiles + 1,),
        in_specs=[
            pl.BlockSpec((GDN_HEADS, tm, LANES), lambda i: (0, proj_tile(i), 0)),
            pl.BlockSpec((DIFF_HEADS, tm, LANES), lambda i: (0, proj_tile(i), 0)),
            pl.BlockSpec((DIL_HEADS, tm, LANES), lambda i: (0, proj_tile(i), 0)),
            pl.BlockSpec((d, d), lambda i: (0, 0)),
            row, bvec,
            pl.BlockSpec((1, d), lambda i: (0, 0)),
            bvec, bvec,
            pl.BlockSpec((ne, d), lambda i: (0, 0)),
            pl.BlockSpec((ne, 1), lambda i: (0, 0)),
        ],
        out_specs=[row,
                   pl.BlockSpec((tm, d // 2), lambda i: (done_tile(i), 0)),
                   pl.BlockSpec((8, tm), lambda i: (0, done_tile(i))),
                   pl.BlockSpec((tm, 8), lambda i: (done_tile(i), 0)),
                   pl.BlockSpec((ne, LANES), lambda i: (0, 0))],
        out_shape=[jax.ShapeDtypeStruct((t, d), F32), jax.ShapeDtypeStruct((t, d // 2), jnp.int32),
                   jax.ShapeDtypeStruct((8, t), jnp.int32), jax.ShapeDtypeStruct((t, 8), F32),
                   jax.ShapeDtypeStruct((ne, LANES), F32)],
        scratch_shapes=[pltpu.VMEM((tm, d), BF16), pltpu.VMEM((2, tm, d), F32), pltpu.VMEM((ne, 1), F32)],
        compiler_params=_cparams(("arbitrary",)),
        name="outproj",
    )(o_a, o_b, o_c, w_out, x2, g_a, nw, sc, sh, w_router_t, router_bias.reshape(ne, 1).astype(F32))


def _moe_kernel(be_ref, nv_ref, first_ref, wpar_ref, wnext_ref, tok_ref, tok_next_ref, dst_spare_ref,
                dst_prev_ref, dst_ref, hf_hbm, wg_hbm, wu_hbm, wd_hbm, y_hbm,
                xbuf, ybuf, wg_f, wu_f, wd_f, wg_s, wu_s, wd_s, gsem, ssem, wsem, *, layer):
    i = pl.program_id(0)
    nv = nv_ref[0]
    slot = i % 2
    rows = MOE_ROWS

    def gather_copy(tok_r, r, s):
        return pltpu.make_async_copy(hf_hbm.at[pl.ds(tok_r[0, 0, r], 1)], xbuf.at[s, pl.ds(r, 1)], gsem.at[s])

    def scatter_copy(dst_r, r, s):
        return pltpu.make_async_copy(ybuf.at[s, pl.ds(r, 1)], y_hbm.at[pl.ds(dst_r[0, 0, r], 1)], ssem.at[s])

    def weight_copies(e, par):
        return (pltpu.make_async_copy(wg_hbm.at[layer, e], wg_f.at[par], wsem.at[par]),
                pltpu.make_async_copy(wu_hbm.at[layer, e], wu_f.at[par], wsem.at[par]),
                pltpu.make_async_copy(wd_hbm.at[layer, e], wd_f.at[par], wsem.at[par]))

    @pl.when(i == 0)
    def _():
        ybuf[...] = jnp.zeros(ybuf.shape, jnp.int32)
        for r in range(rows):
            gather_copy(tok_ref, r, 0).start()
        for r in range(rows):
            scatter_copy(dst_spare_ref, r, 0).start(priority=r % 2)
        for c in weight_copies(be_ref[0], wpar_ref[0]):
            c.start(priority=1)

    @pl.when(first_ref[i] == 1)
    def _():
        par = wpar_ref[i]
        for c in weight_copies(be_ref[i], par):
            c.wait()

        @pl.when(wnext_ref[i] >= 0)
        def _():
            for c in weight_copies(wnext_ref[i], 1 - par):
                c.start(priority=1)

        wg_s[...] = wg_f[par].astype(BF16)
        wu_s[...] = wu_f[par].astype(BF16)
        wd_s[...] = wd_f[par].astype(BF16)

    @pl.when(i < nv)
    def _():
        for r in range(rows):
            gather_copy(tok_ref, r, slot).wait()
        x = _unpack_bf16_pairs(xbuf[slot]).astype(BF16)
        for r in range(rows):
            gather_copy(tok_next_ref, r, 1 - slot).start()
        for r in range(rows):
            scatter_copy(dst_prev_ref, r, 1 - slot).start(priority=r % 2)
        a = _dot(x, wg_s[...])
        u = _dot(x, wu_s[...])
        y = _pack_bf16_pairs(_dot((_silu(a) * u).astype(BF16), wd_s[...]).astype(BF16))

        for r in range(rows):
            scatter_copy(dst_ref, r, slot).wait()
        ybuf[slot] = y

    @pl.when(i == nv - 1)
    def _():
        for r in range(rows):
            gather_copy(tok_next_ref, r, 1 - slot).wait()
            scatter_copy(dst_prev_ref, r, 1 - slot).wait()
        for r in range(rows):
            scatter_copy(dst_ref, r, slot).start(priority=r % 2)
        for r in range(rows):
            scatter_copy(dst_ref, r, slot).wait()


def _moe(hf, layout, w_gate, w_up, w_down, layer):
    block_e, n_valid, first, wpar, wnext, row_tok, row_dst = layout
    t = hf.shape[0]
    d = w_gate.shape[2]
    f = w_gate.shape[3]
    rows = MOE_ROWS
    nb = block_e.shape[0]
    tok3 = row_tok.reshape(nb, 1, rows)
    spare = (TOP_K * t + jnp.arange(2 * rows, dtype=jnp.int32)).reshape(2, 1, rows)
    dst3 = jnp.concatenate([spare, row_dst.reshape(nb, 1, rows)], axis=0)
    smem_blk = lambda fn: pl.BlockSpec((1, 1, rows), fn, memory_space=pltpu.SMEM)
    hbm = pl.BlockSpec(memory_space=pl.ANY)
    grid_spec = pltpu.PrefetchScalarGridSpec(
        num_scalar_prefetch=5,
        grid=(nb,),
        in_specs=[
            smem_blk(lambda i, *_: (i, 0, 0)),
            smem_blk(lambda i, *_: (jnp.minimum(i + 1, nb - 1), 0, 0)),
            smem_blk(lambda i, *_: (0, 0, 0)),
            smem_blk(lambda i, *_: (i + 1, 0, 0)),
            smem_blk(lambda i, *_: (i + 2, 0, 0)),
            hbm, hbm, hbm, hbm,
        ],
        out_specs=hbm,
        scratch_shapes=[
            pltpu.VMEM((2, rows, d // 2), jnp.int32),
            pltpu.VMEM((2, rows, d // 2), jnp.int32),
            pltpu.VMEM((2, d, f), F32), pltpu.VMEM((2, d, f), F32), pltpu.VMEM((2, f, d), F32),
            pltpu.VMEM((d, f), BF16), pltpu.VMEM((d, f), BF16), pltpu.VMEM((f, d), BF16),
            pltpu.SemaphoreType.DMA((2,)),
            pltpu.SemaphoreType.DMA((2,)),
            pltpu.SemaphoreType.DMA((2,)),
        ],
    )
    return pl.pallas_call(
        functools.partial(_moe_kernel, layer=layer),
        grid_spec=grid_spec,
        out_shape=jax.ShapeDtypeStruct((TOP_K * t + 2 * rows, d // 2), jnp.int32),
        compiler_params=_cparams(("arbitrary",), has_side_effects=True, disable_bounds_checks=True),
        name="moe_experts",
    )(block_e, n_valid, first, wpar, wnext, tok3, tok3, dst3, dst3, dst3, hf, w_gate, w_up, w_down)


def _block_layout(route_i, counts_f):
    t = route_i.shape[1]
    rows = MOE_ROWS
    n_assign = TOP_K * t
    counts = counts_f[:, 0].astype(jnp.int32)
    blocks_per_e = (counts + rows - 1) // rows
    blk_end = jnp.cumsum(blocks_per_e)
    blk_start = blk_end - blocks_per_e
    flat_e = route_i[0:TOP_K].reshape(n_assign)
    dest = blk_start[flat_e] * rows + route_i[TOP_K:2 * TOP_K].reshape(n_assign)
    nb = n_assign // rows + N_EXPERTS
    n_rows = nb * rows
    assign = jnp.arange(n_assign, dtype=jnp.int32)
    ridx = jnp.arange(n_rows, dtype=jnp.int32)
    spare = n_assign + ((ridx // rows) % 2) * rows + ridx % rows
    row_dst = spare.at[dest].set(assign, unique_indices=True, mode='promise_in_bounds')
    row_tok = jnp.where(row_dst < n_assign, row_dst % t, 0)
    n_valid = blk_end[-1].astype(jnp.int32)
    bidx = jnp.arange(nb, dtype=jnp.int32)
    block_e = jnp.sum((blk_end[None, :] <= bidx[:, None]).astype(jnp.int32), axis=1)
    block_e = jnp.minimum(block_e, N_EXPERTS - 1)
    last_e = block_e[jnp.maximum(n_valid - 1, 0)]
    block_e = jnp.where(bidx < n_valid, block_e, last_e)
    prev_e = jnp.concatenate([jnp.full((1,), -1, jnp.int32), block_e[:-1]])
    first = ((bidx < n_valid) & (block_e != prev_e)).astype(jnp.int32)
    wpar = (jnp.cumsum(first) + 1) % 2
    first_at = jnp.where(first == 1, bidx, nb)
    later = jnp.concatenate([lax.cummin(first_at[::-1])[::-1][1:], jnp.full((1,), nb, jnp.int32)])
    wnext = jnp.where(later < nb, block_e[jnp.minimum(later, nb - 1)], -1).astype(jnp.int32)
    return block_e, n_valid.reshape(1), first, wpar.astype(jnp.int32), wnext, row_tok, row_dst


def _combine_kernel(x_ref, y0_ref, y1_ref, rw_ref, g_ref, o_ref):
    w = rw_ref[...]
    y0 = _unpack_bf16_pairs(y0_ref[...])
    y1 = _unpack_bf16_pairs(y1_ref[...])
    o_ref[...] = x_ref[...] + g_ref[0] * (y0 * w[:, 0:1] + y1 * w[:, 1:2])


def _combine(x2, y2, route_w, g_f, seq):
    t, d = x2.shape
    tm = 256
    per_b = seq // tm
    n_blk = t // tm
    return pl.pallas_call(
        _combine_kernel,
        grid=(n_blk,),
        in_specs=[pl.BlockSpec((tm, d), lambda i: (i, 0)),
                  pl.BlockSpec((tm, d // 2), lambda i: (i, 0)),
                  pl.BlockSpec((tm, d // 2), lambda i: (n_blk + i, 0)),
                  pl.BlockSpec((tm, 8), lambda i: (i, 0)),
                  pl.BlockSpec((1, 1, d), lambda i: (i // per_b, 0, 0))],
        out_specs=pl.BlockSpec((tm, d), lambda i: (i, 0)),
        out_shape=jax.ShapeDtypeStruct((t, d), F32),
        compiler_params=_cparams(("arbitrary",)),
        name="moe_combine",
    )(x2, y2, y2, route_w, g_f)


def kernel(x, c, norm_mix, norm_ffn, w_ada, b_ada, w_in, conv_w, a_log, dt_bias, gdn_norm, diff_q_norm, diff_k_norm, lam_q1, lam_k1, lam_q2, lam_k2, diff_subln, dil_q_norm, dil_k_norm, w_out, w_router, router_bias, w_gate, w_up, w_down):
    batch, seq, d = x.shape
    depth = w_ada.shape[0]
    t = batch * seq
    x2 = x.reshape(t, d)
    mod = _ada(c, w_ada, b_ada).reshape(depth, batch, N_MOD, 1, d)
    ba_lo = 3 * GDN_HEADS * LANES + GDN_HEADS * LANES
    ba_hi = ba_lo + 2 * GDN_HEADS
    w_router_t = w_router.T.astype(BF16)
    for l in range(depth):
        sh_a, sc_a, g_a, sh_f, sc_f, g_f = (mod[l, :, k] for k in range(N_MOD))
        w_main = jnp.concatenate([w_in[l, :, :ba_lo], w_in[l, :, ba_hi:]], axis=1).astype(BF16)
        w_ba = jnp.pad(w_in[l, :, ba_lo:ba_hi], ((0, 0), (0, LANES - 2 * GDN_HEADS))).astype(BF16)
        proj, ba = _inproj(x2, norm_mix[l].reshape(1, d), sc_a, sh_a, w_main, w_ba, seq)
        o_a = _gdn(proj, ba, conv_w[l], a_log[l], dt_bias[l], gdn_norm[l], batch, seq)
        lam_init = 0.8 - 0.6 * math.exp(-0.3 * l)
        lam_vecs = jnp.stack([lam_q1[l], lam_k1[l], lam_q2[l], lam_k2[l]]).astype(F32)
        o_b = _diff(proj, diff_q_norm[l], diff_k_norm[l], diff_subln[l], lam_vecs, lam_init, batch, seq)
        o_c = _dil(proj, dil_q_norm[l], dil_k_norm[l], batch, seq)
        x2, hf, route_i, route_w, counts = _outproj(o_a, o_b, o_c, w_out[l].astype(BF16), x2, g_a,
                                                    norm_ffn[l].reshape(1, d), sc_f, sh_f, w_router_t, router_bias, seq)
        y2 = _moe(hf, _block_layout(route_i, counts), w_gate, w_up, w_down, l)
        x2 = _combine(x2, y2, route_w, g_f, seq)
    return x2.reshape(batch, seq, d)
```
